```python
import jax, jax.numpy as jnp
from jax import lax
import numpy as np

D_MODEL = 1024
BATCH = 2
SEQ = 16384
DEPTH = 4

CHUNK = 64
N_A_LAYERS = DEPTH // 2
N_B_LAYERS = DEPTH - N_A_LAYERS
N_HEADS = 16
HEAD_DIM = D_MODEL // N_HEADS
Q_BLOCK = 128
CONV_WIDTH = 31
N_EXPERTS = 32
TOP_K = 4
D_EXPERT = D_MODEL
SWIGLU_ALPHA = 1.702
SWIGLU_LIMIT = 7.0
EXPERT_BLOCK = 256
EPS = 1e-6

kernel_name = "yoco_conformer_fox_moe_trunk"


def rmsnorm(x, g):
    xf = x.astype(jnp.float32)
    y = xf * lax.rsqrt(jnp.mean(xf * xf, axis=-1, keepdims=True) + EPS)
    return (y * g.astype(jnp.float32)).astype(x.dtype)


def layernorm(x, g, b):
    xf = x.astype(jnp.float32)
    mu = jnp.mean(xf, axis=-1, keepdims=True)
    var = jnp.mean(jnp.square(xf - mu), axis=-1, keepdims=True)
    y = (xf - mu) * lax.rsqrt(var + EPS)
    return (y * g.astype(jnp.float32) + b.astype(jnp.float32)).astype(x.dtype)


def modulate(h, shift, scale):
    return h * (1 + scale[:, None, :]) + shift[:, None, :]


def conformer_conv(h, w_pw1, b_pw1, w_dw, b_dw, ln_g, ln_b, w_pw2, b_pw2):
    d = h.shape[-1]
    u = h @ w_pw1 + b_pw1
    u = u[..., :d] * jax.nn.sigmoid(u[..., d:])
    u = lax.conv_general_dilated(
        u, w_dw[:, None, :], window_strides=(1,), padding=[(CONV_WIDTH - 1, 0)],
        dimension_numbers=('NWC', 'WIO', 'NWC'), feature_group_count=d) + b_dw
    u = jax.nn.silu(layernorm(u, ln_g, ln_b))
    return u @ w_pw2 + b_pw2


def shared_kv(x, shift, scale, kv_norm_g, w_kvf, b_f, k_norm_g):
    b, s, d = x.shape
    h = modulate(rmsnorm(x, kv_norm_g), shift, scale)
    kvf = h @ w_kvf
    k = rmsnorm(kvf[..., :d].reshape(b, s, N_HEADS, HEAD_DIM), k_norm_g).transpose(0, 2, 1, 3)
    v = kvf[..., d:2 * d].reshape(b, s, N_HEADS, HEAD_DIM).transpose(0, 2, 1, 3)
    fz = kvf[..., 2 * d:].astype(jnp.float32) + b_f.astype(jnp.float32)
    cum_logf = jnp.cumsum(jax.nn.log_sigmoid(fz), axis=1).transpose(0, 2, 1)
    return k, v, cum_logf


def fox_attention(q, k, v, cum_logf):
    b, h, s, dh = q.shape
    nq = s // Q_BLOCK
    qb = q.reshape(b, h, nq, Q_BLOCK, dh).transpose(2, 0, 1, 3, 4)
    fb = cum_logf.reshape(b, h, nq, Q_BLOCK).transpose(2, 0, 1, 3)
    kpos = jnp.arange(s)
    inv_sqrt = 1.0 / float(np.sqrt(dh))

    def one_block(args):
        i, qi, fi = args
        sc = jnp.einsum('bhqd,bhkd->bhqk', qi, k, preferred_element_type=jnp.float32) * inv_sqrt
        sc = sc + fi[..., None] - cum_logf[:, :, None, :]
        qpos = i * Q_BLOCK + jnp.arange(Q_BLOCK)
        sc = jnp.where(kpos[None, :] <= qpos[:, None], sc, -jnp.inf)
        p = jax.nn.softmax(sc, axis=-1)
        return jnp.einsum('bhqk,bhkd->bhqd', p.astype(v.dtype), v)

    out = lax.map(one_block, (jnp.arange(nq), qb, fb))
    return out.transpose(1, 2, 0, 3, 4).reshape(b, h, s, dh)


def fox_mixer(h, w_qg, q_norm_g, w_o, k, v, cum_logf):
    b, s, d = h.shape
    qg = h @ w_qg
    q = rmsnorm(qg[..., :d].reshape(b, s, N_HEADS, HEAD_DIM), q_norm_g).transpose(0, 2, 1, 3)
    o = fox_attention(q, k, v, cum_logf).transpose(0, 2, 1, 3).reshape(b, s, d)
    return (o * jax.nn.sigmoid(qg[..., d:])) @ w_o


def moe_ffn(h, router_w, router_b, w_gu, b_gu, w_down, b_down):
    b, s, d = h.shape
    n = b * s
    t = h.reshape(n, d)
    logits = (t @ router_w).astype(jnp.float32) + router_b.astype(jnp.float32)
    top_v, top_i = lax.top_k(logits, TOP_K)
    gate = jax.nn.softmax(top_v, axis=-1)
    nk = n * TOP_K
    flat_e = top_i.reshape(nk).astype(jnp.int32)
    flat_tok = jnp.arange(nk, dtype=jnp.int32) // TOP_K
    flat_gate = gate.reshape(nk)
    order = jnp.argsort(flat_e)
    sorted_e = flat_e[order]
    counts = jnp.bincount(flat_e, length=N_EXPERTS).astype(jnp.int32)
    padded = (counts + EXPERT_BLOCK - 1) // EXPERT_BLOCK * EXPERT_BLOCK
    start = jnp.cumsum(counts) - counts
    pend = jnp.cumsum(padded)
    pstart = pend - padded
    dest = pstart[sorted_e] + jnp.arange(nk, dtype=jnp.int32) - start[sorted_e]
    n_blocks = -(-nk // EXPERT_BLOCK) + N_EXPERTS
    p_rows = n_blocks * EXPERT_BLOCK
    row_tok = jnp.full((p_rows,), n, jnp.int32).at[dest].set(flat_tok[order])
    row_w = jnp.zeros((p_rows,), jnp.float32).at[dest].set(flat_gate[order])
    block_e = jnp.minimum(
        jnp.searchsorted(pend, jnp.arange(n_blocks, dtype=jnp.int32) * EXPERT_BLOCK, side='right'),
        N_EXPERTS - 1)
    t_pad = jnp.concatenate([t, jnp.zeros((1, d), t.dtype)], axis=0)

    def expert_block(args):
        rows, wts, e = args
        xb = t_pad[rows]
        gu = xb @ w_gu[e] + b_gu[e]
        x_glu = jnp.minimum(gu[:, :D_EXPERT], SWIGLU_LIMIT)
        x_lin = jnp.clip(gu[:, D_EXPERT:], -SWIGLU_LIMIT, SWIGLU_LIMIT)
        act = x_glu * jax.nn.sigmoid(SWIGLU_ALPHA * x_glu) * (x_lin + 1)
        yb = act @ w_down[e] + b_down[e]
        return yb * wts[:, None].astype(yb.dtype)

    out = lax.map(expert_block, (row_tok.reshape(n_blocks, EXPERT_BLOCK),
                                 row_w.reshape(n_blocks, EXPERT_BLOCK), block_e))
    y = jax.ops.segment_sum(out.reshape(p_rows, d), row_tok, num_segments=n + 1)[:n]
    return y.reshape(b, s, d)


def setup_inputs(seed: int = 0) -> dict:
    key = jax.random.key(seed)
    ks = iter(jax.random.split(key, 40))
    D = D_MODEL
    E = N_EXPERTS
    F = D_EXPERT

    def nrm(shape, s):
        return jax.random.normal(next(ks), shape, jnp.float32) * s

    inputs = {
        'x': nrm((BATCH, SEQ, D), 1.0),
        'c': nrm((BATCH, D), 1.0),
        'mod_w': nrm((DEPTH, D, 6 * D), 0.5 * D ** -0.5),
        'mod_b': nrm((DEPTH, 6 * D), 0.02),
        'norm1_g': 1.0 + nrm((DEPTH, D), 0.05),
        'norm2_g': 1.0 + nrm((DEPTH, D), 0.05),
        'conv_w_pw1': nrm((N_A_LAYERS, D, 2 * D), D ** -0.5),
        'conv_b_pw1': nrm((N_A_LAYERS, 2 * D), 0.02),
        'conv_w_dw': nrm((N_A_LAYERS, CONV_WIDTH, D), CONV_WIDTH ** -0.5),
        'conv_b_dw': nrm((N_A_LAYERS, D), 0.02),
        'conv_ln_g': 1.0 + nrm((N_A_LAYERS, D), 0.05),
        'conv_ln_b': nrm((N_A_LAYERS, D), 0.02),
        'conv_w_pw2': nrm((N_A_LAYERS, D, D), D ** -0.5),
        'conv_b_pw2': nrm((N_A_LAYERS, D), 0.02),
        'kv_mod_w': nrm((D, 2 * D), 0.5 * D ** -0.5),
        'kv_mod_b': nrm((2 * D,), 0.02),
        'kv_norm_g': 1.0 + nrm((D,), 0.05),
    }
    w_kv = nrm((D, 2 * D), D ** -0.5)
    w_f = nrm((D, N_HEADS), 0.1 * D ** -0.5)
    inputs['w_kvf'] = jnp.concatenate([w_kv, w_f], axis=1)
    inputs['b_f'] = jnp.linspace(1.0, 6.0, N_HEADS, dtype=jnp.float32) + nrm((N_HEADS,), 0.1)
    inputs['k_norm_g'] = 1.0 + nrm((HEAD_DIM,), 0.05)
    inputs['attn_w_qg'] = nrm((N_B_LAYERS, D, 2 * D), D ** -0.5)
    inputs['q_norm_g'] = 1.0 + nrm((N_B_LAYERS, HEAD_DIM), 0.05)
    inputs['attn_w_o'] = nrm((N_B_LAYERS, D, D), D ** -0.5)
    inputs['moe_router_w'] = nrm((DEPTH, D, E), D ** -0.5)
    inputs['moe_router_b'] = nrm((DEPTH, E), 0.01)
    inputs['moe_w_gu'] = nrm((DEPTH, E, D, 2 * F), D ** -0.5)
    inputs['moe_b_gu'] = nrm((DEPTH, E, 2 * F), 0.02)
    inputs['moe_w_down'] = nrm((DEPTH, E, F, D), F ** -0.5)
    inputs['moe_b_down'] = nrm((DEPTH, E, D), 0.02)
    inputs['final_norm_g'] = 1.0 + nrm((D,), 0.05)
    return inputs


def reference(x, c, mod_w, mod_b, norm1_g, norm2_g,
              conv_w_pw1, conv_b_pw1, conv_w_dw, conv_b_dw, conv_ln_g, conv_ln_b,
              conv_w_pw2, conv_b_pw2, kv_mod_w, kv_mod_b, kv_norm_g,
              w_kvf, b_f, k_norm_g, attn_w_qg, q_norm_g, attn_w_o,
              moe_router_w, moe_router_b, moe_w_gu, moe_b_gu, moe_w_down, moe_b_down,
              final_norm_g):
    d = x.shape[-1]
    c_act = jax.nn.silu(c)
    k = v = cum_logf = None
    for l in range(DEPTH):
        mods = c_act @ mod_w[l] + mod_b[l]
        sh1, sc1, g1, sh2, sc2, g2 = [mods[:, i * d:(i + 1) * d] for i in range(6)]
        h = modulate(rmsnorm(x, norm1_g[l]), sh1, sc1)
        if l < N_A_LAYERS:
            y = conformer_conv(h, conv_w_pw1[l], conv_b_pw1[l], conv_w_dw[l], conv_b_dw[l],
                               conv_ln_g[l], conv_ln_b[l], conv_w_pw2[l], conv_b_pw2[l])
        else:
            lb = l - N_A_LAYERS
            y = fox_mixer(h, attn_w_qg[lb], q_norm_g[lb], attn_w_o[lb], k, v, cum_logf)
        x = x + g1[:, None, :] * y
        h = modulate(rmsnorm(x, norm2_g[l]), sh2, sc2)
        x = x + g2[:, None, :] * moe_ffn(h, moe_router_w[l], moe_router_b[l], moe_w_gu[l],
                                         moe_b_gu[l], moe_w_down[l], moe_b_down[l])
        if l == N_A_LAYERS - 1:
            kv_mods = c_act @ kv_mod_w + kv_mod_b
            k, v, cum_logf = shared_kv(x, kv_mods[:, :d], kv_mods[:, d:], kv_norm_g,
                                       w_kvf, b_f, k_norm_g)
    return rmsnorm(x, final_norm_g)
```

```python
import functools
import math

import jax
import jax.numpy as jnp
from jax import lax
from jax.experimental import pallas as pl
from jax.experimental.pallas import tpu as pltpu

N_HEADS = 16
HEAD_DIM = 64
CONV_WIDTH = 31
N_EXPERTS = 32
TOP_K = 4
SWIGLU_ALPHA = 1.702
SWIGLU_LIMIT = 7.0
EPS = 1e-6

LANES = 128
HALO = 32
EXPERT_ROWS = 256
VMEM_LIMIT = 56 * 1024 * 1024
LOG2E = 1.4426950408889634
NEG_BIG = -1e30
N_FPIECES = 3

F32 = jnp.float32
BF16 = jnp.bfloat16
HIGHEST = lax.Precision.HIGHEST


def _cparams(sem):
    return pltpu.CompilerParams(dimension_semantics=sem, vmem_limit_bytes=VMEM_LIMIT)


def _rms(x, g):
    return x * lax.rsqrt(jnp.mean(x * x, axis=-1, keepdims=True) + EPS) * g


def _split3(f):
    hi = f.astype(BF16).astype(F32)
    r1 = f - hi
    mid = r1.astype(BF16).astype(F32)
    lo = (r1 - mid).astype(BF16).astype(F32)
    return hi, mid, lo


def _mods_kernel(c_ref, w_ref, b_ref, o_ref):
    c = c_ref[...]
    ca = c * jax.nn.sigmoid(c)
    o_ref[0] = jnp.dot(ca, w_ref[0], precision=HIGHEST, preferred_element_type=F32) + b_ref[0]


def _mods(c8, w, b):
    n_l, d, m = w.shape
    tn = min(m, 1024)
    return pl.pallas_call(
        _mods_kernel,
        out_shape=jax.ShapeDtypeStruct((n_l, 8, m), F32),
        grid=(n_l, m // tn),
        in_specs=[
            pl.BlockSpec((8, d), lambda l, j: (0, 0)),
            pl.BlockSpec((1, d, tn), lambda l, j: (l, 0, j)),
            pl.BlockSpec((1, 1, tn), lambda l, j: (l, 0, j)),
        ],
        out_specs=pl.BlockSpec((1, 8, tn), lambda l, j: (l, 0, j)),
        compiler_params=_cparams(("arbitrary", "arbitrary")),
        name="mods",
    )(c8, w, b.reshape(n_l, 1, m))


def _conv_kernel(x_ref, mod_ref, ng_ref, w1_ref, b1_ref, wdw_ref, bdw_ref, lng_ref, lnb_ref,
                 w2_ref, b2_ref, o_ref, ubuf):
    t, d = x_ref.shape[1], x_ref.shape[2]

    @pl.when(pl.program_id(1) == 0)
    def _():
        ubuf[pl.ds(0, HALO), :] = jnp.zeros((HALO, d), F32)

    x = x_ref[0]
    mod = mod_ref[0]
    h = _rms(x, ng_ref[...]) * (1.0 + mod[1:2, :]) + mod[0:1, :]
    u = jnp.dot(h.astype(BF16), w1_ref[...], preferred_element_type=F32) + b1_ref[...]
    u = u[:, :d] * jax.nn.sigmoid(u[:, d:])
    ubuf[pl.ds(HALO, t), :] = u
    acc = jnp.zeros((t, d), F32)
    for j in range(CONV_WIDTH):
        acc = acc + ubuf[pl.ds(HALO - (CONV_WIDTH - 1) + j, t), :] * wdw_ref[j:j + 1, :]
    ubuf[pl.ds(0, HALO), :] = ubuf[pl.ds(t, HALO), :]
    acc = acc + bdw_ref[...]
    mu = jnp.mean(acc, axis=-1, keepdims=True)
    cen = acc - mu
    var = jnp.mean(cen * cen, axis=-1, keepdims=True)
    y = cen * lax.rsqrt(var + EPS) * lng_ref[...] + lnb_ref[...]
    y = y * jax.nn.sigmoid(y)
    y = jnp.dot(y.astype(BF16), w2_ref[...], preferred_element_type=F32) + b2_ref[...]
    o_ref[0] = x + mod[2:3, :] * y


def _conv_layer(x, mod, ng, w1, b1, wdw, bdw, lng, lnb, w2, b2, t):
    b, s, d = x.shape
    row = lambda a: a.reshape(1, -1)
    wdw_p = jnp.zeros((HALO, d), F32).at[:CONV_WIDTH].set(wdw)
    const = lambda shape: pl.BlockSpec(shape, lambda bi, si: (0,) * len(shape))
    return pl.pallas_call(
        _conv_kernel,
        out_shape=jax.ShapeDtypeStruct((b, s, d), F32),
        grid=(b, s // t),
        in_specs=[
            pl.BlockSpec((1, t, d), lambda bi, si: (bi, si, 0)),
            pl.BlockSpec((1, 6, d), lambda bi, si: (bi, 0, 0)),
            const((1, d)), const((d, 2 * d)), const((1, 2 * d)), const((HALO, d)), const((1, d)),
            const((1, d)), const((1, d)), const((d, d)), const((1, d)),
        ],
        out_specs=pl.BlockSpec((1, t, d), lambda bi, si: (bi, si, 0)),
        scratch_shapes=[pltpu.VMEM((t + HALO, d), F32)],
        compiler_params=_cparams(("arbitrary", "arbitrary")),
        name="conv_layer",
    )(x, mod, row(ng), w1.astype(BF16), row(b1), wdw_p, row(bdw), row(lng), row(lnb),
      w2.astype(BF16), row(b2))


def _route_kernel(x_ref, mod_ref, ng_ref, rw_ref, rb_ref, h_ref, gate_ref, meta_ref, cnt_ref):
    t = x_ref.shape[0]

    @pl.when(pl.program_id(0) == 0)
    def _():
        cnt_ref[...] = jnp.zeros(cnt_ref.shape, F32)

    mod = mod_ref[0]
    h = _rms(x_ref[...], ng_ref[...]) * (1.0 + mod[4:5, :]) + mod[3:4, :]
    h_ref[...] = h
    logits = jnp.dot(h, rw_ref[...], precision=HIGHEST, preferred_element_type=F32) + rb_ref[...]
    lane = lax.broadcasted_iota(jnp.int32, (t, LANES), 1).astype(F32)
    work = logits
    vals, idxs = [], []
    for _ in range(TOP_K):
        m = jnp.max(work, axis=-1, keepdims=True)
        idx = jnp.min(jnp.where(work == m, lane, float(LANES)), axis=-1, keepdims=True)
        vals.append(m)
        idxs.append(idx)
        work = jnp.where(lane == idx, -jnp.inf, work)
    exps = [jnp.exp(v - vals[0]) for v in vals]
    denom = exps[0] + exps[1] + exps[2] + exps[3]
    onehot = jnp.zeros((t, LANES), F32)
    for idx in idxs:
        onehot = onehot + (lane == idx).astype(F32)
    r_i = lax.broadcasted_iota(jnp.int32, (t, t), 0)
    c_i = lax.broadcasted_iota(jnp.int32, (t, t), 1)
    tri = (c_i < r_i).astype(BF16)
    base = jnp.dot(tri, onehot.astype(BF16), preferred_element_type=F32) + cnt_ref[0:1, :]
    gate_out = jnp.zeros((t, LANES), F32)
    meta = jnp.zeros((t, LANES), F32)
    for k in range(TOP_K):
        rank = jnp.sum(jnp.where(lane == idxs[k], base, 0.0), axis=-1, keepdims=True)
        gate_out = jnp.where(lane == k, exps[k] / denom, gate_out)
        meta = jnp.where(lane == k, rank, meta)
        meta = jnp.where(lane == TOP_K + k, idxs[k], meta)
    gate_ref[...] = gate_out
    meta_ref[...] = meta.astype(jnp.int32)
    cnt_ref[...] = cnt_ref[...] + jnp.sum(onehot, axis=0, keepdims=True)


def _route(x2, mod, ng, rw, rb, t, tiles_per_batch):
    n, d = x2.shape
    rw_p = jnp.zeros((d, LANES), F32).at[:, :N_EXPERTS].set(rw)
    rb_p = jnp.full((1, LANES), NEG_BIG, F32).at[0, :N_EXPERTS].set(rb)
    return pl.pallas_call(
        _route_kernel,
        out_shape=(
            jax.ShapeDtypeStruct((n, d), F32),
            jax.ShapeDtypeStruct((n, LANES), F32),
            jax.ShapeDtypeStruct((n, LANES), jnp.int32),
            jax.ShapeDtypeStruct((8, LANES), F32),
        ),
        grid=(n // t,),
        in_specs=[
            pl.BlockSpec((t, d), lambda i: (i, 0)),
            pl.BlockSpec((1, 6, d), lambda i: (i // tiles_per_batch, 0, 0)),
            pl.BlockSpec((1, d), lambda i: (0, 0)),
            pl.BlockSpec((d, LANES), lambda i: (0, 0)),
            pl.BlockSpec((1, LANES), lambda i: (0, 0)),
        ],
        out_specs=(
            pl.BlockSpec((t, d), lambda i: (i, 0)),
            pl.BlockSpec((t, LANES), lambda i: (i, 0)),
            pl.BlockSpec((t, LANES), lambda i: (i, 0)),
            pl.BlockSpec((8, LANES), lambda i: (0, 0)),
        ),
        compiler_params=_cparams(("arbitrary",)),
        name="moe_route",
    )(x2, mod, ng.reshape(1, d), rw_p, rb_p)


def _dispatch_kernel(h_ref, dest_hbm, xs_in, xs_hbm, idx_smem, sem_idx, sem_rows):
    del xs_in
    t = h_ref.shape[0]
    i = pl.program_id(0)
    cp = pltpu.make_async_copy(dest_hbm.at[i], idx_smem, sem_idx)
    cp.start()
    cp.wait()

    def row_copy(tok, dst):
        return pltpu.make_async_copy(h_ref.at[pl.ds(tok, 1), :], xs_hbm.at[pl.ds(dst, 1), :], sem_rows)

    def issue(tok, carry):
        for k in range(TOP_K):
            row_copy(tok, idx_smem[k * t + tok]).start()
        return carry

    lax.fori_loop(0, t, issue, 0)

    def drain(tok, carry):
        for k in range(TOP_K):
            row_copy(tok, idx_smem[k * t + tok]).wait()
        return carry

    lax.fori_loop(0, t, drain, 0)


def _dispatch(h, dest_tiles, p_rows, t):
    n, d = h.shape
    xs0 = jnp.zeros((p_rows, d), F32)
    return pl.pallas_call(
        _dispatch_kernel,
        out_shape=jax.ShapeDtypeStruct((p_rows, d), F32),
        grid=(n // t,),
        in_specs=[
            pl.BlockSpec((t, d), lambda i: (i, 0)),
            pl.BlockSpec(memory_space=pl.ANY),
            pl.BlockSpec(memory_space=pl.ANY),
        ],
        out_specs=pl.BlockSpec(memory_space=pl.ANY),
        scratch_shapes=[pltpu.SMEM((TOP_K * t,), jnp.int32), pltpu.SemaphoreType.DMA,
                        pltpu.SemaphoreType.DMA],
        input_output_aliases={2: 0},
        compiler_params=_cparams(("arbitrary",)),
        name="moe_dispatch",
    )(h, dest_tiles, xs0)


def _experts_kernel(be_ref, nu_ref, xs_ref, wgu_ref, bgu_ref, wdn_ref, bdn_ref, ys_ref, wgu_bf, wdn_bf):
    b = pl.program_id(0)
    f = wdn_ref.shape[1]
    e = be_ref[b]
    e_prev = be_ref[jnp.maximum(b - 1, 0)]

    @pl.when(jnp.logical_or(b == 0, e != e_prev))
    def _():
        wgu_bf[...] = wgu_ref[0].astype(BF16)
        wdn_bf[...] = wdn_ref[0].astype(BF16)

    @pl.when(b < nu_ref[0])
    def _():
        x = xs_ref[...].astype(BF16)
        gu = jnp.dot(x, wgu_bf[...], preferred_element_type=F32) + bgu_ref[0]
        x_glu = jnp.minimum(gu[:, :f], SWIGLU_LIMIT)
        x_lin = jnp.clip(gu[:, f:], -SWIGLU_LIMIT, SWIGLU_LIMIT)
        act = x_glu * jax.nn.sigmoid(SWIGLU_ALPHA * x_glu) * (x_lin + 1.0)
        ys_ref[...] = jnp.dot(act.astype(BF16), wdn_bf[...], preferred_element_type=F32) + bdn_ref[0]


def _experts(xs, block_e, n_used, w_gu, b_gu, w_dn, b_dn):
    p_rows, d = xs.shape
    n_e, _, f2 = w_gu.shape
    f = f2 // 2
    n_blocks = p_rows // EXPERT_ROWS
    row_map = lambda b, be, nu: (jnp.minimum(b, nu[0] - 1), 0)
    exp_map = lambda b, be, nu: (be[b], 0, 0)
    return pl.pallas_call(
        _experts_kernel,
        out_shape=jax.ShapeDtypeStruct((p_rows, d), F32),
        grid_spec=pltpu.PrefetchScalarGridSpec(
            num_scalar_prefetch=2,
            grid=(n_blocks,),
            in_specs=[
                pl.BlockSpec((EXPERT_ROWS, d), row_map),
                pl.BlockSpec((1, d, f2), exp_map),
                pl.BlockSpec((1, 1, f2), exp_map),
                pl.BlockSpec((1, f, d), exp_map),
                pl.BlockSpec((1, 1, d), exp_map),
            ],
            out_specs=pl.BlockSpec((EXPERT_ROWS, d), row_map),
            scratch_shapes=[pltpu.VMEM((d, f2), BF16), pltpu.VMEM((f, d), BF16)],
        ),
        compiler_params=_cparams(("arbitrary",)),
        name="moe_experts",
    )(block_e, n_used, xs, w_gu, b_gu.reshape(n_e, 1, f2), w_dn, b_dn.reshape(n_e, 1, d))


def _combine_kernel(x_ref, gate_ref, mod_ref, dest_hbm, ys_hbm, o_ref, buf, idx_smem, sem_idx, sem_rows):
    t = x_ref.shape[0]
    i = pl.program_id(0)
    cp = pltpu.make_async_copy(dest_hbm.at[i], idx_smem, sem_idx)
    cp.start()
    cp.wait()

    def row_copy(k, tok, src):
        return pltpu.make_async_copy(ys_hbm.at[pl.ds(src, 1), :], buf.at[k, pl.ds(tok, 1), :], sem_rows)

    def issue(tok, carry):
        for k in range(TOP_K):
            row_copy(k, tok, idx_smem[k * t + tok]).start()
        return carry

    lax.fori_loop(0, t, issue, 0)

    def drain(tok, carry):
        for k in range(TOP_K):
            row_copy(k, tok, idx_smem[k * t + tok]).wait()
        return carry

    lax.fori_loop(0, t, drain, 0)

    gate = gate_ref[...]
    y = gate[:, 0:1] * buf[0]
    for k in range(1, TOP_K):
        y = y + gate[:, k:k + 1] * buf[k]
    o_ref[...] = x_ref[...] + mod_ref[0][5:6, :] * y


def _combine(x2, gates, mod, dest_tiles, ys, t, tiles_per_batch):
    n, d = x2.shape
    return pl.pallas_call(
        _combine_kernel,
        out_shape=jax.ShapeDtypeStruct((n, d), F32),
        grid=(n // t,),
        in_specs=[
            pl.BlockSpec((t, d), lambda i: (i, 0)),
            pl.BlockSpec((t, LANES), lambda i: (i, 0)),
            pl.BlockSpec((1, 6, d), lambda i: (i // tiles_per_batch, 0, 0)),
            pl.BlockSpec(memory_space=pl.ANY),
            pl.BlockSpec(memory_space=pl.ANY),
        ],
        out_specs=pl.BlockSpec((t, d), lambda i: (i, 0)),
        scratch_shapes=[pltpu.VMEM((TOP_K, t, d), F32), pltpu.SMEM((TOP_K * t,), jnp.int32),
                        pltpu.SemaphoreType.DMA, pltpu.SemaphoreType.DMA],
        compiler_params=_cparams(("arbitrary",)),
        name="moe_combine",
    )(x2, gates, mod, dest_tiles, ys)


def _moe_layer(x, mod, ng, rw, rb, w_gu, b_gu, w_dn, b_dn, t):
    b, s, d = x.shape
    n = b * s
    x2 = x.reshape(n, d)
    tiles_per_batch = s // t
    h, gates, meta, cnt = _route(x2, mod, ng, rw, rb, t, tiles_per_batch)
    counts = cnt[0, :N_EXPERTS].astype(jnp.int32)
    padded = (counts + EXPERT_ROWS - 1) // EXPERT_ROWS * EXPERT_ROWS
    pend = jnp.cumsum(padded)
    pstart = pend - padded
    n_blocks = -(-(n * TOP_K) // EXPERT_ROWS) + N_EXPERTS
    p_rows = n_blocks * EXPERT_ROWS
    rank = meta[:, :TOP_K]
    eidx = meta[:, TOP_K:2 * TOP_K]
    dest = pstart[eidx] + rank
    dest_tiles = dest.reshape(n // t, t, TOP_K).transpose(0, 2, 1).reshape(n // t, TOP_K * t)
    block_e = jnp.minimum(
        jnp.searchsorted(pend, jnp.arange(n_blocks, dtype=jnp.int32) * EXPERT_ROWS, side='right'),
        N_EXPERTS - 1).astype(jnp.int32)
    n_used = (pend[-1:] // EXPERT_ROWS).astype(jnp.int32)
    xs = _dispatch(h, dest_tiles, p_rows, t)
    ys = _experts(xs, block_e, n_used, w_gu, b_gu, w_dn, b_dn)
    out = _combine(x2, gates, mod, dest_tiles, ys, t, tiles_per_batch)
    return out.reshape(b, s, d)


def _kv_kernel(x_ref, mod_ref, ng_ref, wkv_ref, wf_ref, bf_ref, kg_ref, hsum_ref,
               kt_ref, v_ref, qf_ref, carry):
    t, d = x_ref.shape[1], x_ref.shape[2]
    n_pairs = N_HEADS // 2

    @pl.when(pl.program_id(1) == 0)
    def _():
        carry[...] = jnp.zeros(carry.shape, F32)

    mod = mod_ref[0]
    h = _rms(x_ref[0], ng_ref[...]) * (1.0 + mod[1:2, :]) + mod[0:1, :]
    kv = jnp.dot(h.astype(BF16), wkv_ref[...], preferred_element_type=F32)
    k = kv[:, :d]
    v_ref[0] = kv[:, d:].astype(BF16)
    ms = jnp.dot((k * k).astype(BF16), hsum_ref[...], preferred_element_type=F32)
    k = k * lax.rsqrt(ms + EPS) * kg_ref[...]
    kt = k.T
    for hp in range(n_pairs):
        kt_ref[0, hp, 0, pl.ds(0, LANES), :] = kt[hp * LANES:(hp + 1) * LANES, :].astype(BF16)

    fz = jnp.dot(h, wf_ref[...], precision=HIGHEST, preferred_element_type=F32) + bf_ref[...]
    ls = jax.nn.log_sigmoid(fz)
    r_i = lax.broadcasted_iota(jnp.int32, (t, t), 0)
    c_i = lax.broadcasted_iota(jnp.int32, (t, t), 1)
    tri = (c_i <= r_i).astype(F32)
    cum = jnp.dot(tri, ls, precision=HIGHEST, preferred_element_type=F32) + carry[0:1, :]
    carry[...] = jnp.broadcast_to(cum[t - 1:t, :], carry.shape)
    f2 = cum * LOG2E
    f2t = f2.T
    lane = lax.broadcasted_iota(jnp.int32, (t, LANES), 1)
    sub = lax.broadcasted_iota(jnp.int32, (LANES, t), 0)
    q_pieces = _split3(f2)
    k_pieces = _split3(-f2t)
    for hp in range(n_pairs):
        q_aug = jnp.zeros((t, LANES), F32)
        k_aug = jnp.zeros((LANES, t), F32)
        for hh in range(2):
            head = 2 * hp + hh
            o = hh * 2 * N_FPIECES
            for p in range(N_FPIECES):
                q_aug = jnp.where(lane == o + p, q_pieces[p][:, head:head + 1], q_aug)
                q_aug = jnp.where(lane == o + N_FPIECES + p, 1.0, q_aug)
                k_aug = jnp.where(sub == o + p, 1.0, k_aug)
                k_aug = jnp.where(sub == o + N_FPIECES + p, k_pieces[p][head:head + 1, :], k_aug)
        qf_ref[0, hp] = q_aug.astype(BF16)
        kt_ref[0, hp, 0, pl.ds(LANES, LANES), :] = k_aug.astype(BF16)


def _shared_kv(x, kvmod, ng, w_kvf, b_f, k_norm_g, t):
    b, s, d = x.shape
    n_pairs = N_HEADS // 2
    wkv = w_kvf[:, :2 * d].astype(BF16)
    wf = jnp.zeros((d, LANES), F32).at[:, :N_HEADS].set(w_kvf[:, 2 * d:])
    bf = jnp.zeros((1, LANES), F32).at[0, :N_HEADS].set(b_f)
    kg = jnp.tile(k_norm_g, N_HEADS).reshape(1, d)
    head_of = jnp.arange(d) // HEAD_DIM
    hsum = ((head_of[:, None] == head_of[None, :]).astype(F32) / HEAD_DIM).astype(BF16)
    const = lambda shape: pl.BlockSpec(shape, lambda bi, si: (0,) * len(shape))
    return pl.pallas_call(
        _kv_kernel,
        out_shape=(
            jax.ShapeDtypeStruct((b, n_pairs, s // t, 2 * LANES, t), BF16),
            jax.ShapeDtypeStruct((b, s, d), BF16),
            jax.ShapeDtypeStruct((b, n_pairs, s, LANES), BF16),
        ),
        grid=(b, s // t),
        in_specs=[
            pl.BlockSpec((1, t, d), lambda bi, si: (bi, si, 0)),
            pl.BlockSpec((1, 2, d), lambda bi, si: (bi, 0, 0)),
            const((1, d)), const((d, 2 * d)), const((d, LANES)), const((1, LANES)), const((1, d)),
            const((d, d)),
        ],
        out_specs=(
            pl.BlockSpec((1, n_pairs, 1, 2 * LANES, t), lambda bi, si: (bi, 0, si, 0, 0)),
            pl.BlockSpec((1, t, d), lambda bi, si: (bi, si, 0)),
            pl.BlockSpec((1, n_pairs, t, LANES), lambda bi, si: (bi, 0, si, 0)),
        ),
        scratch_shapes=[pltpu.VMEM((8, LANES), F32)],
        compiler_params=_cparams(("arbitrary", "arbitrary")),
        name="shared_kv",
    )(x, kvmod, ng.reshape(1, d), wkv, wf, bf, kg, hsum), hsum


def _qg_kernel(x_ref, mod_ref, ng_ref, w_ref, qg_ref, hsum_ref, q_ref, g_ref):
    d = x_ref.shape[2]
    mod = mod_ref[0]
    h = _rms(x_ref[0], ng_ref[...]) * (1.0 + mod[1:2, :]) + mod[0:1, :]
    qg = jnp.dot(h.astype(BF16), w_ref[...], preferred_element_type=F32)
    q = qg[:, :d]
    ms = jnp.dot((q * q).astype(BF16), hsum_ref[...], preferred_element_type=F32)
    q_ref[0] = (q * lax.rsqrt(ms + EPS) * qg_ref[...] * (LOG2E / math.sqrt(HEAD_DIM))).astype(BF16)
    g_ref[0] = jax.nn.sigmoid(qg[:, d:]).astype(BF16)


def _qg(x, mod, ng, w_qg, q_norm_g, hsum, t):
    b, s, d = x.shape
    const = lambda shape: pl.BlockSpec(shape, lambda bi, si: (0,) * len(shape))
    tile = pl.BlockSpec((1, t, d), lambda bi, si: (bi, si, 0))
    return pl.pallas_call(
        _qg_kernel,
        out_shape=(jax.ShapeDtypeStruct((b, s, d), BF16), jax.ShapeDtypeStruct((b, s, d), BF16)),
        grid=(b, s // t),
        in_specs=[tile, pl.BlockSpec((1, 6, d), lambda bi, si: (bi, 0, 0)),
                  const((1, d)), const((d, 2 * d)), const((1, d)), const((d, d))],
        out_specs=(tile, tile),
        compiler_params=_cparams(("arbitrary", "arbitrary")),
        name="attn_qg",
    )(x, mod, ng.reshape(1, d), w_qg.astype(BF16), jnp.tile(q_norm_g, N_HEADS).reshape(1, d), hsum)


def _attn_kernel(q_ref, qf_ref, kt_ref, v_ref, o_ref, qa_scr, m_scr, l_scr, acc_scr):
    tq = q_ref.shape[1]
    tk = kt_ref.shape[4]
    i = pl.program_id(2)
    lane = lax.broadcasted_iota(jnp.int32, (tq, LANES), 1)
    q2 = q_ref[0]
    qf = qf_ref[0, 0]
    zero = jnp.zeros((), BF16)
    n_aug = 2 * N_FPIECES
    qa_scr[0, :, pl.ds(0, LANES)] = jnp.where(lane < HEAD_DIM, q2, zero)
    qa_scr[0, :, pl.ds(LANES, LANES)] = jnp.where(lane < n_aug, qf, zero)
    qa_scr[1, :, pl.ds(0, LANES)] = jnp.where(lane >= HEAD_DIM, q2, zero)
    qa_scr[1, :, pl.ds(LANES, LANES)] = jnp.where(jnp.logical_and(lane >= n_aug, lane < 2 * n_aug), qf, zero)
    m_scr[...] = jnp.full(m_scr.shape, NEG_BIG, F32)
    l_scr[...] = jnp.zeros(l_scr.shape, F32)
    acc_scr[...] = jnp.zeros(acc_scr.shape, F32)

    def tile(j, masked):
        kt = kt_ref[0, 0, j]
        vv = v_ref[0, pl.ds(pl.multiple_of(j * tk, tk), tk), :]
        for hh in range(2):
            s = jnp.dot(qa_scr[hh], kt, preferred_element_type=F32)
            if masked:
                r_i = lax.broadcasted_iota(jnp.int32, (tq, tk), 0)
                c_i = lax.broadcasted_iota(jnp.int32, (tq, tk), 1)
                s = jnp.where(c_i <= r_i, s, NEG_BIG)
            m_prev = m_scr[hh]
            m_new = jnp.maximum(m_prev, jnp.max(s, axis=-1, keepdims=True))
            alpha = jnp.exp2(m_prev - m_new)
            p = jnp.exp2(s - m_new[:, 0:1])
            l_scr[hh] = alpha * l_scr[hh] + jnp.sum(p, axis=-1, keepdims=True)
            acc_scr[hh] = alpha * acc_scr[hh] + jnp.dot(p.astype(BF16), vv, preferred_element_type=F32)
            m_scr[hh] = m_new

    def full_tile(j, carry):
        tile(j, False)
        return carry

    lax.fori_loop(0, i, full_tile, 0)
    tile(i, True)
    o0 = acc_scr[0] / l_scr[0]
    o1 = acc_scr[1] / l_scr[1]
    o_ref[0] = jnp.where(lane < HEAD_DIM, o0, o1).astype(BF16)


def _attention(q, qf, kt, v, t):
    b, s, d = q.shape
    n_pairs = N_HEADS // 2
    nkv = s // t
    return pl.pallas_call(
        _attn_kernel,
        out_shape=jax.ShapeDtypeStruct((b, s, d), BF16),
        grid=(b, n_pairs, s // t),
        in_specs=[
            pl.BlockSpec((1, t, LANES), lambda bi, hp, i: (bi, i, hp)),
            pl.BlockSpec((1, 1, t, LANES), lambda bi, hp, i: (bi, hp, i, 0)),
            pl.BlockSpec((1, 1, nkv, 2 * LANES, t), lambda bi, hp, i: (bi, hp, 0, 0, 0)),
            pl.BlockSpec((1, s, LANES), lambda bi, hp, i: (bi, 0, hp)),
        ],
        out_specs=pl.BlockSpec((1, t, LANES), lambda bi, hp, i: (bi, i, hp)),
        scratch_shapes=[
            pltpu.VMEM((2, t, 2 * LANES), BF16),
            pltpu.VMEM((2, t, LANES), F32),
            pltpu.VMEM((2, t, LANES), F32),
            pltpu.VMEM((2, t, LANES), F32),
        ],
        compiler_params=_cparams(("arbitrary", "arbitrary", "arbitrary")),
        name="fox_attention",
    )(q, qf, kt, v)


def _attn_out_kernel(x_ref, o_ref, g_ref, mod_ref, w_ref, out_ref):
    og = o_ref[0] * g_ref[0]
    y = jnp.dot(og, w_ref[...], preferred_element_type=F32)
    out_ref[0] = x_ref[0] + mod_ref[0][2:3, :] * y


def _attn_out(x, o, g, mod, w_o, t):
    b, s, d = x.shape
    tile = pl.BlockSpec((1, t, d), lambda bi, si: (bi, si, 0))
    return pl.pallas_call(
        _attn_out_kernel,
        out_shape=jax.ShapeDtypeStruct((b, s, d), F32),
        grid=(b, s // t),
        in_specs=[tile, tile, tile, pl.BlockSpec((1, 6, d), lambda bi, si: (bi, 0, 0)),
                  pl.BlockSpec((d, d), lambda bi, si: (0, 0))],
        out_specs=tile,
        compiler_params=_cparams(("arbitrary", "arbitrary")),
        name="attn_out",
    )(x, o, g, mod, w_o.astype(BF16))


def _final_norm_kernel(x_ref, g_ref, o_ref):
    o_ref[...] = _rms(x_ref[...], g_ref[...])


def _final_norm(x2, g, t):
    n, d = x2.shape
    return pl.pallas_call(
        _final_norm_kernel,
        out_shape=jax.ShapeDtypeStruct((n, d), F32),
        grid=(n // t,),
        in_specs=[pl.BlockSpec((t, d), lambda i: (i, 0)), pl.BlockSpec((1, d), lambda i: (0, 0))],
        out_specs=pl.BlockSpec((t, d), lambda i: (i, 0)),
        compiler_params=_cparams(("arbitrary",)),
        name="final_norm",
    )(x2, g.reshape(1, d))


def kernel(x, c, mod_w, mod_b, norm1_g, norm2_g, conv_w_pw1, conv_b_pw1, conv_w_dw, conv_b_dw, conv_ln_g, conv_ln_b, conv_w_pw2, conv_b_pw2, kv_mod_w, kv_mod_b, kv_norm_g, w_kvf, b_f, k_norm_g, attn_w_qg, q_norm_g, attn_w_o, moe_router_w, moe_router_b, moe_w_gu, moe_b_gu, moe_w_down, moe_b_down, final_norm_g):
    b, s, d = x.shape
    depth = mod_w.shape[0]
    n_a = conv_w_pw1.shape[0]
    t = min(512, s)
    c8 = jnp.zeros((8, d), F32).at[:b].set(c)
    mods = _mods(c8, mod_w, mod_b)[:, :b].reshape(depth, b, 6, d)
    kvmod = _mods(c8, kv_mod_w[None], kv_mod_b[None])[0, :b].reshape(b, 2, d)
    kt = v = qf = hsum = None
    for l in range(depth):
        if l < n_a:
            x = _conv_layer(x, mods[l], norm1_g[l], conv_w_pw1[l], conv_b_pw1[l], conv_w_dw[l],
                            conv_b_dw[l], conv_ln_g[l], conv_ln_b[l], conv_w_pw2[l], conv_b_pw2[l], t)
        else:
            lb = l - n_a
            q, g = _qg(x, mods[l], norm1_g[l], attn_w_qg[lb], q_norm_g[lb], hsum, t)
            o = _attention(q, qf, kt, v, t)
            x = _attn_out(x, o, g, mods[l], attn_w_o[lb], t)
        x = _moe_layer(x, mods[l], norm2_g[l], moe_router_w[l], moe_router_b[l], moe_w_gu[l],
                       moe_b_gu[l], moe_w_down[l], moe_b_down[l], t)
        if l == n_a - 1:
            (kt, v, qf), hsum = _shared_kv(x, kvmod, kv_norm_g, w_kvf, b_f, k_norm_g, t)
    return _final_norm(x.reshape(b * s, d), final_norm_g, t).reshape(b, s, d)
```

```python
import functools
import math

import jax
import jax.numpy as jnp
from jax import lax
from jax.experimental import pallas as pl
from jax.experimental.pallas import tpu as pltpu

N_HEADS = 16
HEAD_DIM = 64
CONV_WIDTH = 31
N_EXPERTS = 32
TOP_K = 4
SWIGLU_ALPHA = 1.702
SWIGLU_LIMIT = 7.0
EPS = 1e-6

LANES = 128
HALO = 32
EXPERT_ROWS = 256
VMEM_LIMIT = 56 * 1024 * 1024
LOG2E = 1.4426950408889634
NEG_BIG = -1e30
N_FPIECES = 3
DIRECT_EXP_LIMIT = 60.0

F32 = jnp.float32
BF16 = jnp.bfloat16
HIGHEST = lax.Precision.HIGHEST


def _cparams(sem):
    return pltpu.CompilerParams(dimension_semantics=sem, vmem_limit_bytes=VMEM_LIMIT)


def _rms(x, g):
    return x * lax.rsqrt(jnp.mean(x * x, axis=-1, keepdims=True) + EPS) * g


def _split3(f):
    hi = f.astype(BF16).astype(F32)
    r1 = f - hi
    mid = r1.astype(BF16).astype(F32)
    lo = (r1 - mid).astype(BF16).astype(F32)
    return hi, mid, lo


def _mods_kernel(c_ref, w_ref, b_ref, o_ref):
    c = c_ref[...]
    ca = c * jax.nn.sigmoid(c)
    o_ref[0] = jnp.dot(ca, w_ref[0], precision=HIGHEST, preferred_element_type=F32) + b_ref[0]


def _mods(c8, w, b):
    n_l, d, m = w.shape
    tn = min(m, 1024)
    return pl.pallas_call(
        _mods_kernel,
        out_shape=jax.ShapeDtypeStruct((n_l, 8, m), F32),
        grid=(n_l, m // tn),
        in_specs=[
            pl.BlockSpec((8, d), lambda l, j: (0, 0)),
            pl.BlockSpec((1, d, tn), lambda l, j: (l, 0, j)),
            pl.BlockSpec((1, 1, tn), lambda l, j: (l, 0, j)),
        ],
        out_specs=pl.BlockSpec((1, 8, tn), lambda l, j: (l, 0, j)),
        compiler_params=_cparams(("arbitrary", "arbitrary")),
        name="mods",
    )(c8, w, b.reshape(n_l, 1, m))


def _conv_kernel(x_ref, mod_ref, ng_ref, w1_ref, b1_ref, wdw_ref, bdw_ref, lng_ref, lnb_ref,
                 w2_ref, b2_ref, o_ref, ubuf):
    t, d = x_ref.shape[1], x_ref.shape[2]

    @pl.when(pl.program_id(1) == 0)
    def _():
        ubuf[pl.ds(0, HALO), :] = jnp.zeros((HALO, d), F32)

    x = x_ref[0]
    mod = mod_ref[0]
    h = _rms(x, ng_ref[...]) * (1.0 + mod[1:2, :]) + mod[0:1, :]
    u = jnp.dot(h.astype(BF16), w1_ref[...], preferred_element_type=F32) + b1_ref[...]
    u = u[:, :d] * jax.nn.sigmoid(u[:, d:])
    ubuf[pl.ds(HALO, t), :] = u
    acc = jnp.zeros((t, d), F32)
    for j in range(CONV_WIDTH):
        acc = acc + ubuf[pl.ds(HALO - (CONV_WIDTH - 1) + j, t), :] * wdw_ref[j:j + 1, :]
    ubuf[pl.ds(0, HALO), :] = ubuf[pl.ds(t, HALO), :]
    acc = acc + bdw_ref[...]
    mu = jnp.mean(acc, axis=-1, keepdims=True)
    cen = acc - mu
    var = jnp.mean(cen * cen, axis=-1, keepdims=True)
    y = cen * lax.rsqrt(var + EPS) * lng_ref[...] + lnb_ref[...]
    y = y * jax.nn.sigmoid(y)
    y = jnp.dot(y.astype(BF16), w2_ref[...], preferred_element_type=F32) + b2_ref[...]
    o_ref[0] = x + mod[2:3, :] * y


def _conv_layer(x, mod, ng, w1, b1, wdw, bdw, lng, lnb, w2, b2, t):
    b, s, d = x.shape
    row = lambda a: a.reshape(1, -1)
    wdw_p = jnp.zeros((HALO, d), F32).at[:CONV_WIDTH].set(wdw)
    const = lambda shape: pl.BlockSpec(shape, lambda bi, si: (0,) * len(shape))
    return pl.pallas_call(
        _conv_kernel,
        out_shape=jax.ShapeDtypeStruct((b, s, d), F32),
        grid=(b, s // t),
        in_specs=[
            pl.BlockSpec((1, t, d), lambda bi, si: (bi, si, 0)),
            pl.BlockSpec((1, 6, d), lambda bi, si: (bi, 0, 0)),
            const((1, d)), const((d, 2 * d)), const((1, 2 * d)), const((HALO, d)), const((1, d)),
            const((1, d)), const((1, d)), const((d, d)), const((1, d)),
        ],
        out_specs=pl.BlockSpec((1, t, d), lambda bi, si: (bi, si, 0)),
        scratch_shapes=[pltpu.VMEM((t + HALO, d), F32)],
        compiler_params=_cparams(("arbitrary", "arbitrary")),
        name="conv_layer",
    )(x, mod, row(ng), w1.astype(BF16), row(b1), wdw_p, row(bdw), row(lng), row(lnb),
      w2.astype(BF16), row(b2))


def _route_kernel(x_ref, mod_ref, ng_ref, rw_ref, rb_ref, h_ref, gate_ref, meta_ref, cnt_ref):
    t = x_ref.shape[0]

    @pl.when(pl.program_id(0) == 0)
    def _():
        cnt_ref[...] = jnp.zeros(cnt_ref.shape, F32)

    mod = mod_ref[0]
    h = _rms(x_ref[...], ng_ref[...]) * (1.0 + mod[4:5, :]) + mod[3:4, :]
    h_ref[...] = h
    logits = jnp.dot(h, rw_ref[...], precision=HIGHEST, preferred_element_type=F32) + rb_ref[...]
    lane = lax.broadcasted_iota(jnp.int32, (t, LANES), 1).astype(F32)
    work = logits
    vals, idxs = [], []
    for _ in range(TOP_K):
        m = jnp.max(work, axis=-1, keepdims=True)
        idx = jnp.min(jnp.where(work == m, lane, float(LANES)), axis=-1, keepdims=True)
        vals.append(m)
        idxs.append(idx)
        work = jnp.where(lane == idx, -jnp.inf, work)
    exps = [jnp.exp(v - vals[0]) for v in vals]
    denom = exps[0] + exps[1] + exps[2] + exps[3]
    onehot = jnp.zeros((t, LANES), F32)
    for idx in idxs:
        onehot = onehot + (lane == idx).astype(F32)
    r_i = lax.broadcasted_iota(jnp.int32, (t, t), 0)
    c_i = lax.broadcasted_iota(jnp.int32, (t, t), 1)
    tri = (c_i < r_i).astype(BF16)
    base = jnp.dot(tri, onehot.astype(BF16), preferred_element_type=F32) + cnt_ref[0:1, :]
    gate_out = jnp.zeros((t, LANES), F32)
    meta = jnp.zeros((t, LANES), F32)
    for k in range(TOP_K):
        rank = jnp.sum(jnp.where(lane == idxs[k], base, 0.0), axis=-1, keepdims=True)
        gate_out = jnp.where(lane == k, exps[k] / denom, gate_out)
        meta = jnp.where(lane == k, rank, meta)
        meta = jnp.where(lane == TOP_K + k, idxs[k], meta)
    gate_ref[...] = gate_out
    meta_ref[...] = meta.astype(jnp.int32)
    cnt_ref[...] = cnt_ref[...] + jnp.sum(onehot, axis=0, keepdims=True)


def _route(x2, mod, ng, rw, rb, t, tiles_per_batch):
    n, d = x2.shape
    rw_p = jnp.zeros((d, LANES), F32).at[:, :N_EXPERTS].set(rw)
    rb_p = jnp.full((1, LANES), NEG_BIG, F32).at[0, :N_EXPERTS].set(rb)
    return pl.pallas_call(
        _route_kernel,
        out_shape=(
            jax.ShapeDtypeStruct((n, d), F32),
            jax.ShapeDtypeStruct((n, LANES), F32),
            jax.ShapeDtypeStruct((n, LANES), jnp.int32),
            jax.ShapeDtypeStruct((8, LANES), F32),
        ),
        grid=(n // t,),
        in_specs=[
            pl.BlockSpec((t, d), lambda i: (i, 0)),
            pl.BlockSpec((1, 6, d), lambda i: (i // tiles_per_batch, 0, 0)),
            pl.BlockSpec((1, d), lambda i: (0, 0)),
            pl.BlockSpec((d, LANES), lambda i: (0, 0)),
            pl.BlockSpec((1, LANES), lambda i: (0, 0)),
        ],
        out_specs=(
            pl.BlockSpec((t, d), lambda i: (i, 0)),
            pl.BlockSpec((t, LANES), lambda i: (i, 0)),
            pl.BlockSpec((t, LANES), lambda i: (i, 0)),
            pl.BlockSpec((8, LANES), lambda i: (0, 0)),
        ),
        compiler_params=_cparams(("arbitrary",)),
        name="moe_route",
    )(x2, mod, ng.reshape(1, d), rw_p, rb_p)


def _dispatch_kernel(pad0_ref, padn_ref, h_ref, dest_hbm, xs_hbm, zrow, idx_smem, sem_idx, sem_rows, sem_pad):
    t = h_ref.shape[0]
    i = pl.program_id(0)

    @pl.when(i == 0)
    def _():
        zrow[...] = jnp.zeros(zrow.shape, F32)
        for start in (True, False):
            def per_expert(e, carry, start=start):
                def per_row(r, c):
                    pad_cp = pltpu.make_async_copy(zrow, xs_hbm.at[pl.ds(pad0_ref[e] + r, 1), :], sem_pad)
                    if start:
                        pad_cp.start()
                    else:
                        pad_cp.wait()
                    return c
                return lax.fori_loop(0, padn_ref[e], per_row, carry)
            lax.fori_loop(0, N_EXPERTS, per_expert, 0)

    cp = pltpu.make_async_copy(dest_hbm.at[i], idx_smem, sem_idx)
    cp.start()
    cp.wait()

    def row_copy(tok, dst):
        return pltpu.make_async_copy(h_ref.at[pl.ds(tok, 1), :], xs_hbm.at[pl.ds(dst, 1), :], sem_rows)

    def issue(tok, carry):
        for k in range(TOP_K):
            row_copy(tok, idx_smem[k * t + tok]).start()
        return carry

    lax.fori_loop(0, t, issue, 0)

    def drain(tok, carry):
        for k in range(TOP_K):
            row_copy(tok, idx_smem[k * t + tok]).wait()
        return carry

    lax.fori_loop(0, t, drain, 0)


def _dispatch(h, dest_tiles, pad_start, pad_n, p_rows, t):
    n, d = h.shape
    return pl.pallas_call(
        _dispatch_kernel,
        out_shape=jax.ShapeDtypeStruct((p_rows, d), F32),
        grid_spec=pltpu.PrefetchScalarGridSpec(
            num_scalar_prefetch=2,
            grid=(n // t,),
            in_specs=[
                pl.BlockSpec((t, d), lambda i, p0, pn: (i, 0)),
                pl.BlockSpec(memory_space=pl.ANY),
            ],
            out_specs=pl.BlockSpec(memory_space=pl.ANY),
            scratch_shapes=[pltpu.VMEM((1, d), F32), pltpu.SMEM((TOP_K * t,), jnp.int32),
                            pltpu.SemaphoreType.DMA, pltpu.SemaphoreType.DMA, pltpu.SemaphoreType.DMA],
        ),
        compiler_params=_cparams(("arbitrary",)),
        name="moe_dispatch",
    )(pad_start, pad_n, h, dest_tiles)


def _experts_kernel(be_ref, nu_ref, xs_ref, wgu_ref, bgu_ref, wdn_ref, bdn_ref, ys_ref, wgu_bf, wdn_bf):
    b = pl.program_id(0)
    f = wdn_ref.shape[2]
    e = be_ref[b]
    e_prev = be_ref[jnp.maximum(b - 1, 0)]

    @pl.when(jnp.logical_or(b == 0, e != e_prev))
    def _():
        wgu_bf[...] = wgu_ref[0, 0].astype(BF16)
        wdn_bf[...] = wdn_ref[0, 0].astype(BF16)

    @pl.when(b < nu_ref[0])
    def _():
        x = xs_ref[...].astype(BF16)
        gu = jnp.dot(x, wgu_bf[...], preferred_element_type=F32) + bgu_ref[0, 0]
        x_glu = jnp.minimum(gu[:, :f], SWIGLU_LIMIT)
        x_lin = jnp.clip(gu[:, f:], -SWIGLU_LIMIT, SWIGLU_LIMIT)
        act = x_glu * jax.nn.sigmoid(SWIGLU_ALPHA * x_glu) * (x_lin + 1.0)
        ys_ref[...] = jnp.dot(act.astype(BF16), wdn_bf[...], preferred_element_type=F32) + bdn_ref[0, 0]


def _experts(xs, block_e, n_used, layer, w_gu, b_gu, w_dn, b_dn):
    p_rows, d = xs.shape
    n_l, n_e, _, f2 = w_gu.shape
    f = f2 // 2
    n_blocks = p_rows // EXPERT_ROWS
    row_map = lambda b, be, nu: (jnp.minimum(b, nu[0] - 1), 0)
    exp_map = lambda b, be, nu: (layer, be[b], 0, 0)
    return pl.pallas_call(
        _experts_kernel,
        out_shape=jax.ShapeDtypeStruct((p_rows, d), F32),
        grid_spec=pltpu.PrefetchScalarGridSpec(
            num_scalar_prefetch=2,
            grid=(n_blocks,),
            in_specs=[
                pl.BlockSpec((EXPERT_ROWS, d), row_map),
                pl.BlockSpec((1, 1, d, f2), exp_map),
                pl.BlockSpec((1, 1, 1, f2), exp_map),
                pl.BlockSpec((1, 1, f, d), exp_map),
                pl.BlockSpec((1, 1, 1, d), exp_map),
            ],
            out_specs=pl.BlockSpec((EXPERT_ROWS, d), row_map),
            scratch_shapes=[pltpu.VMEM((d, f2), BF16), pltpu.VMEM((f, d), BF16)],
        ),
        compiler_params=_cparams(("arbitrary",)),
        name="moe_experts",
    )(block_e, n_used, xs, w_gu, b_gu.reshape(n_l, n_e, 1, f2), w_dn, b_dn.reshape(n_l, n_e, 1, d))


def _combine_kernel(x_ref, gate_ref, mod_ref, dest_hbm, ys_hbm, o_ref, buf, idx_smem, sem_idx, sem_rows):
    t = x_ref.shape[0]
    i = pl.program_id(0)
    cp = pltpu.make_async_copy(dest_hbm.at[i], idx_smem, sem_idx)
    cp.start()
    cp.wait()

    def row_copy(k, tok, src):
        return pltpu.make_async_copy(ys_hbm.at[pl.ds(src, 1), :], buf.at[k, pl.ds(tok, 1), :], sem_rows)

    def issue(tok, carry):
        for k in range(TOP_K):
            row_copy(k, tok, idx_smem[k * t + tok]).start()
        return carry

    lax.fori_loop(0, t, issue, 0)

    def drain(tok, carry):
        for k in range(TOP_K):
            row_copy(k, tok, idx_smem[k * t + tok]).wait()
        return carry

    lax.fori_loop(0, t, drain, 0)

    gate = gate_ref[...]
    y = gate[:, 0:1] * buf[0]
    for k in range(1, TOP_K):
        y = y + gate[:, k:k + 1] * buf[k]
    o_ref[...] = x_ref[...] + mod_ref[0][5:6, :] * y


def _combine(x2, gates, mod, dest_tiles, ys, t, tiles_per_batch):
    n, d = x2.shape
    return pl.pallas_call(
        _combine_kernel,
        out_shape=jax.ShapeDtypeStruct((n, d), F32),
        grid=(n // t,),
        in_specs=[
            pl.BlockSpec((t, d), lambda i: (i, 0)),
            pl.BlockSpec((t, LANES), lambda i: (i, 0)),
            pl.BlockSpec((1, 6, d), lambda i: (i // tiles_per_batch, 0, 0)),
            pl.BlockSpec(memory_space=pl.ANY),
            pl.BlockSpec(memory_space=pl.ANY),
        ],
        out_specs=pl.BlockSpec((t, d), lambda i: (i, 0)),
        scratch_shapes=[pltpu.VMEM((TOP_K, t, d), F32), pltpu.SMEM((TOP_K * t,), jnp.int32),
                        pltpu.SemaphoreType.DMA, pltpu.SemaphoreType.DMA],
        compiler_params=_cparams(("arbitrary",)),
        name="moe_combine",
    )(x2, gates, mod, dest_tiles, ys)


def _moe_layer(x, mod, ng, rw, rb, layer, w_gu, b_gu, w_dn, b_dn, t):
    b, s, d = x.shape
    n = b * s
    x2 = x.reshape(n, d)
    tiles_per_batch = s // t
    h, gates, meta, cnt = _route(x2, mod, ng, rw, rb, t, tiles_per_batch)
    counts = cnt[0, :N_EXPERTS].astype(jnp.int32)
    padded = (counts + EXPERT_ROWS - 1) // EXPERT_ROWS * EXPERT_ROWS
    pend = jnp.cumsum(padded)
    pstart = pend - padded
    n_blocks = -(-(n * TOP_K) // EXPERT_ROWS) + N_EXPERTS
    p_rows = n_blocks * EXPERT_ROWS
    rank = meta[:, :TOP_K]
    eidx = meta[:, TOP_K:2 * TOP_K]
    experts = jnp.arange(N_EXPERTS, dtype=jnp.int32)
    dest = rank + jnp.sum(jnp.where(eidx[..., None] == experts, pstart, 0), axis=-1)
    dest_tiles = dest.reshape(n // t, t, TOP_K).transpose(0, 2, 1).reshape(n // t, TOP_K * t)
    block_row0 = jnp.arange(n_blocks, dtype=jnp.int32) * EXPERT_ROWS
    block_e = jnp.minimum(jnp.sum((pend[None, :] <= block_row0[:, None]).astype(jnp.int32), axis=1),
                          N_EXPERTS - 1)
    n_used = (pend[-1:] // EXPERT_ROWS).astype(jnp.int32)
    xs = _dispatch(h, dest_tiles, pstart + counts, padded - counts, p_rows, t)
    ys = _experts(xs, block_e, n_used, layer, w_gu, b_gu, w_dn, b_dn)
    out = _combine(x2, gates, mod, dest_tiles, ys, t, tiles_per_batch)
    return out.reshape(b, s, d)


def _kv_kernel(x_ref, mod_ref, ng_ref, wkv_ref, wf_ref, bf_ref, kg_ref, hsum_ref,
               kt_ref, v_ref, qf_ref, k2_ref, carry):
    t, d = x_ref.shape[1], x_ref.shape[2]
    n_pairs = N_HEADS // 2

    @pl.when(pl.program_id(1) == 0)
    def _():
        carry[...] = jnp.zeros(carry.shape, F32)

    mod = mod_ref[0]
    h = _rms(x_ref[0], ng_ref[...]) * (1.0 + mod[1:2, :]) + mod[0:1, :]
    kv = jnp.dot(h.astype(BF16), wkv_ref[...], preferred_element_type=F32)
    k = kv[:, :d]
    v_ref[0] = kv[:, d:].astype(BF16)
    ms = jnp.dot((k * k).astype(BF16), hsum_ref[...], preferred_element_type=F32)
    k = k * lax.rsqrt(ms + EPS) * kg_ref[...]
    k2 = jnp.dot((k * k).astype(BF16), hsum_ref[...], preferred_element_type=F32) * HEAD_DIM
    k2_ref[0, 0] = jnp.max(k2, axis=0, keepdims=True)
    kt = k.T
    for hp in range(n_pairs):
        kt_ref[0, hp, 0, pl.ds(0, LANES), :] = kt[hp * LANES:(hp + 1) * LANES, :].astype(BF16)

    fz = jnp.dot(h, wf_ref[...], precision=HIGHEST, preferred_element_type=F32) + bf_ref[...]
    ls = jax.nn.log_sigmoid(fz)
    r_i = lax.broadcasted_iota(jnp.int32, (t, t), 0)
    c_i = lax.broadcasted_iota(jnp.int32, (t, t), 1)
    tri = (c_i <= r_i).astype(F32)
    cum = jnp.dot(tri, ls, precision=HIGHEST, preferred_element_type=F32) + carry[0:1, :]
    carry[...] = jnp.broadcast_to(cum[t - 1:t, :], carry.shape)
    f2 = cum * LOG2E
    f2t = f2.T
    lane = lax.broadcasted_iota(jnp.int32, (t, LANES), 1)
    sub = lax.broadcasted_iota(jnp.int32, (LANES, t), 0)
    q_pieces = _split3(f2)
    k_pieces = _split3(-f2t)
    for hp in range(n_pairs):
        q_aug = jnp.zeros((t, LANES), F32)
        k_aug = jnp.zeros((LANES, t), F32)
        for hh in range(2):
            head = 2 * hp + hh
            o = hh * 2 * N_FPIECES
            for p in range(N_FPIECES):
                q_aug = jnp.where(lane == o + p, q_pieces[p][:, head:head + 1], q_aug)
                q_aug = jnp.where(lane == o + N_FPIECES + p, 1.0, q_aug)
                k_aug = jnp.where(sub == o + p, 1.0, k_aug)
                k_aug = jnp.where(sub == o + N_FPIECES + p, k_pieces[p][head:head + 1, :], k_aug)
        qf_ref[0, hp] = q_aug.astype(BF16)
        kt_ref[0, hp, 0, pl.ds(LANES, LANES), :] = k_aug.astype(BF16)


def _shared_kv(x, kvmod, ng, w_kvf, b_f, k_norm_g, t):
    b, s, d = x.shape
    n_pairs = N_HEADS // 2
    wkv = w_kvf[:, :2 * d].astype(BF16)
    wf = jnp.zeros((d, LANES), F32).at[:, :N_HEADS].set(w_kvf[:, 2 * d:])
    bf = jnp.zeros((1, LANES), F32).at[0, :N_HEADS].set(b_f)
    kg = jnp.tile(k_norm_g, N_HEADS).reshape(1, d)
    head_of = jnp.arange(d) // HEAD_DIM
    hsum = ((head_of[:, None] == head_of[None, :]).astype(F32) / HEAD_DIM).astype(BF16)
    const = lambda shape: pl.BlockSpec(shape, lambda bi, si: (0,) * len(shape))
    return pl.pallas_call(
        _kv_kernel,
        out_shape=(
            jax.ShapeDtypeStruct((b, n_pairs, s // t, 2 * LANES, t), BF16),
            jax.ShapeDtypeStruct((b, s, d), BF16),
            jax.ShapeDtypeStruct((b, n_pairs, s, LANES), BF16),
            jax.ShapeDtypeStruct((b, s // t, 1, d), F32),
        ),
        grid=(b, s // t),
        in_specs=[
            pl.BlockSpec((1, t, d), lambda bi, si: (bi, si, 0)),
            pl.BlockSpec((1, 2, d), lambda bi, si: (bi, 0, 0)),
            const((1, d)), const((d, 2 * d)), const((d, LANES)), const((1, LANES)), const((1, d)),
            const((d, d)),
        ],
        out_specs=(
            pl.BlockSpec((1, n_pairs, 1, 2 * LANES, t), lambda bi, si: (bi, 0, si, 0, 0)),
            pl.BlockSpec((1, t, d), lambda bi, si: (bi, si, 0)),
            pl.BlockSpec((1, n_pairs, t, LANES), lambda bi, si: (bi, 0, si, 0)),
            pl.BlockSpec((1, 1, 1, d), lambda bi, si: (bi, si, 0, 0)),
        ),
        scratch_shapes=[pltpu.VMEM((8, LANES), F32)],
        compiler_params=_cparams(("arbitrary", "arbitrary")),
        name="shared_kv",
    )(x, kvmod, ng.reshape(1, d), wkv, wf, bf, kg, hsum), hsum


def _qg_kernel(x_ref, mod_ref, ng_ref, w_ref, qg_ref, hsum_ref, q_ref, g_ref, q2_ref):
    d = x_ref.shape[2]
    mod = mod_ref[0]
    h = _rms(x_ref[0], ng_ref[...]) * (1.0 + mod[1:2, :]) + mod[0:1, :]
    qg = jnp.dot(h.astype(BF16), w_ref[...], preferred_element_type=F32)
    q = qg[:, :d]
    ms = jnp.dot((q * q).astype(BF16), hsum_ref[...], preferred_element_type=F32)
    q = q * lax.rsqrt(ms + EPS) * qg_ref[...] * (LOG2E / math.sqrt(HEAD_DIM))
    q_ref[0] = q.astype(BF16)
    g_ref[0] = jax.nn.sigmoid(qg[:, d:]).astype(BF16)
    q2 = jnp.dot((q * q).astype(BF16), hsum_ref[...], preferred_element_type=F32) * HEAD_DIM
    q2_ref[0, 0] = jnp.max(q2, axis=0, keepdims=True)


def _qg(x, mod, ng, w_qg, q_norm_g, hsum, t):
    b, s, d = x.shape
    const = lambda shape: pl.BlockSpec(shape, lambda bi, si: (0,) * len(shape))
    tile = pl.BlockSpec((1, t, d), lambda bi, si: (bi, si, 0))
    return pl.pallas_call(
        _qg_kernel,
        out_shape=(jax.ShapeDtypeStruct((b, s, d), BF16), jax.ShapeDtypeStruct((b, s, d), BF16),
                   jax.ShapeDtypeStruct((b, s // t, 1, d), F32)),
        grid=(b, s // t),
        in_specs=[tile, pl.BlockSpec((1, 6, d), lambda bi, si: (bi, 0, 0)),
                  const((1, d)), const((d, 2 * d)), const((1, d)), const((d, d))],
        out_specs=(tile, tile, pl.BlockSpec((1, 1, 1, d), lambda bi, si: (bi, si, 0, 0))),
        compiler_params=_cparams(("arbitrary", "arbitrary")),
        name="attn_qg",
    )(x, mod, ng.reshape(1, d), w_qg.astype(BF16), jnp.tile(q_norm_g, N_HEADS).reshape(1, d), hsum)


def _attn_kernel(q_ref, qf_ref, kt_ref, v_ref, o_ref, qa_scr, m_scr, l_scr, acc_scr, *, online):
    tq = q_ref.shape[1]
    tk = kt_ref.shape[4]
    i = pl.program_id(2)
    lane = lax.broadcasted_iota(jnp.int32, (tq, LANES), 1)
    q2 = q_ref[0]
    qf = qf_ref[0, 0]
    zero = jnp.zeros((), BF16)
    n_aug = 2 * N_FPIECES
    qa_scr[0, :, pl.ds(0, LANES)] = jnp.where(lane < HEAD_DIM, q2, zero)
    qa_scr[0, :, pl.ds(LANES, LANES)] = jnp.where(lane < n_aug, qf, zero)
    qa_scr[1, :, pl.ds(0, LANES)] = jnp.where(lane >= HEAD_DIM, q2, zero)
    qa_scr[1, :, pl.ds(LANES, LANES)] = jnp.where(jnp.logical_and(lane >= n_aug, lane < 2 * n_aug), qf, zero)
    if online:
        m_scr[...] = jnp.full(m_scr.shape, NEG_BIG, F32)
    l_scr[...] = jnp.zeros(l_scr.shape, F32)
    acc_scr[...] = jnp.zeros(acc_scr.shape, F32)

    def tile(j, masked):
        kt = kt_ref[0, 0, j]
        vv = v_ref[0, pl.ds(pl.multiple_of(j * tk, tk), tk), :]
        for hh in range(2):
            s = jnp.dot(qa_scr[hh], kt, preferred_element_type=F32)
            if masked:
                r_i = lax.broadcasted_iota(jnp.int32, (tq, tk), 0)
                c_i = lax.broadcasted_iota(jnp.int32, (tq, tk), 1)
                s = jnp.where(c_i <= r_i, s, NEG_BIG)
            if online:
                m_prev = m_scr[hh]
                m_new = jnp.maximum(m_prev, jnp.max(s, axis=-1, keepdims=True))
                alpha = jnp.exp2(m_prev - m_new)
                p = jnp.exp2(s - m_new[:, 0:1])
                l_scr[hh] = alpha * l_scr[hh] + jnp.sum(p, axis=-1, keepdims=True)
                acc_scr[hh] = alpha * acc_scr[hh] + jnp.dot(p.astype(BF16), vv, preferred_element_type=F32)
                m_scr[hh] = m_new
            else:
                p = jnp.exp2(s)
                part = p[:, 0:LANES]
                for c in range(1, tk // LANES):
                    part = part + p[:, c * LANES:(c + 1) * LANES]
                l_scr[hh] = l_scr[hh] + part
                acc_scr[hh] = acc_scr[hh] + jnp.dot(p.astype(BF16), vv, preferred_element_type=F32)

    def full_tile(j, carry):
        tile(j, False)
        return carry

    lax.fori_loop(0, i, full_tile, 0)
    tile(i, True)
    if online:
        l0, l1 = l_scr[0], l_scr[1]
    else:
        l0 = jnp.sum(l_scr[0], axis=-1, keepdims=True)
        l1 = jnp.sum(l_scr[1], axis=-1, keepdims=True)
    o_ref[0] = jnp.where(lane < HEAD_DIM, acc_scr[0] / l0, acc_scr[1] / l1).astype(BF16)


def _attention(q, qf, kt, v, t, online):
    b, s, d = q.shape
    n_pairs = N_HEADS // 2
    nkv = s // t
    return pl.pallas_call(
        functools.partial(_attn_kernel, online=online),
        out_shape=jax.ShapeDtypeStruct((b, s, d), BF16),
        grid=(b, n_pairs, s // t),
        in_specs=[
            pl.BlockSpec((1, t, LANES), lambda bi, hp, i: (bi, i, hp)),
            pl.BlockSpec((1, 1, t, LANES), lambda bi, hp, i: (bi, hp, i, 0)),
            pl.BlockSpec((1, 1, nkv, 2 * LANES, t), lambda bi, hp, i: (bi, hp, 0, 0, 0)),
            pl.BlockSpec((1, s, LANES), lambda bi, hp, i: (bi, 0, hp)),
        ],
        out_specs=pl.BlockSpec((1, t, LANES), lambda bi, hp, i: (bi, i, hp)),
        scratch_shapes=[
            pltpu.VMEM((2, t, 2 * LANES), BF16),
            pltpu.VMEM((2, t, LANES), F32),
            pltpu.VMEM((2, t, LANES), F32),
            pltpu.VMEM((2, t, LANES), F32),
        ],
        compiler_params=_cparams(("arbitrary", "arbitrary", "arbitrary")),
        name="fox_attention_online" if online else "fox_attention",
    )(q, qf, kt, v)


def _attn_out_kernel(x_ref, o_ref, g_ref, mod_ref, w_ref, out_ref):
    og = o_ref[0] * g_ref[0]
    y = jnp.dot(og, w_ref[...], preferred_element_type=F32)
    out_ref[0] = x_ref[0] + mod_ref[0][2:3, :] * y


def _attn_out(x, o, g, mod, w_o, t):
    b, s, d = x.shape
    tile = pl.BlockSpec((1, t, d), lambda bi, si: (bi, si, 0))
    return pl.pallas_call(
        _attn_out_kernel,
        out_shape=jax.ShapeDtypeStruct((b, s, d), F32),
        grid=(b, s // t),
        in_specs=[tile, tile, tile, pl.BlockSpec((1, 6, d), lambda bi, si: (bi, 0, 0)),
                  pl.BlockSpec((d, d), lambda bi, si: (0, 0))],
        out_specs=tile,
        compiler_params=_cparams(("arbitrary", "arbitrary")),
        name="attn_out",
    )(x, o, g, mod, w_o.astype(BF16))


def _final_norm_kernel(x_ref, g_ref, o_ref):
    o_ref[...] = _rms(x_ref[...], g_ref[...])


def _final_norm(x2, g, t):
    n, d = x2.shape
    return pl.pallas_call(
        _final_norm_kernel,
        out_shape=jax.ShapeDtypeStruct((n, d), F32),
        grid=(n // t,),
        in_specs=[pl.BlockSpec((t, d), lambda i: (i, 0)), pl.BlockSpec((1, d), lambda i: (0, 0))],
        out_specs=pl.BlockSpec((t, d), lambda i: (i, 0)),
        compiler_params=_cparams(("arbitrary",)),
        name="final_norm",
    )(x2, g.reshape(1, d))


def kernel(x, c, mod_w, mod_b, norm1_g, norm2_g, conv_w_pw1, conv_b_pw1, conv_w_dw, conv_b_dw, conv_ln_g, conv_ln_b, conv_w_pw2, conv_b_pw2, kv_mod_w, kv_mod_b, kv_norm_g, w_kvf, b_f, k_norm_g, attn_w_qg, q_norm_g, attn_w_o, moe_router_w, moe_router_b, moe_w_gu, moe_b_gu, moe_w_down, moe_b_down, final_norm_g):
    b, s, d = x.shape
    depth = mod_w.shape[0]
    n_a = conv_w_pw1.shape[0]
    t = min(512, s)
    c8 = jnp.zeros((8, d), F32).at[:b].set(c)
    mods = _mods(c8, mod_w, mod_b)[:, :b].reshape(depth, b, 6, d)
    kvmod = _mods(c8, kv_mod_w[None], kv_mod_b[None])[0, :b].reshape(b, 2, d)
    kt = v = qf = hsum = None
    for l in range(depth):
        if l < n_a:
            x = _conv_layer(x, mods[l], norm1_g[l], conv_w_pw1[l], conv_b_pw1[l], conv_w_dw[l],
                            conv_b_dw[l], conv_ln_g[l], conv_ln_b[l], conv_w_pw2[l], conv_b_pw2[l], t)
        else:
            lb = l - n_a
            q, g, q2 = _qg(x, mods[l], norm1_g[l], attn_w_qg[lb], q_norm_g[lb], hsum, t)
            bound = jnp.sqrt(jnp.max(q2) * jnp.max(k2))
            o = lax.cond(bound <= DIRECT_EXP_LIMIT,
                         functools.partial(_attention, t=t, online=False),
                         functools.partial(_attention, t=t, online=True),
                         q, qf, kt, v)
            x = _attn_out(x, o, g, mods[l], attn_w_o[lb], t)
        x = _moe_layer(x, mods[l], norm2_g[l], moe_router_w[l], moe_router_b[l], l, moe_w_gu,
                       moe_b_gu, moe_w_down, moe_b_down, t)
        if l == n_a - 1:
            (kt, v, qf, k2), hsum = _shared_kv(x, kvmod, kv_norm_g, w_kvf, b_f, k_norm_g, t)
    return _final_norm(x.reshape(b * s, d), final_norm_g, t).reshape(b, s, d)
```

```python
import functools
import math

import jax
import jax.numpy as jnp
from jax import lax
from jax.experimental import pallas as pl
from jax.experimental.pallas import tpu as pltpu

N_HEADS = 16
HEAD_DIM = 64
CONV_WIDTH = 31
N_EXPERTS = 32
TOP_K = 4
SWIGLU_ALPHA = 1.702
SWIGLU_LIMIT = 7.0
EPS = 1e-6

LANES = 128
HALO = 32
EXPERT_ROWS = 256
VMEM_LIMIT = 56 * 1024 * 1024
LOG2E = 1.4426950408889634
NEG_BIG = -1e30
N_FPIECES = 3
DIRECT_EXP_LIMIT = 60.0
ZERO_WEIGHT_EXPONENT = -160.0
BOUND_SLACK = 1.05

F32 = jnp.float32
BF16 = jnp.bfloat16
HIGHEST = lax.Precision.HIGHEST


def _cparams(sem):
    return pltpu.CompilerParams(dimension_semantics=sem, vmem_limit_bytes=VMEM_LIMIT)


def _rms(x, g):
    return x * lax.rsqrt(jnp.mean(x * x, axis=-1, keepdims=True) + EPS) * g


def _split3(f):
    hi = f.astype(BF16).astype(F32)
    r1 = f - hi
    mid = r1.astype(BF16).astype(F32)
    lo = (r1 - mid).astype(BF16).astype(F32)
    return hi, mid, lo


def _mods_kernel(c_ref, w_ref, b_ref, o_ref):
    c = c_ref[...]
    ca = c * jax.nn.sigmoid(c)
    o_ref[0] = jnp.dot(ca, w_ref[0], precision=HIGHEST, preferred_element_type=F32) + b_ref[0]


def _mods(c8, w, b):
    n_l, d, m = w.shape
    tn = min(m, 1024)
    return pl.pallas_call(
        _mods_kernel,
        out_shape=jax.ShapeDtypeStruct((n_l, 8, m), F32),
        grid=(n_l, m // tn),
        in_specs=[
            pl.BlockSpec((8, d), lambda l, j: (0, 0)),
            pl.BlockSpec((1, d, tn), lambda l, j: (l, 0, j)),
            pl.BlockSpec((1, 1, tn), lambda l, j: (l, 0, j)),
        ],
        out_specs=pl.BlockSpec((1, 8, tn), lambda l, j: (l, 0, j)),
        compiler_params=_cparams(("arbitrary", "arbitrary")),
        name="mods",
    )(c8, w, b.reshape(n_l, 1, m))


def _conv_kernel(x_ref, mod_ref, ng_ref, w1_ref, b1_ref, wdw_ref, bdw_ref, lng_ref, lnb_ref,
                 w2_ref, b2_ref, o_ref, ubuf):
    t, d = x_ref.shape[1], x_ref.shape[2]

    @pl.when(pl.program_id(1) == 0)
    def _():
        ubuf[pl.ds(0, HALO), :] = jnp.zeros((HALO, d), F32)

    x = x_ref[0]
    mod = mod_ref[0]
    h = _rms(x, ng_ref[...]) * (1.0 + mod[1:2, :]) + mod[0:1, :]
    u = jnp.dot(h.astype(BF16), w1_ref[...], preferred_element_type=F32) + b1_ref[...]
    u = u[:, :d] * jax.nn.sigmoid(u[:, d:])
    ubuf[pl.ds(HALO, t), :] = u
    acc = jnp.zeros((t, d), F32)
    for j in range(CONV_WIDTH):
        acc = acc + ubuf[pl.ds(HALO - (CONV_WIDTH - 1) + j, t), :] * wdw_ref[j:j + 1, :]
    ubuf[pl.ds(0, HALO), :] = ubuf[pl.ds(t, HALO), :]
    acc = acc + bdw_ref[...]
    mu = jnp.mean(acc, axis=-1, keepdims=True)
    cen = acc - mu
    var = jnp.mean(cen * cen, axis=-1, keepdims=True)
    y = cen * lax.rsqrt(var + EPS) * lng_ref[...] + lnb_ref[...]
    y = y * jax.nn.sigmoid(y)
    y = jnp.dot(y.astype(BF16), w2_ref[...], preferred_element_type=F32) + b2_ref[...]
    o_ref[0] = x + mod[2:3, :] * y


def _conv_layer(x, mod, ng, w1, b1, wdw, bdw, lng, lnb, w2, b2, t):
    b, s, d = x.shape
    row = lambda a: a.reshape(1, -1)
    wdw_p = jnp.zeros((HALO, d), F32).at[:CONV_WIDTH].set(wdw)
    const = lambda shape: pl.BlockSpec(shape, lambda bi, si: (0,) * len(shape))
    return pl.pallas_call(
        _conv_kernel,
        out_shape=jax.ShapeDtypeStruct((b, s, d), F32),
        grid=(b, s // t),
        in_specs=[
            pl.BlockSpec((1, t, d), lambda bi, si: (bi, si, 0)),
            pl.BlockSpec((1, 6, d), lambda bi, si: (bi, 0, 0)),
            const((1, d)), const((d, 2 * d)), const((1, 2 * d)), const((HALO, d)), const((1, d)),
            const((1, d)), const((1, d)), const((d, d)), const((1, d)),
        ],
        out_specs=pl.BlockSpec((1, t, d), lambda bi, si: (bi, si, 0)),
        scratch_shapes=[pltpu.VMEM((t + HALO, d), F32)],
        compiler_params=_cparams(("arbitrary", "arbitrary")),
        name="conv_layer",
    )(x, mod, row(ng), w1.astype(BF16), row(b1), wdw_p, row(bdw), row(lng), row(lnb),
      w2.astype(BF16), row(b2))


def _route_kernel(x_ref, mod_ref, ng_ref, rw_ref, rb_ref, h_ref, gate_ref, meta_ref, cnt_ref):
    t = x_ref.shape[0]

    @pl.when(pl.program_id(0) == 0)
    def _():
        cnt_ref[...] = jnp.zeros(cnt_ref.shape, F32)

    mod = mod_ref[0]
    h = _rms(x_ref[...], ng_ref[...]) * (1.0 + mod[4:5, :]) + mod[3:4, :]
    h_ref[...] = h.reshape(h_ref.shape)
    logits = jnp.dot(h, rw_ref[...], precision=HIGHEST, preferred_element_type=F32) + rb_ref[...]
    lane = lax.broadcasted_iota(jnp.int32, (t, LANES), 1).astype(F32)
    work = logits
    vals, idxs = [], []
    for _ in range(TOP_K):
        m = jnp.max(work, axis=-1, keepdims=True)
        idx = jnp.min(jnp.where(work == m, lane, float(LANES)), axis=-1, keepdims=True)
        vals.append(m)
        idxs.append(idx)
        work = jnp.where(lane == idx, -jnp.inf, work)
    exps = [jnp.exp(v - vals[0]) for v in vals]
    denom = exps[0] + exps[1] + exps[2] + exps[3]
    onehot = jnp.zeros((t, LANES), F32)
    for idx in idxs:
        onehot = onehot + (lane == idx).astype(F32)
    r_i = lax.broadcasted_iota(jnp.int32, (t, t), 0)
    c_i = lax.broadcasted_iota(jnp.int32, (t, t), 1)
    tri = (c_i < r_i).astype(BF16)
    base = jnp.dot(tri, onehot.astype(BF16), preferred_element_type=F32) + cnt_ref[0:1, :]
    gate_out = jnp.zeros((t, LANES), F32)
    meta = jnp.zeros((t, LANES), F32)
    for k in range(TOP_K):
        rank = jnp.sum(jnp.where(lane == idxs[k], base, 0.0), axis=-1, keepdims=True)
        gate_out = jnp.where(lane == k, exps[k] / denom, gate_out)
        meta = jnp.where(lane == k, rank, meta)
        meta = jnp.where(lane == TOP_K + k, idxs[k], meta)
    gate_ref[...] = gate_out
    meta_ref[...] = meta.astype(jnp.int32)
    cnt_ref[...] = cnt_ref[...] + jnp.sum(onehot, axis=0, keepdims=True)


def _route(x2, mod, ng, rw, rb, t, tiles_per_batch):
    n, d = x2.shape
    rw_p = jnp.zeros((d, LANES), F32).at[:, :N_EXPERTS].set(rw)
    rb_p = jnp.full((1, LANES), NEG_BIG, F32).at[0, :N_EXPERTS].set(rb)
    return pl.pallas_call(
        _route_kernel,
        out_shape=(
            jax.ShapeDtypeStruct((n, d // LANES, LANES), F32),
            jax.ShapeDtypeStruct((n, LANES), F32),
            jax.ShapeDtypeStruct((n, LANES), jnp.int32),
            jax.ShapeDtypeStruct((8, LANES), F32),
        ),
        grid=(n // t,),
        in_specs=[
            pl.BlockSpec((t, d), lambda i: (i, 0)),
            pl.BlockSpec((1, 6, d), lambda i: (i // tiles_per_batch, 0, 0)),
            pl.BlockSpec((1, d), lambda i: (0, 0)),
            pl.BlockSpec((d, LANES), lambda i: (0, 0)),
            pl.BlockSpec((1, LANES), lambda i: (0, 0)),
        ],
        out_specs=(
            pl.BlockSpec((t, d // LANES, LANES), lambda i: (i, 0, 0)),
            pl.BlockSpec((t, LANES), lambda i: (i, 0)),
            pl.BlockSpec((t, LANES), lambda i: (i, 0)),
            pl.BlockSpec((8, LANES), lambda i: (0, 0)),
        ),
        compiler_params=_cparams(("arbitrary",)),
        name="moe_route",
    )(x2, mod, ng.reshape(1, d), rw_p, rb_p)


def _dispatch_kernel(pad0_ref, padn_ref, h_ref, dest_hbm, xs_hbm, zrow, idx_smem, sem_idx, sem_rows, sem_pad):
    t = h_ref.shape[0]
    i = pl.program_id(0)

    @pl.when(i == 0)
    def _():
        zrow[...] = jnp.zeros(zrow.shape, F32)
        for start in (True, False):
            def per_expert(e, carry, start=start):
                def per_row(r, c):
                    pad_cp = pltpu.make_async_copy(zrow, xs_hbm.at[pad0_ref[e] + r], sem_pad)
                    if start:
                        pad_cp.start()
                    else:
                        pad_cp.wait()
                    return c
                return lax.fori_loop(0, padn_ref[e], per_row, carry)
            lax.fori_loop(0, N_EXPERTS, per_expert, 0)

    cp = pltpu.make_async_copy(dest_hbm.at[i], idx_smem, sem_idx)
    cp.start()
    cp.wait()

    def row_copy(tok, dst):
        return pltpu.make_async_copy(h_ref.at[tok], xs_hbm.at[dst], sem_rows)

    def issue(tok, carry):
        for k in range(TOP_K):
            row_copy(tok, idx_smem[k * t + tok]).start()
        return carry

    lax.fori_loop(0, t, issue, 0)

    def drain(tok, carry):
        for k in range(TOP_K):
            row_copy(tok, idx_smem[k * t + tok]).wait()
        return carry

    lax.fori_loop(0, t, drain, 0)


def _dispatch(h, dest_tiles, pad_start, pad_n, p_rows, t):
    n, sub, _ = h.shape
    return pl.pallas_call(
        _dispatch_kernel,
        out_shape=jax.ShapeDtypeStruct((p_rows, sub, LANES), F32),
        grid_spec=pltpu.PrefetchScalarGridSpec(
            num_scalar_prefetch=2,
            grid=(n // t,),
            in_specs=[
                pl.BlockSpec((t, sub, LANES), lambda i, p0, pn: (i, 0, 0)),
                pl.BlockSpec(memory_space=pl.ANY),
            ],
            out_specs=pl.BlockSpec(memory_space=pl.ANY),
            scratch_shapes=[pltpu.VMEM((sub, LANES), F32), pltpu.SMEM((TOP_K * t,), jnp.int32),
                            pltpu.SemaphoreType.DMA, pltpu.SemaphoreType.DMA, pltpu.SemaphoreType.DMA],
        ),
        compiler_params=_cparams(("arbitrary",)),
        name="moe_dispatch",
    )(pad_start, pad_n, h, dest_tiles)


def _experts_kernel(be_ref, nu_ref, xs_ref, wgu_ref, bgu_ref, wdn_ref, bdn_ref, ys_ref, wgu_bf, wdn_bf):
    b = pl.program_id(0)
    f = wdn_ref.shape[2]
    e = be_ref[b]
    e_prev = be_ref[jnp.maximum(b - 1, 0)]

    @pl.when(jnp.logical_or(b == 0, e != e_prev))
    def _():
        wgu_bf[...] = wgu_ref[0, 0].astype(BF16)
        wdn_bf[...] = wdn_ref[0, 0].astype(BF16)

    @pl.when(b < nu_ref[0])
    def _():
        rows, sub, _ = xs_ref.shape
        x = xs_ref[...].reshape(rows, sub * LANES).astype(BF16)
        gu = jnp.dot(x, wgu_bf[...], preferred_element_type=F32) + bgu_ref[0, 0]
        x_glu = jnp.minimum(gu[:, :f], SWIGLU_LIMIT)
        x_lin = jnp.clip(gu[:, f:], -SWIGLU_LIMIT, SWIGLU_LIMIT)
        act = x_glu * jax.nn.sigmoid(SWIGLU_ALPHA * x_glu) * (x_lin + 1.0)
        y = jnp.dot(act.astype(BF16), wdn_bf[...], preferred_element_type=F32) + bdn_ref[0, 0]
        ys_ref[...] = y.reshape(ys_ref.shape)


def _experts(xs, block_e, n_used, layer, w_gu, b_gu, w_dn, b_dn):
    p_rows, sub, _ = xs.shape
    n_l, n_e, d, f2 = w_gu.shape
    f = f2 // 2
    n_blocks = p_rows // EXPERT_ROWS
    row_map = lambda b, be, nu: (jnp.minimum(b, nu[0] - 1), 0, 0)
    exp_map = lambda b, be, nu: (layer, be[b], 0, 0)
    return pl.pallas_call(
        _experts_kernel,
        out_shape=jax.ShapeDtypeStruct((p_rows, sub, LANES), F32),
        grid_spec=pltpu.PrefetchScalarGridSpec(
            num_scalar_prefetch=2,
            grid=(n_blocks,),
            in_specs=[
                pl.BlockSpec((EXPERT_ROWS, sub, LANES), row_map),
                pl.BlockSpec((1, 1, d, f2), exp_map),
                pl.BlockSpec((1, 1, 1, f2), exp_map),
                pl.BlockSpec((1, 1, f, d), exp_map),
                pl.BlockSpec((1, 1, 1, d), exp_map),
            ],
            out_specs=pl.BlockSpec((EXPERT_ROWS, sub, LANES), row_map),
            scratch_shapes=[pltpu.VMEM((d, f2), BF16), pltpu.VMEM((f, d), BF16)],
        ),
        compiler_params=_cparams(("arbitrary",)),
        name="moe_experts",
    )(block_e, n_used, xs, w_gu, b_gu.reshape(n_l, n_e, 1, f2), w_dn, b_dn.reshape(n_l, n_e, 1, d))


def _combine_kernel(x_ref, gate_ref, mod_ref, dest_hbm, ys_hbm, o_ref, buf, idx_smem, sem_idx, sem_rows):
    t = x_ref.shape[0]
    i = pl.program_id(0)
    cp = pltpu.make_async_copy(dest_hbm.at[i], idx_smem, sem_idx)
    cp.start()
    cp.wait()

    def row_copy(k, tok, src):
        return pltpu.make_async_copy(ys_hbm.at[src], buf.at[k, tok], sem_rows)

    def issue(tok, carry):
        for k in range(TOP_K):
            row_copy(k, tok, idx_smem[k * t + tok]).start()
        return carry

    lax.fori_loop(0, t, issue, 0)

    def drain(tok, carry):
        for k in range(TOP_K):
            row_copy(k, tok, idx_smem[k * t + tok]).wait()
        return carry

    lax.fori_loop(0, t, drain, 0)

    gate = gate_ref[...]
    y = gate[:, 0:1] * buf[0].reshape(x_ref.shape)
    for k in range(1, TOP_K):
        y = y + gate[:, k:k + 1] * buf[k].reshape(x_ref.shape)
    o_ref[...] = x_ref[...] + mod_ref[0][5:6, :] * y


def _combine(x2, gates, mod, dest_tiles, ys, t, tiles_per_batch):
    n, d = x2.shape
    return pl.pallas_call(
        _combine_kernel,
        out_shape=jax.ShapeDtypeStruct((n, d), F32),
        grid=(n // t,),
        in_specs=[
            pl.BlockSpec((t, d), lambda i: (i, 0)),
            pl.BlockSpec((t, LANES), lambda i: (i, 0)),
            pl.BlockSpec((1, 6, d), lambda i: (i // tiles_per_batch, 0, 0)),
            pl.BlockSpec(memory_space=pl.ANY),
            pl.BlockSpec(memory_space=pl.ANY),
        ],
        out_specs=pl.BlockSpec((t, d), lambda i: (i, 0)),
        scratch_shapes=[pltpu.VMEM((TOP_K, t, d // LANES, LANES), F32), pltpu.SMEM((TOP_K * t,), jnp.int32),
                        pltpu.SemaphoreType.DMA, pltpu.SemaphoreType.DMA],
        compiler_params=_cparams(("arbitrary",)),
        name="moe_combine",
    )(x2, gates, mod, dest_tiles, ys)


def _moe_layer(x, mod, ng, rw, rb, layer, w_gu, b_gu, w_dn, b_dn, t):
    b, s, d = x.shape
    n = b * s
    x2 = x.reshape(n, d)
    tiles_per_batch = s // t
    h, gates, meta, cnt = _route(x2, mod, ng, rw, rb, t, tiles_per_batch)
    counts = cnt[0, :N_EXPERTS].astype(jnp.int32)
    padded = (counts + EXPERT_ROWS - 1) // EXPERT_ROWS * EXPERT_ROWS
    pend = jnp.cumsum(padded)
    pstart = pend - padded
    n_blocks = -(-(n * TOP_K) // EXPERT_ROWS) + N_EXPERTS
    p_rows = n_blocks * EXPERT_ROWS
    rank = meta[:, :TOP_K]
    eidx = meta[:, TOP_K:2 * TOP_K]
    experts = jnp.arange(N_EXPERTS, dtype=jnp.int32)
    dest = rank + jnp.sum(jnp.where(eidx[..., None] == experts, pstart, 0), axis=-1)
    dest_tiles = dest.reshape(n // t, t, TOP_K).transpose(0, 2, 1).reshape(n // t, TOP_K * t)
    block_row0 = jnp.arange(n_blocks, dtype=jnp.int32) * EXPERT_ROWS
    block_e = jnp.minimum(jnp.sum((pend[None, :] <= block_row0[:, None]).astype(jnp.int32), axis=1),
                          N_EXPERTS - 1)
    n_used = (pend[-1:] // EXPERT_ROWS).astype(jnp.int32)
    xs = _dispatch(h, dest_tiles, pstart + counts, padded - counts, p_rows, t)
    ys = _experts(xs, block_e, n_used, layer, w_gu, b_gu, w_dn, b_dn)
    out = _combine(x2, gates, mod, dest_tiles, ys, t, tiles_per_batch)
    return out.reshape(b, s, d)


def _kv_kernel(x_ref, mod_ref, ng_ref, wkv_ref, wf_ref, bf_ref, kg_ref, hsum_ref,
               kt_ref, v_ref, qf_ref, k2_ref, fb_ref, carry):
    t, d = x_ref.shape[1], x_ref.shape[2]
    n_pairs = N_HEADS // 2

    @pl.when(pl.program_id(1) == 0)
    def _():
        carry[...] = jnp.zeros(carry.shape, F32)

    mod = mod_ref[0]
    h = _rms(x_ref[0], ng_ref[...]) * (1.0 + mod[1:2, :]) + mod[0:1, :]
    kv = jnp.dot(h.astype(BF16), wkv_ref[...], preferred_element_type=F32)
    k = kv[:, :d]
    v_ref[0] = kv[:, d:].astype(BF16)
    ms = jnp.dot((k * k).astype(BF16), hsum_ref[...], preferred_element_type=F32)
    k = k * lax.rsqrt(ms + EPS) * kg_ref[...]
    k2 = jnp.dot((k * k).astype(BF16), hsum_ref[...], preferred_element_type=F32) * HEAD_DIM
    k2_ref[0, 0] = jnp.max(k2, axis=0, keepdims=True)
    kt = k.T
    for hp in range(n_pairs):
        kt_ref[0, hp, 0, pl.ds(0, LANES), :] = kt[hp * LANES:(hp + 1) * LANES, :].astype(BF16)

    fz = jnp.dot(h, wf_ref[...], precision=HIGHEST, preferred_element_type=F32) + bf_ref[...]
    ls = jax.nn.log_sigmoid(fz)
    r_i = lax.broadcasted_iota(jnp.int32, (t, t), 0)
    c_i = lax.broadcasted_iota(jnp.int32, (t, t), 1)
    tri = (c_i <= r_i).astype(F32)
    cum = jnp.dot(tri, ls, precision=HIGHEST, preferred_element_type=F32) + carry[0:1, :]
    carry[...] = jnp.broadcast_to(cum[t - 1:t, :], carry.shape)
    f2 = cum * LOG2E
    row8 = lax.broadcasted_iota(jnp.int32, (8, LANES), 0)
    fb_ref[0, 0] = jnp.where(row8 == 0, f2[0:1, :], jnp.where(row8 == 1, f2[t - 1:t, :], 0.0))
    f2t = f2.T
    lane = lax.broadcasted_iota(jnp.int32, (t, LANES), 1)
    sub = lax.broadcasted_iota(jnp.int32, (LANES, t), 0)
    q_pieces = _split3(f2)
    k_pieces = _split3(-f2t)
    for hp in range(n_pairs):
        q_aug = jnp.zeros((t, LANES), F32)
        k_aug = jnp.zeros((LANES, t), F32)
        for hh in range(2):
            head = 2 * hp + hh
            o = hh * 2 * N_FPIECES
            for p in range(N_FPIECES):
                q_aug = jnp.where(lane == o + p, q_pieces[p][:, head:head + 1], q_aug)
                q_aug = jnp.where(lane == o + N_FPIECES + p, 1.0, q_aug)
                k_aug = jnp.where(sub == o + p, 1.0, k_aug)
                k_aug = jnp.where(sub == o + N_FPIECES + p, k_pieces[p][head:head + 1, :], k_aug)
        qf_ref[0, hp] = q_aug.astype(BF16)
        kt_ref[0, hp, 0, pl.ds(LANES, LANES), :] = k_aug.astype(BF16)


def _shared_kv(x, kvmod, ng, w_kvf, b_f, k_norm_g, t):
    b, s, d = x.shape
    n_pairs = N_HEADS // 2
    wkv = w_kvf[:, :2 * d].astype(BF16)
    wf = jnp.zeros((d, LANES), F32).at[:, :N_HEADS].set(w_kvf[:, 2 * d:])
    bf = jnp.zeros((1, LANES), F32).at[0, :N_HEADS].set(b_f)
    kg = jnp.tile(k_norm_g, N_HEADS).reshape(1, d)
    head_of = jnp.arange(d) // HEAD_DIM
    hsum = ((head_of[:, None] == head_of[None, :]).astype(F32) / HEAD_DIM).astype(BF16)
    const = lambda shape: pl.BlockSpec(shape, lambda bi, si: (0,) * len(shape))
    return pl.pallas_call(
        _kv_kernel,
        out_shape=(
            jax.ShapeDtypeStruct((b, n_pairs, s // t, 2 * LANES, t), BF16),
            jax.ShapeDtypeStruct((b, s, d), BF16),
            jax.ShapeDtypeStruct((b, n_pairs, s, LANES), BF16),
            jax.ShapeDtypeStruct((b, s // t, 1, d), F32),
            jax.ShapeDtypeStruct((b, s // t, 8, LANES), F32),
        ),
        grid=(b, s // t),
        in_specs=[
            pl.BlockSpec((1, t, d), lambda bi, si: (bi, si, 0)),
            pl.BlockSpec((1, 2, d), lambda bi, si: (bi, 0, 0)),
            const((1, d)), const((d, 2 * d)), const((d, LANES)), const((1, LANES)), const((1, d)),
            const((d, d)),
        ],
        out_specs=(
            pl.BlockSpec((1, n_pairs, 1, 2 * LANES, t), lambda bi, si: (bi, 0, si, 0, 0)),
            pl.BlockSpec((1, t, d), lambda bi, si: (bi, si, 0)),
            pl.BlockSpec((1, n_pairs, t, LANES), lambda bi, si: (bi, 0, si, 0)),
            pl.BlockSpec((1, 1, 1, d), lambda bi, si: (bi, si, 0, 0)),
            pl.BlockSpec((1, 1, 8, LANES), lambda bi, si: (bi, si, 0, 0)),
        ),
        scratch_shapes=[pltpu.VMEM((8, LANES), F32)],
        compiler_params=_cparams(("arbitrary", "arbitrary")),
        name="shared_kv",
    )(x, kvmod, ng.reshape(1, d), wkv, wf, bf, kg, hsum), hsum


def _qg_kernel(x_ref, mod_ref, ng_ref, w_ref, qg_ref, hsum_ref, q_ref, g_ref, q2_ref):
    d = x_ref.shape[2]
    mod = mod_ref[0]
    h = _rms(x_ref[0], ng_ref[...]) * (1.0 + mod[1:2, :]) + mod[0:1, :]
    qg = jnp.dot(h.astype(BF16), w_ref[...], preferred_element_type=F32)
    q = qg[:, :d]
    ms = jnp.dot((q * q).astype(BF16), hsum_ref[...], preferred_element_type=F32)
    q = q * lax.rsqrt(ms + EPS) * qg_ref[...] * (LOG2E / math.sqrt(HEAD_DIM))
    q_ref[0] = q.astype(BF16)
    g_ref[0] = jax.nn.sigmoid(qg[:, d:]).astype(BF16)
    q2 = jnp.dot((q * q).astype(BF16), hsum_ref[...], preferred_element_type=F32) * HEAD_DIM
    q2_ref[0, 0] = jnp.max(q2, axis=0, keepdims=True)


def _qg(x, mod, ng, w_qg, q_norm_g, hsum, t):
    b, s, d = x.shape
    const = lambda shape: pl.BlockSpec(shape, lambda bi, si: (0,) * len(shape))
    tile = pl.BlockSpec((1, t, d), lambda bi, si: (bi, si, 0))
    return pl.pallas_call(
        _qg_kernel,
        out_shape=(jax.ShapeDtypeStruct((b, s, d), BF16), jax.ShapeDtypeStruct((b, s, d), BF16),
                   jax.ShapeDtypeStruct((b, s // t, 1, d), F32)),
        grid=(b, s // t),
        in_specs=[tile, pl.BlockSpec((1, 6, d), lambda bi, si: (bi, 0, 0)),
                  const((1, d)), const((d, 2 * d)), const((1, d)), const((d, d))],
        out_specs=(tile, tile, pl.BlockSpec((1, 1, 1, d), lambda bi, si: (bi, si, 0, 0))),
        compiler_params=_cparams(("arbitrary", "arbitrary")),
        name="attn_qg",
    )(x, mod, ng.reshape(1, d), w_qg.astype(BF16), jnp.tile(q_norm_g, N_HEADS).reshape(1, d), hsum)


def _attn_kernel(j0_ref, q_ref, qf_ref, kt_ref, v_ref, o_ref, qa_scr, m_scr, l_scr, acc_scr, *, online):
    tq = q_ref.shape[1]
    tk = kt_ref.shape[4]
    i = pl.program_id(2)
    n_q = pl.num_programs(2)
    head0 = pl.program_id(0) * N_HEADS + 2 * pl.program_id(1)
    first = [j0_ref[(head0 + hh) * n_q + i] for hh in range(2)]
    first_both = jnp.maximum(first[0], first[1])
    lane = lax.broadcasted_iota(jnp.int32, (tq, LANES), 1)
    q2 = q_ref[0]
    qf = qf_ref[0, 0]
    zero = jnp.zeros((), BF16)
    n_aug = 2 * N_FPIECES
    qa_scr[0, :, pl.ds(0, LANES)] = jnp.where(lane < HEAD_DIM, q2, zero)
    qa_scr[0, :, pl.ds(LANES, LANES)] = jnp.where(lane < n_aug, qf, zero)
    qa_scr[1, :, pl.ds(0, LANES)] = jnp.where(lane >= HEAD_DIM, q2, zero)
    qa_scr[1, :, pl.ds(LANES, LANES)] = jnp.where(jnp.logical_and(lane >= n_aug, lane < 2 * n_aug), qf, zero)
    if online:
        m_scr[...] = jnp.full(m_scr.shape, NEG_BIG, F32)
    l_scr[...] = jnp.zeros(l_scr.shape, F32)
    acc_scr[...] = jnp.zeros(acc_scr.shape, F32)

    def tile(j, masked, heads):
        kt = kt_ref[0, 0, j]
        vv = v_ref[0, pl.ds(pl.multiple_of(j * tk, tk), tk), :]
        for hh in heads:
            s = jnp.dot(qa_scr[hh], kt, preferred_element_type=F32)
            if masked:
                r_i = lax.broadcasted_iota(jnp.int32, (tq, tk), 0)
                c_i = lax.broadcasted_iota(jnp.int32, (tq, tk), 1)
                s = jnp.where(c_i <= r_i, s, NEG_BIG)
            if online:
                m_prev = m_scr[hh]
                m_new = jnp.maximum(m_prev, jnp.max(s, axis=-1, keepdims=True))
                alpha = jnp.exp2(m_prev - m_new)
                p = jnp.exp2(s - m_new[:, 0:1])
                l_scr[hh] = alpha * l_scr[hh] + jnp.sum(p, axis=-1, keepdims=True)
                acc_scr[hh] = alpha * acc_scr[hh] + jnp.dot(p.astype(BF16), vv, preferred_element_type=F32)
                m_scr[hh] = m_new
            else:
                p = jnp.exp2(s)
                part = p[:, 0:LANES]
                for c in range(1, tk // LANES):
                    part = part + p[:, c * LANES:(c + 1) * LANES]
                l_scr[hh] = l_scr[hh] + part
                acc_scr[hh] = acc_scr[hh] + jnp.dot(p.astype(BF16), vv, preferred_element_type=F32)

    def full_tiles(heads):
        def body(j, carry):
            tile(j, False, heads)
            return carry
        return body

    lax.fori_loop(first[0], first_both, full_tiles((0,)), 0)
    lax.fori_loop(first[1], first_both, full_tiles((1,)), 0)
    lax.fori_loop(first_both, i, full_tiles((0, 1)), 0)
    tile(i, True, (0, 1))
    if online:
        l0, l1 = l_scr[0], l_scr[1]
    else:
        l0 = jnp.sum(l_scr[0], axis=-1, keepdims=True)
        l1 = jnp.sum(l_scr[1], axis=-1, keepdims=True)
    o_ref[0] = jnp.where(lane < HEAD_DIM, acc_scr[0] / l0, acc_scr[1] / l1).astype(BF16)


def _attention(j0, q, qf, kt, v, t, online):
    b, s, d = q.shape
    n_pairs = N_HEADS // 2
    nkv = s // t
    return pl.pallas_call(
        functools.partial(_attn_kernel, online=online),
        out_shape=jax.ShapeDtypeStruct((b, s, d), BF16),
        grid_spec=pltpu.PrefetchScalarGridSpec(
            num_scalar_prefetch=1,
            grid=(b, n_pairs, s // t),
            in_specs=[
                pl.BlockSpec((1, t, LANES), lambda bi, hp, i, j0r: (bi, i, hp)),
                pl.BlockSpec((1, 1, t, LANES), lambda bi, hp, i, j0r: (bi, hp, i, 0)),
                pl.BlockSpec((1, 1, nkv, 2 * LANES, t), lambda bi, hp, i, j0r: (bi, hp, 0, 0, 0)),
                pl.BlockSpec((1, s, LANES), lambda bi, hp, i, j0r: (bi, 0, hp)),
            ],
            out_specs=pl.BlockSpec((1, t, LANES), lambda bi, hp, i, j0r: (bi, i, hp)),
            scratch_shapes=[
                pltpu.VMEM((2, t, 2 * LANES), BF16),
                pltpu.VMEM((2, t, LANES), F32),
                pltpu.VMEM((2, t, LANES), F32),
                pltpu.VMEM((2, t, LANES), F32),
            ],
        ),
        compiler_params=_cparams(("arbitrary", "arbitrary", "arbitrary")),
        name="fox_attention_online" if online else "fox_attention",
    )(j0, q, qf, kt, v)


def _attn_out_kernel(x_ref, o_ref, g_ref, mod_ref, w_ref, out_ref):
    og = o_ref[0] * g_ref[0]
    y = jnp.dot(og, w_ref[...], preferred_element_type=F32)
    out_ref[0] = x_ref[0] + mod_ref[0][2:3, :] * y


def _attn_out(x, o, g, mod, w_o, t):
    b, s, d = x.shape
    tile = pl.BlockSpec((1, t, d), lambda bi, si: (bi, si, 0))
    return pl.pallas_call(
        _attn_out_kernel,
        out_shape=jax.ShapeDtypeStruct((b, s, d), F32),
        grid=(b, s // t),
        in_specs=[tile, tile, tile, pl.BlockSpec((1, 6, d), lambda bi, si: (bi, 0, 0)),
                  pl.BlockSpec((d, d), lambda bi, si: (0, 0))],
        out_specs=tile,
        compiler_params=_cparams(("arbitrary", "arbitrary")),
        name="attn_out",
    )(x, o, g, mod, w_o.astype(BF16))


def _final_norm_kernel(x_ref, g_ref, o_ref):
    o_ref[...] = _rms(x_ref[...], g_ref[...])


def _final_norm(x2, g, t):
    n, d = x2.shape
    return pl.pallas_call(
        _final_norm_kernel,
        out_shape=jax.ShapeDtypeStruct((n, d), F32),
        grid=(n // t,),
        in_specs=[pl.BlockSpec((t, d), lambda i: (i, 0)), pl.BlockSpec((1, d), lambda i: (0, 0))],
        out_specs=pl.BlockSpec((t, d), lambda i: (i, 0)),
        compiler_params=_cparams(("arbitrary",)),
        name="final_norm",
    )(x2, g.reshape(1, d))


def _fox_layer(x, mod, ng, w_qg, q_norm_g, w_o, kv, t):
    (kt, v, qf, k2, fb), hsum = kv
    n_t = x.shape[1] // t
    q, g, q2 = _qg(x, mod, ng, w_qg, q_norm_g, hsum, t)
    bound = jnp.sqrt(jnp.max(q2) * jnp.max(k2))
    f_first, f_last = fb[:, :, 0, :N_HEADS], fb[:, :, 1, :N_HEADS]
    best = (bound * BOUND_SLACK + f_first[:, :, None, :]) - f_last[:, None, :, :]
    before = jnp.arange(n_t)[None, :] < jnp.arange(n_t)[:, None]
    dead = jnp.logical_and(best < ZERO_WEIGHT_EXPONENT, before[None, :, :, None])
    j0 = jnp.sum(dead.astype(jnp.int32), axis=2).transpose(0, 2, 1).reshape(-1)
    o = lax.cond(bound <= DIRECT_EXP_LIMIT,
                 functools.partial(_attention, t=t, online=False),
                 lambda j0_, *rest: _attention(jnp.zeros_like(j0_), *rest, t=t, online=True),
                 j0, q, qf, kt, v)
    return _attn_out(x, o, g, mod, w_o, t)


def kernel(x, c, mod_w, mod_b, norm1_g, norm2_g, conv_w_pw1, conv_b_pw1, conv_w_dw, conv_b_dw, conv_ln_g, conv_ln_b, conv_w_pw2, conv_b_pw2, kv_mod_w, kv_mod_b, kv_norm_g, w_kvf, b_f, k_norm_g, attn_w_qg, q_norm_g, attn_w_o, moe_router_w, moe_router_b, moe_w_gu, moe_b_gu, moe_w_down, moe_b_down, final_norm_g):
    b, s, d = x.shape
    depth = mod_w.shape[0]
    n_a = conv_w_pw1.shape[0]
    t = min(512, s)
    c8 = jnp.zeros((8, d), F32).at[:b].set(c)
    mods = _mods(c8, mod_w, mod_b)[:, :b].reshape(depth, b, 6, d)
    kvmod = _mods(c8, kv_mod_w[None], kv_mod_b[None])[0, :b].reshape(b, 2, d)
    kv = None
    for l in range(depth):
        if l < n_a:
            x = _conv_layer(x, mods[l], norm1_g[l], conv_w_pw1[l], conv_b_pw1[l], conv_w_dw[l],
                            conv_b_dw[l], conv_ln_g[l], conv_ln_b[l], conv_w_pw2[l], conv_b_pw2[l], t)
        else:
            lb = l - n_a
            x = _fox_layer(x, mods[l], norm1_g[l], attn_w_qg[lb], q_norm_g[lb], attn_w_o[lb], kv, t)
        x = _moe_layer(x, mods[l], norm2_g[l], moe_router_w[l], moe_router_b[l], l, moe_w_gu,
                       moe_b_gu, moe_w_down, moe_b_down, t)
        if l == n_a - 1:
            kv = _shared_kv(x, kvmod, kv_norm_g, w_kvf, b_f, k_norm_g, t)
    return _final_norm(x.reshape(b * s, d), final_norm_g, t).reshape(b, s, d)
```

```python
import functools
import math

import jax
import jax.numpy as jnp
from jax import lax
from jax.experimental import pallas as pl
from jax.experimental.pallas import tpu as pltpu

N_HEADS = 16
HEAD_DIM = 64
CONV_WIDTH = 31
N_EXPERTS = 32
TOP_K = 4
SWIGLU_ALPHA = 1.702
SWIGLU_LIMIT = 7.0
EPS = 1e-6

LANES = 128
HALO = 32
CONV_ROWS = 16
EXPERT_ROWS = 512
VMEM_LIMIT = 56 * 1024 * 1024
LOG2E = 1.4426950408889634
NEG_BIG = -1e30
N_FPIECES = 3
DIRECT_EXP_LIMIT = 60.0
ZERO_WEIGHT_EXPONENT = -160.0
BOUND_SLACK = 1.05

F32 = jnp.float32
BF16 = jnp.bfloat16
HIGHEST = lax.Precision.HIGHEST


def _cparams(sem):
    return pltpu.CompilerParams(dimension_semantics=sem, vmem_limit_bytes=VMEM_LIMIT)


def _rms(x, g):
    return x * lax.rsqrt(jnp.mean(x * x, axis=-1, keepdims=True) + EPS) * g


def _split3(f):
    hi = f.astype(BF16).astype(F32)
    r1 = f - hi
    mid = r1.astype(BF16).astype(F32)
    lo = (r1 - mid).astype(BF16).astype(F32)
    return hi, mid, lo


def _mods_kernel(c_ref, w_ref, b_ref, o_ref):
    c = c_ref[...]
    ca = c * jax.nn.sigmoid(c)
    o_ref[0] = jnp.dot(ca, w_ref[0], precision=HIGHEST, preferred_element_type=F32) + b_ref[0]


def _mods(c8, w, b):
    n_l, d, m = w.shape
    tn = min(m, 1024)
    return pl.pallas_call(
        _mods_kernel,
        out_shape=jax.ShapeDtypeStruct((n_l, 8, m), F32),
        grid=(n_l, m // tn),
        in_specs=[
            pl.BlockSpec((8, d), lambda l, j: (0, 0)),
            pl.BlockSpec((1, d, tn), lambda l, j: (l, 0, j)),
            pl.BlockSpec((1, 1, tn), lambda l, j: (l, 0, j)),
        ],
        out_specs=pl.BlockSpec((1, 8, tn), lambda l, j: (l, 0, j)),
        compiler_params=_cparams(("arbitrary", "arbitrary")),
        name="mods",
    )(c8, w, b.reshape(n_l, 1, m))


def _conv_kernel(x_ref, mod_ref, ng_ref, w1_ref, b1_ref, wdw_ref, bdw_ref, lng_ref, lnb_ref,
                 w2_ref, b2_ref, o_ref, ubuf, cbuf):
    t, d = x_ref.shape[1], x_ref.shape[2]
    slab = ubuf.shape[1:]

    @pl.when(pl.program_id(1) == 0)
    def _():
        ubuf[pl.ds(0, HALO)] = jnp.zeros((HALO,) + slab, F32)

    x = x_ref[0]
    mod = mod_ref[0]
    h = _rms(x, ng_ref[...]) * (1.0 + mod[1:2, :]) + mod[0:1, :]
    u = jnp.dot(h.astype(BF16), w1_ref[...], preferred_element_type=F32) + b1_ref[...]
    u = u[:, :d] * jax.nn.sigmoid(u[:, d:])
    ubuf[pl.ds(HALO, t)] = u.reshape((t,) + slab)

    def conv_rows(c, carry):
        base = pl.multiple_of(c * CONV_ROWS, CONV_ROWS)
        acc = jnp.zeros((CONV_ROWS,) + slab, F32)
        for j in range(CONV_WIDTH):
            acc = acc + ubuf[pl.ds(base + (HALO - (CONV_WIDTH - 1) + j), CONV_ROWS)] * wdw_ref[j]
        cbuf[pl.ds(base, CONV_ROWS)] = acc
        return carry

    lax.fori_loop(0, t // CONV_ROWS, conv_rows, 0)
    ubuf[pl.ds(0, HALO)] = ubuf[pl.ds(t, HALO)]
    acc = cbuf[...].reshape(t, d) + bdw_ref[...]
    mu = jnp.mean(acc, axis=-1, keepdims=True)
    cen = acc - mu
    var = jnp.mean(cen * cen, axis=-1, keepdims=True)
    y = cen * lax.rsqrt(var + EPS) * lng_ref[...] + lnb_ref[...]
    y = y * jax.nn.sigmoid(y)
    y = jnp.dot(y.astype(BF16), w2_ref[...], preferred_element_type=F32) + b2_ref[...]
    o_ref[0] = x + mod[2:3, :] * y


def _conv_layer(x, mod, ng, w1, b1, wdw, bdw, lng, lnb, w2, b2, t):
    b, s, d = x.shape
    row = lambda a: a.reshape(1, -1)
    slab = (d // LANES, LANES)
    wdw_p = jnp.zeros((HALO, d), F32).at[:CONV_WIDTH].set(wdw).reshape((HALO,) + slab)
    const = lambda shape: pl.BlockSpec(shape, lambda bi, si: (0,) * len(shape))
    return pl.pallas_call(
        _conv_kernel,
        out_shape=jax.ShapeDtypeStruct((b, s, d), F32),
        grid=(b, s // t),
        in_specs=[
            pl.BlockSpec((1, t, d), lambda bi, si: (bi, si, 0)),
            pl.BlockSpec((1, 6, d), lambda bi, si: (bi, 0, 0)),
            const((1, d)), const((d, 2 * d)), const((1, 2 * d)), const((HALO,) + slab), const((1, d)),
            const((1, d)), const((1, d)), const((d, d)), const((1, d)),
        ],
        out_specs=pl.BlockSpec((1, t, d), lambda bi, si: (bi, si, 0)),
        scratch_shapes=[pltpu.VMEM((t + HALO,) + slab, F32), pltpu.VMEM((t,) + slab, F32)],
        compiler_params=_cparams(("arbitrary", "arbitrary")),
        name="conv_layer",
    )(x, mod, row(ng), w1.astype(BF16), row(b1), wdw_p, row(bdw), row(lng), row(lnb),
      w2.astype(BF16), row(b2))


def _route_kernel(x_ref, mod_ref, ng_ref, rw_ref, rb_ref, h_ref, gate_ref, meta_ref, cnt_ref):
    t = x_ref.shape[0]

    @pl.when(pl.program_id(0) == 0)
    def _():
        cnt_ref[...] = jnp.zeros(cnt_ref.shape, F32)

    mod = mod_ref[0]
    h = _rms(x_ref[...], ng_ref[...]) * (1.0 + mod[4:5, :]) + mod[3:4, :]
    h_ref[...] = h.reshape(h_ref.shape)
    logits = jnp.dot(h, rw_ref[...], precision=HIGHEST, preferred_element_type=F32) + rb_ref[...]
    lane = lax.broadcasted_iota(jnp.int32, (t, LANES), 1).astype(F32)
    work = logits
    vals, idxs = [], []
    for _ in range(TOP_K):
        m = jnp.max(work, axis=-1, keepdims=True)
        idx = jnp.min(jnp.where(work == m, lane, float(LANES)), axis=-1, keepdims=True)
        vals.append(m)
        idxs.append(idx)
        work = jnp.where(lane == idx, -jnp.inf, work)
    exps = [jnp.exp(v - vals[0]) for v in vals]
    denom = exps[0] + exps[1] + exps[2] + exps[3]
    onehot = jnp.zeros((t, LANES), F32)
    for idx in idxs:
        onehot = onehot + (lane == idx).astype(F32)
    r_i = lax.broadcasted_iota(jnp.int32, (t, t), 0)
    c_i = lax.broadcasted_iota(jnp.int32, (t, t), 1)
    tri = (c_i < r_i).astype(BF16)
    base = jnp.dot(tri, onehot.astype(BF16), preferred_element_type=F32) + cnt_ref[0:1, :]
    gate_out = jnp.zeros((t, LANES), F32)
    meta = jnp.zeros((t, LANES), F32)
    for k in range(TOP_K):
        rank = jnp.sum(jnp.where(lane == idxs[k], base, 0.0), axis=-1, keepdims=True)
        gate_out = jnp.where(lane == k, exps[k] / denom, gate_out)
        meta = jnp.where(lane == k, rank, meta)
        meta = jnp.where(lane == TOP_K + k, idxs[k], meta)
    gate_ref[...] = gate_out
    meta_ref[...] = meta.astype(jnp.int32)
    cnt_ref[...] = cnt_ref[...] + jnp.sum(onehot, axis=0, keepdims=True)


def _route(x2, mod, ng, rw, rb, t, tiles_per_batch):
    n, d = x2.shape
    rw_p = jnp.zeros((d, LANES), F32).at[:, :N_EXPERTS].set(rw)
    rb_p = jnp.full((1, LANES), NEG_BIG, F32).at[0, :N_EXPERTS].set(rb)
    return pl.pallas_call(
        _route_kernel,
        out_shape=(
            jax.ShapeDtypeStruct((n, d // LANES, LANES), F32),
            jax.ShapeDtypeStruct((n, LANES), F32),
            jax.ShapeDtypeStruct((n, LANES), jnp.int32),
            jax.ShapeDtypeStruct((8, LANES), F32),
        ),
        grid=(n // t,),
        in_specs=[
            pl.BlockSpec((t, d), lambda i: (i, 0)),
            pl.BlockSpec((1, 6, d), lambda i: (i // tiles_per_batch, 0, 0)),
            pl.BlockSpec((1, d), lambda i: (0, 0)),
            pl.BlockSpec((d, LANES), lambda i: (0, 0)),
            pl.BlockSpec((1, LANES), lambda i: (0, 0)),
        ],
        out_specs=(
            pl.BlockSpec((t, d // LANES, LANES), lambda i: (i, 0, 0)),
            pl.BlockSpec((t, LANES), lambda i: (i, 0)),
            pl.BlockSpec((t, LANES), lambda i: (i, 0)),
            pl.BlockSpec((8, LANES), lambda i: (0, 0)),
        ),
        compiler_params=_cparams(("arbitrary",)),
        name="moe_route",
    )(x2, mod, ng.reshape(1, d), rw_p, rb_p)


def _dispatch_kernel(pad0_ref, padn_ref, h_ref, dest_hbm, xs_hbm, zrow, idx_smem, sem_idx, sem_rows, sem_pad):
    t = h_ref.shape[0]
    i = pl.program_id(0)

    @pl.when(i == 0)
    def _():
        zrow[...] = jnp.zeros(zrow.shape, F32)
        for start in (True, False):
            def per_expert(e, carry, start=start):
                def per_row(r, c):
                    pad_cp = pltpu.make_async_copy(zrow, xs_hbm.at[pad0_ref[e] + r], sem_pad)
                    if start:
                        pad_cp.start()
                    else:
                        pad_cp.wait()
                    return c
                return lax.fori_loop(0, padn_ref[e], per_row, carry)
            lax.fori_loop(0, N_EXPERTS, per_expert, 0)

    cp = pltpu.make_async_copy(dest_hbm.at[i], idx_smem, sem_idx)
    cp.start()
    cp.wait()

    def row_copy(tok, dst):
        return pltpu.make_async_copy(h_ref.at[tok], xs_hbm.at[dst], sem_rows)

    def issue(tok, carry):
        for k in range(TOP_K):
            row_copy(tok, idx_smem[k * t + tok]).start()
        return carry

    lax.fori_loop(0, t, issue, 0)

    def drain(tok, carry):
        for k in range(TOP_K):
            row_copy(tok, idx_smem[k * t + tok]).wait()
        return carry

    lax.fori_loop(0, t, drain, 0)


def _dispatch(h, dest_tiles, pad_start, pad_n, p_rows, t):
    n, sub, _ = h.shape
    return pl.pallas_call(
        _dispatch_kernel,
        out_shape=jax.ShapeDtypeStruct((p_rows, sub, LANES), F32),
        grid_spec=pltpu.PrefetchScalarGridSpec(
            num_scalar_prefetch=2,
            grid=(n // t,),
            in_specs=[
                pl.BlockSpec((t, sub, LANES), lambda i, p0, pn: (i, 0, 0)),
                pl.BlockSpec(memory_space=pl.ANY),
            ],
            out_specs=pl.BlockSpec(memory_space=pl.ANY),
            scratch_shapes=[pltpu.VMEM((sub, LANES), F32), pltpu.SMEM((TOP_K * t,), jnp.int32),
                            pltpu.SemaphoreType.DMA, pltpu.SemaphoreType.DMA, pltpu.SemaphoreType.DMA],
        ),
        compiler_params=_cparams(("arbitrary",)),
        name="moe_dispatch",
    )(pad_start, pad_n, h, dest_tiles)


def _experts_kernel(be_ref, nu_ref, xs_ref, wgu_ref, bgu_ref, wdn_ref, bdn_ref, ys_ref, wgu_bf, wdn_bf):
    b = pl.program_id(0)
    f = wdn_ref.shape[2]
    e = be_ref[b]
    e_prev = be_ref[jnp.maximum(b - 1, 0)]

    @pl.when(jnp.logical_or(b == 0, e != e_prev))
    def _():
        wgu_bf[...] = wgu_ref[0, 0].astype(BF16)
        wdn_bf[...] = wdn_ref[0, 0].astype(BF16)

    @pl.when(b < nu_ref[0])
    def _():
        rows, sub, _ = xs_ref.shape
        x = xs_ref[...].reshape(rows, sub * LANES).astype(BF16)
        gu = jnp.dot(x, wgu_bf[...], preferred_element_type=F32) + bgu_ref[0, 0]
        x_glu = jnp.minimum(gu[:, :f], SWIGLU_LIMIT)
        x_lin = jnp.clip(gu[:, f:], -SWIGLU_LIMIT, SWIGLU_LIMIT)
        act = x_glu * jax.nn.sigmoid(SWIGLU_ALPHA * x_glu) * (x_lin + 1.0)
        y = jnp.dot(act.astype(BF16), wdn_bf[...], preferred_element_type=F32) + bdn_ref[0, 0]
        ys_ref[...] = y.reshape(ys_ref.shape)


def _experts(xs, block_e, n_used, layer, w_gu, b_gu, w_dn, b_dn):
    p_rows, sub, _ = xs.shape
    n_l, n_e, d, f2 = w_gu.shape
    f = f2 // 2
    n_blocks = p_rows // EXPERT_ROWS
    row_map = lambda b, be, nu: (jnp.minimum(b, nu[0] - 1), 0, 0)
    exp_map = lambda b, be, nu: (layer, be[b], 0, 0)
    return pl.pallas_call(
        _experts_kernel,
        out_shape=jax.ShapeDtypeStruct((p_rows, sub, LANES), F32),
        grid_spec=pltpu.PrefetchScalarGridSpec(
            num_scalar_prefetch=2,
            grid=(n_blocks,),
            in_specs=[
                pl.BlockSpec((EXPERT_ROWS, sub, LANES), row_map),
                pl.BlockSpec((1, 1, d, f2), exp_map),
                pl.BlockSpec((1, 1, 1, f2), exp_map),
                pl.BlockSpec((1, 1, f, d), exp_map),
                pl.BlockSpec((1, 1, 1, d), exp_map),
            ],
            out_specs=pl.BlockSpec((EXPERT_ROWS, sub, LANES), row_map),
            scratch_shapes=[pltpu.VMEM((d, f2), BF16), pltpu.VMEM((f, d), BF16)],
        ),
        compiler_params=_cparams(("arbitrary",)),
        name="moe_experts",
    )(block_e, n_used, xs, w_gu, b_gu.reshape(n_l, n_e, 1, f2), w_dn, b_dn.reshape(n_l, n_e, 1, d))


def _combine_kernel(x_ref, gate_ref, mod_ref, fng_ref, dest_hbm, ys_hbm, o_ref, buf, idx_smem, sem_idx, sem_rows,
                    *, final):
    t = x_ref.shape[0]
    i = pl.program_id(0)
    cp = pltpu.make_async_copy(dest_hbm.at[i], idx_smem, sem_idx)
    cp.start()
    cp.wait()

    def row_copy(k, tok, src):
        return pltpu.make_async_copy(ys_hbm.at[src], buf.at[k, tok], sem_rows)

    def issue(tok, carry):
        for k in range(TOP_K):
            row_copy(k, tok, idx_smem[k * t + tok]).start()
        return carry

    lax.fori_loop(0, t, issue, 0)

    def drain(tok, carry):
        for k in range(TOP_K):
            row_copy(k, tok, idx_smem[k * t + tok]).wait()
        return carry

    lax.fori_loop(0, t, drain, 0)

    gate = gate_ref[...]
    y = gate[:, 0:1] * buf[0].reshape(x_ref.shape)
    for k in range(1, TOP_K):
        y = y + gate[:, k:k + 1] * buf[k].reshape(x_ref.shape)
    out = x_ref[...] + mod_ref[0][5:6, :] * y
    o_ref[...] = _rms(out, fng_ref[...]) if final else out


def _combine(x2, gates, mod, final_g, dest_tiles, ys, t, tiles_per_batch, final):
    n, d = x2.shape
    return pl.pallas_call(
        functools.partial(_combine_kernel, final=final),
        out_shape=jax.ShapeDtypeStruct((n, d), F32),
        grid=(n // t,),
        in_specs=[
            pl.BlockSpec((t, d), lambda i: (i, 0)),
            pl.BlockSpec((t, LANES), lambda i: (i, 0)),
            pl.BlockSpec((1, 6, d), lambda i: (i // tiles_per_batch, 0, 0)),
            pl.BlockSpec((1, d), lambda i: (0, 0)),
            pl.BlockSpec(memory_space=pl.ANY),
            pl.BlockSpec(memory_space=pl.ANY),
        ],
        out_specs=pl.BlockSpec((t, d), lambda i: (i, 0)),
        scratch_shapes=[pltpu.VMEM((TOP_K, t, d // LANES, LANES), F32), pltpu.SMEM((TOP_K * t,), jnp.int32),
                        pltpu.SemaphoreType.DMA, pltpu.SemaphoreType.DMA],
        compiler_params=_cparams(("arbitrary",)),
        name="moe_combine",
    )(x2, gates, mod, final_g.reshape(1, d), dest_tiles, ys)


def _moe_layer(x, mod, ng, rw, rb, layer, w_gu, b_gu, w_dn, b_dn, final_g, final, t):
    b, s, d = x.shape
    n = b * s
    x2 = x.reshape(n, d)
    tiles_per_batch = s // t
    h, gates, meta, cnt = _route(x2, mod, ng, rw, rb, t, tiles_per_batch)
    counts = cnt[0, :N_EXPERTS].astype(jnp.int32)
    padded = (counts + EXPERT_ROWS - 1) // EXPERT_ROWS * EXPERT_ROWS
    pend = jnp.cumsum(padded)
    pstart = pend - padded
    n_blocks = -(-(n * TOP_K) // EXPERT_ROWS) + N_EXPERTS
    p_rows = n_blocks * EXPERT_ROWS
    rank = meta[:, :TOP_K]
    eidx = meta[:, TOP_K:2 * TOP_K]
    experts = jnp.arange(N_EXPERTS, dtype=jnp.int32)
    dest = rank + jnp.sum(jnp.where(eidx[..., None] == experts, pstart, 0), axis=-1)
    dest_tiles = dest.reshape(n // t, t, TOP_K).transpose(0, 2, 1).reshape(n // t, TOP_K * t)
    block_row0 = jnp.arange(n_blocks, dtype=jnp.int32) * EXPERT_ROWS
    block_e = jnp.minimum(jnp.sum((pend[None, :] <= block_row0[:, None]).astype(jnp.int32), axis=1),
                          N_EXPERTS - 1)
    n_used = (pend[-1:] // EXPERT_ROWS).astype(jnp.int32)
    xs = _dispatch(h, dest_tiles, pstart + counts, padded - counts, p_rows, t)
    ys = _experts(xs, block_e, n_used, layer, w_gu, b_gu, w_dn, b_dn)
    out = _combine(x2, gates, mod, final_g, dest_tiles, ys, t, tiles_per_batch, final)
    return out.reshape(b, s, d)


def _kv_kernel(x_ref, mod_ref, ng_ref, wkv_ref, wf_ref, bf_ref, kg_ref, hsum_ref,
               kt_ref, v_ref, qf_ref, fb_ref, carry):
    t, d = x_ref.shape[1], x_ref.shape[2]
    n_pairs = N_HEADS // 2

    @pl.when(pl.program_id(1) == 0)
    def _():
        carry[...] = jnp.zeros(carry.shape, F32)

    mod = mod_ref[0]
    h = _rms(x_ref[0], ng_ref[...]) * (1.0 + mod[1:2, :]) + mod[0:1, :]
    h_hi = h.astype(BF16)
    kv = jnp.dot(h_hi, wkv_ref[...], preferred_element_type=F32)
    k = kv[:, :d]
    v_ref[0] = kv[:, d:].astype(BF16)
    ms = jnp.dot((k * k).astype(BF16), hsum_ref[...], preferred_element_type=F32)
    k = k * lax.rsqrt(ms + EPS) * kg_ref[...]
    kt = k.T
    for hp in range(n_pairs):
        kt_ref[0, hp, 0, pl.ds(0, LANES), :] = kt[hp * LANES:(hp + 1) * LANES, :].astype(BF16)

    h_lo = (h - h_hi.astype(F32)).astype(BF16)
    fz2 = jnp.dot(h_hi, wf_ref[...], preferred_element_type=F32)
    fz = (fz2[:, :LANES] + fz2[:, LANES:] + bf_ref[...]
          + jnp.dot(h_lo, wf_ref[:, pl.ds(0, LANES)], preferred_element_type=F32))
    ls = jax.nn.log_sigmoid(fz)
    r_i = lax.broadcasted_iota(jnp.int32, (t, t), 0)
    c_i = lax.broadcasted_iota(jnp.int32, (t, t), 1)
    tri = (c_i <= r_i).astype(BF16)
    ls_hi, ls_mid, ls_lo = _split3(ls)
    cum2 = jnp.dot(tri, jnp.concatenate([ls_hi, ls_mid], axis=1).astype(BF16), preferred_element_type=F32)
    cum = (cum2[:, :LANES] + cum2[:, LANES:] + carry[0:1, :]
           + jnp.dot(tri, ls_lo.astype(BF16), preferred_element_type=F32))
    carry[...] = jnp.broadcast_to(cum[t - 1:t, :], carry.shape)
    f2 = cum * LOG2E
    row8 = lax.broadcasted_iota(jnp.int32, (8, LANES), 0)
    fb_ref[0, 0] = jnp.where(row8 == 0, f2[0:1, :], jnp.where(row8 == 1, f2[t - 1:t, :], 0.0))
    f2t = f2.T
    lane = lax.broadcasted_iota(jnp.int32, (t, LANES), 1)
    sub = lax.broadcasted_iota(jnp.int32, (LANES, t), 0)
    q_pieces = _split3(f2)
    k_pieces = _split3(-f2t)
    for hp in range(n_pairs):
        q_aug = jnp.zeros((t, LANES), F32)
        k_aug = jnp.zeros((LANES, t), F32)
        for hh in range(2):
            head = 2 * hp + hh
            o = hh * 2 * N_FPIECES
            for p in range(N_FPIECES):
                q_aug = jnp.where(lane == o + p, q_pieces[p][:, head:head + 1], q_aug)
                q_aug = jnp.where(lane == o + N_FPIECES + p, 1.0, q_aug)
                k_aug = jnp.where(sub == o + p, 1.0, k_aug)
                k_aug = jnp.where(sub == o + N_FPIECES + p, k_pieces[p][head:head + 1, :], k_aug)
        qf_ref[0, hp] = q_aug.astype(BF16)
        kt_ref[0, hp, 0, pl.ds(LANES, LANES), :] = k_aug.astype(BF16)


def _shared_kv(x, kvmod, ng, w_kvf, b_f, k_norm_g, t):
    b, s, d = x.shape
    n_pairs = N_HEADS // 2
    wkv = w_kvf[:, :2 * d].astype(BF16)
    wf = jnp.zeros((d, LANES), F32).at[:, :N_HEADS].set(w_kvf[:, 2 * d:])
    wf_hi = wf.astype(BF16)
    wf2 = jnp.concatenate([wf_hi, (wf - wf_hi.astype(F32)).astype(BF16)], axis=1)
    bf = jnp.zeros((1, LANES), F32).at[0, :N_HEADS].set(b_f)
    kg = jnp.tile(k_norm_g, N_HEADS).reshape(1, d)
    head_of = jnp.arange(d) // HEAD_DIM
    hsum = ((head_of[:, None] == head_of[None, :]).astype(F32) / HEAD_DIM).astype(BF16)
    const = lambda shape: pl.BlockSpec(shape, lambda bi, si: (0,) * len(shape))
    return pl.pallas_call(
        _kv_kernel,
        out_shape=(
            jax.ShapeDtypeStruct((b, n_pairs, s // t, 2 * LANES, t), BF16),
            jax.ShapeDtypeStruct((b, s, d), BF16),
            jax.ShapeDtypeStruct((b, n_pairs, s, LANES), BF16),
            jax.ShapeDtypeStruct((b, s // t, 8, LANES), F32),
        ),
        grid=(b, s // t),
        in_specs=[
            pl.BlockSpec((1, t, d), lambda bi, si: (bi, si, 0)),
            pl.BlockSpec((1, 2, d), lambda bi, si: (bi, 0, 0)),
            const((1, d)), const((d, 2 * d)), const((d, 2 * LANES)), const((1, LANES)), const((1, d)),
            const((d, d)),
        ],
        out_specs=(
            pl.BlockSpec((1, n_pairs, 1, 2 * LANES, t), lambda bi, si: (bi, 0, si, 0, 0)),
            pl.BlockSpec((1, t, d), lambda bi, si: (bi, si, 0)),
            pl.BlockSpec((1, n_pairs, t, LANES), lambda bi, si: (bi, 0, si, 0)),
            pl.BlockSpec((1, 1, 8, LANES), lambda bi, si: (bi, si, 0, 0)),
        ),
        scratch_shapes=[pltpu.VMEM((8, LANES), F32)],
        compiler_params=_cparams(("arbitrary", "arbitrary")),
        name="shared_kv",
    )(x, kvmod, ng.reshape(1, d), wkv, wf2, bf, kg, hsum), hsum


def _qg_kernel(x_ref, mod_ref, ng_ref, w_ref, qg_ref, hsum_ref, q_ref, g_ref):
    d = x_ref.shape[2]
    mod = mod_ref[0]
    h = _rms(x_ref[0], ng_ref[...]) * (1.0 + mod[1:2, :]) + mod[0:1, :]
    qg = jnp.dot(h.astype(BF16), w_ref[...], preferred_element_type=F32)
    q = qg[:, :d]
    ms = jnp.dot((q * q).astype(BF16), hsum_ref[...], preferred_element_type=F32)
    q = q * lax.rsqrt(ms + EPS) * qg_ref[...] * (LOG2E / math.sqrt(HEAD_DIM))
    q_ref[0] = q.astype(BF16)
    g_ref[0] = jax.nn.sigmoid(qg[:, d:]).astype(BF16)


def _qg(x, mod, ng, w_qg, q_norm_g, hsum, t):
    b, s, d = x.shape
    const = lambda shape: pl.BlockSpec(shape, lambda bi, si: (0,) * len(shape))
    tile = pl.BlockSpec((1, t, d), lambda bi, si: (bi, si, 0))
    return pl.pallas_call(
        _qg_kernel,
        out_shape=(jax.ShapeDtypeStruct((b, s, d), BF16), jax.ShapeDtypeStruct((b, s, d), BF16)),
        grid=(b, s // t),
        in_specs=[tile, pl.BlockSpec((1, 6, d), lambda bi, si: (bi, 0, 0)),
                  const((1, d)), const((d, 2 * d)), const((1, d)), const((d, d))],
        out_specs=(tile, tile),
        compiler_params=_cparams(("arbitrary", "arbitrary")),
        name="attn_qg",
    )(x, mod, ng.reshape(1, d), w_qg.astype(BF16), jnp.tile(q_norm_g, N_HEADS).reshape(1, d), hsum)


def _attn_kernel(j0_ref, q_ref, qf_ref, kt_ref, v_ref, o_ref, qa_scr, m_scr, l_scr, acc_scr, *, online):
    tq = q_ref.shape[1]
    tk = kt_ref.shape[4]
    i = pl.program_id(2)
    n_q = pl.num_programs(2)
    head0 = pl.program_id(0) * N_HEADS + 2 * pl.program_id(1)
    first = [j0_ref[(head0 + hh) * n_q + i] for hh in range(2)]
    first_both = jnp.maximum(first[0], first[1])
    lane = lax.broadcasted_iota(jnp.int32, (tq, LANES), 1)
    q2 = q_ref[0]
    qf = qf_ref[0, 0]
    zero = jnp.zeros((), BF16)
    n_aug = 2 * N_FPIECES
    qa_scr[0, :, pl.ds(0, LANES)] = jnp.where(lane < HEAD_DIM, q2, zero)
    qa_scr[0, :, pl.ds(LANES, LANES)] = jnp.where(lane < n_aug, qf, zero)
    qa_scr[1, :, pl.ds(0, LANES)] = jnp.where(lane >= HEAD_DIM, q2, zero)
    qa_scr[1, :, pl.ds(LANES, LANES)] = jnp.where(jnp.logical_and(lane >= n_aug, lane < 2 * n_aug), qf, zero)
    if online:
        m_scr[...] = jnp.full(m_scr.shape, NEG_BIG, F32)
    l_scr[...] = jnp.zeros(l_scr.shape, F32)
    acc_scr[...] = jnp.zeros(acc_scr.shape, F32)

    def tile(j, masked, heads):
        kt = kt_ref[0, 0, j]
        vv = v_ref[0, pl.ds(pl.multiple_of(j * tk, tk), tk), :]
        for hh in heads:
            s = jnp.dot(qa_scr[hh], kt, preferred_element_type=F32)
            if masked:
                r_i = lax.broadcasted_iota(jnp.int32, (tq, tk), 0)
                c_i = lax.broadcasted_iota(jnp.int32, (tq, tk), 1)
                s = jnp.where(c_i <= r_i, s, NEG_BIG)
            if online:
                m_prev = m_scr[hh]
                m_new = jnp.maximum(m_prev, jnp.max(s, axis=-1, keepdims=True))
                alpha = jnp.exp2(m_prev - m_new)
                p = jnp.exp2(s - m_new[:, 0:1])
                l_scr[hh] = alpha * l_scr[hh] + jnp.sum(p, axis=-1, keepdims=True)
                acc_scr[hh] = alpha * acc_scr[hh] + jnp.dot(p.astype(BF16), vv, preferred_element_type=F32)
                m_scr[hh] = m_new
            else:
                p = jnp.exp2(s)
                part = p[:, 0:LANES]
                for c in range(1, tk // LANES):
                    part = part + p[:, c * LANES:(c + 1) * LANES]
                l_scr[hh] = l_scr[hh] + part
                acc_scr[hh] = acc_scr[hh] + jnp.dot(p.astype(BF16), vv, preferred_element_type=F32)

    def full_tiles(heads):
        def body(j, carry):
            tile(j, False, heads)
            return carry
        return body

    lax.fori_loop(first[0], first_both, full_tiles((0,)), 0)
    lax.fori_loop(first[1], first_both, full_tiles((1,)), 0)
    lax.fori_loop(first_both, i, full_tiles((0, 1)), 0)
    tile(i, True, (0, 1))
    if online:
        l0, l1 = l_scr[0], l_scr[1]
    else:
        l0 = jnp.sum(l_scr[0], axis=-1, keepdims=True)
        l1 = jnp.sum(l_scr[1], axis=-1, keepdims=True)
    o_ref[0] = jnp.where(lane < HEAD_DIM, acc_scr[0] / l0, acc_scr[1] / l1).astype(BF16)


def _attention(j0, q, qf, kt, v, t, online):
    b, s, d = q.shape
    n_pairs = N_HEADS // 2
    nkv = s // t
    return pl.pallas_call(
        functools.partial(_attn_kernel, online=online),
        out_shape=jax.ShapeDtypeStruct((b, s, d), BF16),
        grid_spec=pltpu.PrefetchScalarGridSpec(
            num_scalar_prefetch=1,
            grid=(b, n_pairs, s // t),
            in_specs=[
                pl.BlockSpec((1, t, LANES), lambda bi, hp, i, j0r: (bi, i, hp)),
                pl.BlockSpec((1, 1, t, LANES), lambda bi, hp, i, j0r: (bi, hp, i, 0)),
                pl.BlockSpec((1, 1, nkv, 2 * LANES, t), lambda bi, hp, i, j0r: (bi, hp, 0, 0, 0)),
                pl.BlockSpec((1, s, LANES), lambda bi, hp, i, j0r: (bi, 0, hp)),
            ],
            out_specs=pl.BlockSpec((1, t, LANES), lambda bi, hp, i, j0r: (bi, i, hp)),
            scratch_shapes=[
                pltpu.VMEM((2, t, 2 * LANES), BF16),
                pltpu.VMEM((2, t, LANES), F32),
                pltpu.VMEM((2, t, LANES), F32),
                pltpu.VMEM((2, t, LANES), F32),
            ],
        ),
        compiler_params=_cparams(("arbitrary", "arbitrary", "arbitrary")),
        name="fox_attention_online" if online else "fox_attention",
    )(j0, q, qf, kt, v)


def _attn_out_kernel(x_ref, o_ref, g_ref, mod_ref, w_ref, out_ref):
    og = o_ref[0] * g_ref[0]
    y = jnp.dot(og, w_ref[...], preferred_element_type=F32)
    out_ref[0] = x_ref[0] + mod_ref[0][2:3, :] * y


def _attn_out(x, o, g, mod, w_o, t):
    b, s, d = x.shape
    tile = pl.BlockSpec((1, t, d), lambda bi, si: (bi, si, 0))
    return pl.pallas_call(
        _attn_out_kernel,
        out_shape=jax.ShapeDtypeStruct((b, s, d), F32),
        grid=(b, s // t),
        in_specs=[tile, tile, tile, pl.BlockSpec((1, 6, d), lambda bi, si: (bi, 0, 0)),
                  pl.BlockSpec((d, d), lambda bi, si: (0, 0))],
        out_specs=tile,
        compiler_params=_cparams(("arbitrary", "arbitrary")),
        name="attn_out",
    )(x, o, g, mod, w_o.astype(BF16))


def _fox_layer(x, mod, ng, w_qg, q_norm_g, k_norm_g, w_o, kv, t):
    (kt, v, qf, fb), hsum = kv
    n_t = x.shape[1] // t
    q, g = _qg(x, mod, ng, w_qg, q_norm_g, hsum, t)
    bound = (HEAD_DIM * jnp.max(jnp.abs(q_norm_g)) * jnp.max(jnp.abs(k_norm_g))
             * (LOG2E / math.sqrt(HEAD_DIM)))
    f_first, f_last = fb[:, :, 0, :N_HEADS], fb[:, :, 1, :N_HEADS]
    best = (bound * BOUND_SLACK + f_first[:, :, None, :]) - f_last[:, None, :, :]
    before = jnp.arange(n_t)[None, :] < jnp.arange(n_t)[:, None]
    dead = jnp.logical_and(best < ZERO_WEIGHT_EXPONENT, before[None, :, :, None])
    j0 = jnp.sum(dead.astype(jnp.int32), axis=2).transpose(0, 2, 1).reshape(-1)
    o = lax.cond(bound <= DIRECT_EXP_LIMIT,
                 functools.partial(_attention, t=t, online=False),
                 lambda j0_, *rest: _attention(jnp.zeros_like(j0_), *rest, t=t, online=True),
                 j0, q, qf, kt, v)
    return _attn_out(x, o, g, mod, w_o, t)


def kernel(x, c, mod_w, mod_b, norm1_g, norm2_g, conv_w_pw1, conv_b_pw1, conv_w_dw, conv_b_dw, conv_ln_g, conv_ln_b, conv_w_pw2, conv_b_pw2, kv_mod_w, kv_mod_b, kv_norm_g, w_kvf, b_f, k_norm_g, attn_w_qg, q_norm_g, attn_w_o, moe_router_w, moe_router_b, moe_w_gu, moe_b_gu, moe_w_down, moe_b_down, final_norm_g):
    b, s, d = x.shape
    depth = mod_w.shape[0]
    n_a = conv_w_pw1.shape[0]
    t = min(512, s)
    c8 = jnp.zeros((8, d), F32).at[:b].set(c)
    mods = _mods(c8, mod_w, mod_b)[:, :b].reshape(depth, b, 6, d)
    kvmod = _mods(c8, kv_mod_w[None], kv_mod_b[None])[0, :b].reshape(b, 2, d)
    kv = None
    for l in range(depth):
        if l < n_a:
            x = _conv_layer(x, mods[l], norm1_g[l], conv_w_pw1[l], conv_b_pw1[l], conv_w_dw[l],
                            conv_b_dw[l], conv_ln_g[l], conv_ln_b[l], conv_w_pw2[l], conv_b_pw2[l], t)
        else:
            lb = l - n_a
            x = _fox_layer(x, mods[l], norm1_g[l], attn_w_qg[lb], q_norm_g[lb], k_norm_g, attn_w_o[lb], kv, t)
        x = _moe_layer(x, mods[l], norm2_g[l], moe_router_w[l], moe_router_b[l], l, moe_w_gu,
                       moe_b_gu, moe_w_down, moe_b_down, final_norm_g, l == depth - 1, t)
        if l == n_a - 1:
            kv = _shared_kv(x, kvmod, kv_norm_g, w_kvf, b_f, k_norm_g, t)
    return x
```

```python
import functools
import math

import jax
import jax.numpy as jnp
from jax import lax
from jax.experimental import pallas as pl
from jax.experimental.pallas import tpu as pltpu

N_HEADS = 16
HEAD_DIM = 64
CONV_WIDTH = 31
N_EXPERTS = 32
TOP_K = 4
SWIGLU_ALPHA = 1.702
SWIGLU_LIMIT = 7.0
EPS = 1e-6

LANES = 128
HALO = 32
CONV_ROWS = 16
EXPERT_ROWS = 512
VMEM_LIMIT = 56 * 1024 * 1024
LOG2E = 1.4426950408889634
NEG_BIG = -1e30
N_FPIECES = 3
DIRECT_EXP_LIMIT = 60.0
ZERO_WEIGHT_EXPONENT = -160.0
BOUND_SLACK = 1.05

F32 = jnp.float32
BF16 = jnp.bfloat16
HIGHEST = lax.Precision.HIGHEST


def _cparams(sem):
    return pltpu.CompilerParams(dimension_semantics=sem, vmem_limit_bytes=VMEM_LIMIT)


def _rms(x, g):
    return x * lax.rsqrt(jnp.mean(x * x, axis=-1, keepdims=True) + EPS) * g


def _split3(f):
    hi = f.astype(BF16).astype(F32)
    r1 = f - hi
    mid = r1.astype(BF16).astype(F32)
    lo = (r1 - mid).astype(BF16).astype(F32)
    return hi, mid, lo


def _mods_kernel(c_ref, w_ref, b_ref, o_ref):
    c = c_ref[...]
    ca = c * jax.nn.sigmoid(c)
    o_ref[0] = jnp.dot(ca, w_ref[0], precision=HIGHEST, preferred_element_type=F32) + b_ref[0]


def _mods(c8, w, b):
    n_l, d, m = w.shape
    tn = min(m, 1024)
    return pl.pallas_call(
        _mods_kernel,
        out_shape=jax.ShapeDtypeStruct((n_l, 8, m), F32),
        grid=(n_l, m // tn),
        in_specs=[
            pl.BlockSpec((8, d), lambda l, j: (0, 0)),
            pl.BlockSpec((1, d, tn), lambda l, j: (l, 0, j)),
            pl.BlockSpec((1, 1, tn), lambda l, j: (l, 0, j)),
        ],
        out_specs=pl.BlockSpec((1, 8, tn), lambda l, j: (l, 0, j)),
        compiler_params=_cparams(("arbitrary", "arbitrary")),
        name="mods",
    )(c8, w, b.reshape(n_l, 1, m))


def _conv_kernel(x_ref, mod_ref, ng_ref, w1_ref, b1_ref, wdw_ref, bdw_ref, lng_ref, lnb_ref,
                 w2_ref, b2_ref, o_ref, ubuf, cbuf):
    t, d = x_ref.shape[1], x_ref.shape[2]
    slab = ubuf.shape[1:]

    @pl.when(pl.program_id(1) == 0)
    def _():
        ubuf[pl.ds(0, HALO)] = jnp.zeros((HALO,) + slab, F32)

    x = x_ref[0]
    mod = mod_ref[0]
    h = _rms(x, ng_ref[...]) * (1.0 + mod[1:2, :]) + mod[0:1, :]
    u = jnp.dot(h.astype(BF16), w1_ref[...], preferred_element_type=F32) + b1_ref[...]
    u = u[:, :d] * jax.nn.sigmoid(u[:, d:])
    ubuf[pl.ds(HALO, t)] = u.reshape((t,) + slab)

    def conv_rows(c, carry):
        base = pl.multiple_of(c * CONV_ROWS, CONV_ROWS)
        acc = jnp.zeros((CONV_ROWS,) + slab, F32)
        for j in range(CONV_WIDTH):
            acc = acc + ubuf[pl.ds(base + (HALO - (CONV_WIDTH - 1) + j), CONV_ROWS)] * wdw_ref[j]
        cbuf[pl.ds(base, CONV_ROWS)] = acc
        return carry

    lax.fori_loop(0, t // CONV_ROWS, conv_rows, 0)
    ubuf[pl.ds(0, HALO)] = ubuf[pl.ds(t, HALO)]
    acc = cbuf[...].reshape(t, d) + bdw_ref[...]
    mu = jnp.mean(acc, axis=-1, keepdims=True)
    cen = acc - mu
    var = jnp.mean(cen * cen, axis=-1, keepdims=True)
    y = cen * lax.rsqrt(var + EPS) * lng_ref[...] + lnb_ref[...]
    y = y * jax.nn.sigmoid(y)
    y = jnp.dot(y.astype(BF16), w2_ref[...], preferred_element_type=F32) + b2_ref[...]
    o_ref[0] = x + mod[2:3, :] * y


def _conv_layer(x, mod, ng, w1, b1, wdw, bdw, lng, lnb, w2, b2, t):
    b, s, d = x.shape
    row = lambda a: a.reshape(1, -1)
    slab = (d // LANES, LANES)
    wdw_p = jnp.zeros((HALO, d), F32).at[:CONV_WIDTH].set(wdw).reshape((HALO,) + slab)
    const = lambda shape: pl.BlockSpec(shape, lambda bi, si: (0,) * len(shape))
    return pl.pallas_call(
        _conv_kernel,
        out_shape=jax.ShapeDtypeStruct((b, s, d), F32),
        grid=(b, s // t),
        in_specs=[
            pl.BlockSpec((1, t, d), lambda bi, si: (bi, si, 0)),
            pl.BlockSpec((1, 6, d), lambda bi, si: (bi, 0, 0)),
            const((1, d)), const((d, 2 * d)), const((1, 2 * d)), const((HALO,) + slab), const((1, d)),
            const((1, d)), const((1, d)), const((d, d)), const((1, d)),
        ],
        out_specs=pl.BlockSpec((1, t, d), lambda bi, si: (bi, si, 0)),
        scratch_shapes=[pltpu.VMEM((t + HALO,) + slab, F32), pltpu.VMEM((t,) + slab, F32)],
        compiler_params=_cparams(("arbitrary", "arbitrary")),
        name="conv_layer",
    )(x, mod, row(ng), w1.astype(BF16), row(b1), wdw_p, row(bdw), row(lng), row(lnb),
      w2.astype(BF16), row(b2))


def _route_kernel(x_ref, mod_ref, ng_ref, rw_ref, rb_ref, h_ref, gate_ref, meta_ref, cnt_ref):
    t = x_ref.shape[0]

    @pl.when(pl.program_id(0) == 0)
    def _():
        cnt_ref[...] = jnp.zeros(cnt_ref.shape, F32)

    mod = mod_ref[0]
    h = _rms(x_ref[...], ng_ref[...]) * (1.0 + mod[4:5, :]) + mod[3:4, :]
    h_ref[...] = h.reshape(h_ref.shape)
    logits = jnp.dot(h, rw_ref[...], precision=HIGHEST, preferred_element_type=F32) + rb_ref[...]
    lane = lax.broadcasted_iota(jnp.int32, (t, LANES), 1).astype(F32)
    work = logits
    vals, idxs = [], []
    for _ in range(TOP_K):
        m = jnp.max(work, axis=-1, keepdims=True)
        idx = jnp.min(jnp.where(work == m, lane, float(LANES)), axis=-1, keepdims=True)
        vals.append(m)
        idxs.append(idx)
        work = jnp.where(lane == idx, -jnp.inf, work)
    exps = [jnp.exp(v - vals[0]) for v in vals]
    denom = exps[0] + exps[1] + exps[2] + exps[3]
    onehot = jnp.zeros((t, LANES), F32)
    for idx in idxs:
        onehot = onehot + (lane == idx).astype(F32)
    r_i = lax.broadcasted_iota(jnp.int32, (t, t), 0)
    c_i = lax.broadcasted_iota(jnp.int32, (t, t), 1)
    tri = (c_i < r_i).astype(BF16)
    base = jnp.dot(tri, onehot.astype(BF16), preferred_element_type=F32) + cnt_ref[0:1, :]
    gate_out = jnp.zeros((t, LANES), F32)
    meta = jnp.zeros((t, LANES), F32)
    for k in range(TOP_K):
        rank = jnp.sum(jnp.where(lane == idxs[k], base, 0.0), axis=-1, keepdims=True)
        gate_out = jnp.where(lane == k, exps[k] / denom, gate_out)
        meta = jnp.where(lane == k, rank, meta)
        meta = jnp.where(lane == TOP_K + k, idxs[k], meta)
    gate_ref[...] = gate_out
    meta_ref[...] = meta.astype(jnp.int32)
    cnt_ref[...] = cnt_ref[...] + jnp.sum(onehot, axis=0, keepdims=True)


def _route(x2, mod, ng, rw, rb, t, tiles_per_batch):
    n, d = x2.shape
    rw_p = jnp.zeros((d, LANES), F32).at[:, :N_EXPERTS].set(rw)
    rb_p = jnp.full((1, LANES), NEG_BIG, F32).at[0, :N_EXPERTS].set(rb)
    return pl.pallas_call(
        _route_kernel,
        out_shape=(
            jax.ShapeDtypeStruct((n, d // LANES, LANES), F32),
            jax.ShapeDtypeStruct((n, LANES), F32),
            jax.ShapeDtypeStruct((n, LANES), jnp.int32),
            jax.ShapeDtypeStruct((8, LANES), F32),
        ),
        grid=(n // t,),
        in_specs=[
            pl.BlockSpec((t, d), lambda i: (i, 0)),
            pl.BlockSpec((1, 6, d), lambda i: (i // tiles_per_batch, 0, 0)),
            pl.BlockSpec((1, d), lambda i: (0, 0)),
            pl.BlockSpec((d, LANES), lambda i: (0, 0)),
            pl.BlockSpec((1, LANES), lambda i: (0, 0)),
        ],
        out_specs=(
            pl.BlockSpec((t, d // LANES, LANES), lambda i: (i, 0, 0)),
            pl.BlockSpec((t, LANES), lambda i: (i, 0)),
            pl.BlockSpec((t, LANES), lambda i: (i, 0)),
            pl.BlockSpec((8, LANES), lambda i: (0, 0)),
        ),
        compiler_params=_cparams(("arbitrary",)),
        name="moe_route",
    )(x2, mod, ng.reshape(1, d), rw_p, rb_p)


def _dispatch_kernel(pad0_ref, padn_ref, h_ref, dest_hbm, xs_hbm, zrow, idx_smem, sem_idx, sem_rows, sem_pad):
    t = h_ref.shape[0]
    i = pl.program_id(0)

    @pl.when(i == 0)
    def _():
        zrow[...] = jnp.zeros(zrow.shape, F32)
        for start in (True, False):
            def per_expert(e, carry, start=start):
                def per_row(r, c):
                    pad_cp = pltpu.make_async_copy(zrow, xs_hbm.at[pad0_ref[e] + r], sem_pad)
                    if start:
                        pad_cp.start()
                    else:
                        pad_cp.wait()
                    return c
                return lax.fori_loop(0, padn_ref[e], per_row, carry)
            lax.fori_loop(0, N_EXPERTS, per_expert, 0)

    cp = pltpu.make_async_copy(dest_hbm.at[i], idx_smem, sem_idx)
    cp.start()
    cp.wait()

    def row_copy(tok, dst):
        return pltpu.make_async_copy(h_ref.at[tok], xs_hbm.at[dst], sem_rows)

    def issue(tok, carry):
        for k in range(TOP_K):
            row_copy(tok, idx_smem[k * t + tok]).start(priority=k % 2)
        return carry

    lax.fori_loop(0, t, issue, 0)

    def drain(tok, carry):
        for k in range(TOP_K):
            row_copy(tok, idx_smem[k * t + tok]).wait()
        return carry

    lax.fori_loop(0, t, drain, 0)


def _dispatch(h, dest_tiles, pad_start, pad_n, p_rows, t):
    n, sub, _ = h.shape
    return pl.pallas_call(
        _dispatch_kernel,
        out_shape=jax.ShapeDtypeStruct((p_rows, sub, LANES), F32),
        grid_spec=pltpu.PrefetchScalarGridSpec(
            num_scalar_prefetch=2,
            grid=(n // t,),
            in_specs=[
                pl.BlockSpec((t, sub, LANES), lambda i, p0, pn: (i, 0, 0)),
                pl.BlockSpec(memory_space=pl.ANY),
            ],
            out_specs=pl.BlockSpec(memory_space=pl.ANY),
            scratch_shapes=[pltpu.VMEM((sub, LANES), F32), pltpu.SMEM((TOP_K * t,), jnp.int32),
                            pltpu.SemaphoreType.DMA, pltpu.SemaphoreType.DMA, pltpu.SemaphoreType.DMA],
        ),
        compiler_params=_cparams(("arbitrary",)),
        name="moe_dispatch",
    )(pad_start, pad_n, h, dest_tiles)


def _experts_kernel(be_ref, nu_ref, xs_ref, wgu_ref, bgu_ref, wdn_ref, bdn_ref, ys_ref, wgu_bf, wdn_bf):
    b = pl.program_id(0)
    f = wdn_ref.shape[2]
    e = be_ref[b]
    e_prev = be_ref[jnp.maximum(b - 1, 0)]

    @pl.when(jnp.logical_or(b == 0, e != e_prev))
    def _():
        wgu_bf[...] = wgu_ref[0, 0].astype(BF16)
        wdn_bf[...] = wdn_ref[0, 0].astype(BF16)

    @pl.when(b < nu_ref[0])
    def _():
        rows, sub, _ = xs_ref.shape
        x = xs_ref[...].reshape(rows, sub * LANES).astype(BF16)
        gu = jnp.dot(x, wgu_bf[...], preferred_element_type=F32) + bgu_ref[0, 0]
        x_glu = jnp.minimum(gu[:, :f], SWIGLU_LIMIT)
        x_lin = jnp.clip(gu[:, f:], -SWIGLU_LIMIT, SWIGLU_LIMIT)
        act = x_glu * jax.nn.sigmoid(SWIGLU_ALPHA * x_glu) * (x_lin + 1.0)
        y = jnp.dot(act.astype(BF16), wdn_bf[...], preferred_element_type=F32) + bdn_ref[0, 0]
        ys_ref[...] = y.reshape(ys_ref.shape)


def _experts(xs, block_e, n_used, layer, w_gu, b_gu, w_dn, b_dn):
    p_rows, sub, _ = xs.shape
    n_l, n_e, d, f2 = w_gu.shape
    f = f2 // 2
    n_blocks = p_rows // EXPERT_ROWS
    row_map = lambda b, be, nu: (jnp.minimum(b, nu[0] - 1), 0, 0)
    exp_map = lambda b, be, nu: (layer, be[b], 0, 0)
    return pl.pallas_call(
        _experts_kernel,
        out_shape=jax.ShapeDtypeStruct((p_rows, sub, LANES), F32),
        grid_spec=pltpu.PrefetchScalarGridSpec(
            num_scalar_prefetch=2,
            grid=(n_blocks,),
            in_specs=[
                pl.BlockSpec((EXPERT_ROWS, sub, LANES), row_map),
                pl.BlockSpec((1, 1, d, f2), exp_map),
                pl.BlockSpec((1, 1, 1, f2), exp_map),
                pl.BlockSpec((1, 1, f, d), exp_map),
                pl.BlockSpec((1, 1, 1, d), exp_map),
            ],
            out_specs=pl.BlockSpec((EXPERT_ROWS, sub, LANES), row_map),
            scratch_shapes=[pltpu.VMEM((d, f2), BF16), pltpu.VMEM((f, d), BF16)],
        ),
        compiler_params=_cparams(("arbitrary",)),
        name="moe_experts",
    )(block_e, n_used, xs, w_gu, b_gu.reshape(n_l, n_e, 1, f2), w_dn, b_dn.reshape(n_l, n_e, 1, d))


def _combine_kernel(x_ref, gate_ref, mod_ref, fng_ref, dest_hbm, ys_hbm, o_ref, buf, idx_smem, sem_idx, sem_rows,
                    *, final):
    t = x_ref.shape[0]
    i = pl.program_id(0)
    cp = pltpu.make_async_copy(dest_hbm.at[i], idx_smem, sem_idx)
    cp.start()
    cp.wait()

    def row_copy(k, tok, src):
        return pltpu.make_async_copy(ys_hbm.at[src], buf.at[k, tok], sem_rows)

    def issue(tok, carry):
        for k in range(TOP_K):
            row_copy(k, tok, idx_smem[k * t + tok]).start(priority=k % 2)
        return carry

    lax.fori_loop(0, t, issue, 0)

    def drain(tok, carry):
        for k in range(TOP_K):
            row_copy(k, tok, idx_smem[k * t + tok]).wait()
        return carry

    lax.fori_loop(0, t, drain, 0)

    gate = gate_ref[...]
    y = gate[:, 0:1] * buf[0].reshape(x_ref.shape)
    for k in range(1, TOP_K):
        y = y + gate[:, k:k + 1] * buf[k].reshape(x_ref.shape)
    out = x_ref[...] + mod_ref[0][5:6, :] * y
    o_ref[...] = _rms(out, fng_ref[...]) if final else out


def _combine(x2, gates, mod, final_g, dest_tiles, ys, t, tiles_per_batch, final):
    n, d = x2.shape
    return pl.pallas_call(
        functools.partial(_combine_kernel, final=final),
        out_shape=jax.ShapeDtypeStruct((n, d), F32),
        grid=(n // t,),
        in_specs=[
            pl.BlockSpec((t, d), lambda i: (i, 0)),
            pl.BlockSpec((t, LANES), lambda i: (i, 0)),
            pl.BlockSpec((1, 6, d), lambda i: (i // tiles_per_batch, 0, 0)),
            pl.BlockSpec((1, d), lambda i: (0, 0)),
            pl.BlockSpec(memory_space=pl.ANY),
            pl.BlockSpec(memory_space=pl.ANY),
        ],
        out_specs=pl.BlockSpec((t, d), lambda i: (i, 0)),
        scratch_shapes=[pltpu.VMEM((TOP_K, t, d // LANES, LANES), F32), pltpu.SMEM((TOP_K * t,), jnp.int32),
                        pltpu.SemaphoreType.DMA, pltpu.SemaphoreType.DMA],
        compiler_params=_cparams(("arbitrary",)),
        name="moe_combine",
    )(x2, gates, mod, final_g.reshape(1, d), dest_tiles, ys)


def _moe_layer(x, mod, ng, rw, rb, layer, w_gu, b_gu, w_dn, b_dn, final_g, final, t):
    b, s, d = x.shape
    n = b * s
    x2 = x.reshape(n, d)
    tiles_per_batch = s // t
    h, gates, meta, cnt = _route(x2, mod, ng, rw, rb, t, tiles_per_batch)
    counts = cnt[0, :N_EXPERTS].astype(jnp.int32)
    padded = (counts + EXPERT_ROWS - 1) // EXPERT_ROWS * EXPERT_ROWS
    pend = jnp.cumsum(padded)
    pstart = pend - padded
    n_blocks = -(-(n * TOP_K) // EXPERT_ROWS) + N_EXPERTS
    p_rows = n_blocks * EXPERT_ROWS
    rank = meta[:, :TOP_K]
    eidx = meta[:, TOP_K:2 * TOP_K]
    experts = jnp.arange(N_EXPERTS, dtype=jnp.int32)
    dest = rank + jnp.sum(jnp.where(eidx[..., None] == experts, pstart, 0), axis=-1)
    dest_tiles = dest.reshape(n // t, t, TOP_K).transpose(0, 2, 1).reshape(n // t, TOP_K * t)
    block_row0 = jnp.arange(n_blocks, dtype=jnp.int32) * EXPERT_ROWS
    block_e = jnp.minimum(jnp.sum((pend[None, :] <= block_row0[:, None]).astype(jnp.int32), axis=1),
                          N_EXPERTS - 1)
    n_used = (pend[-1:] // EXPERT_ROWS).astype(jnp.int32)
    xs = _dispatch(h, dest_tiles, pstart + counts, padded - counts, p_rows, t)
    ys = _experts(xs, block_e, n_used, layer, w_gu, b_gu, w_dn, b_dn)
    out = _combine(x2, gates, mod, final_g, dest_tiles, ys, t, tiles_per_batch, final)
    return out.reshape(b, s, d)


def _kv_kernel(x_ref, mod_ref, ng_ref, wkv_ref, wf_ref, bf_ref, kg_ref, hsum_ref,
               kt_ref, v_ref, qf_ref, fb_ref, carry):
    t, d = x_ref.shape[1], x_ref.shape[2]
    n_pairs = N_HEADS // 2

    @pl.when(pl.program_id(1) == 0)
    def _():
        carry[...] = jnp.zeros(carry.shape, F32)

    mod = mod_ref[0]
    h = _rms(x_ref[0], ng_ref[...]) * (1.0 + mod[1:2, :]) + mod[0:1, :]
    h_hi = h.astype(BF16)
    kv = jnp.dot(h_hi, wkv_ref[...], preferred_element_type=F32)
    k = kv[:, :d]
    v_ref[0] = kv[:, d:].astype(BF16)
    ms = jnp.dot((k * k).astype(BF16), hsum_ref[...], preferred_element_type=F32)
    k = k * lax.rsqrt(ms + EPS) * kg_ref[...]
    kt = k.T
    for hp in range(n_pairs):
        kt_ref[0, hp, 0, pl.ds(0, LANES), :] = kt[hp * LANES:(hp + 1) * LANES, :].astype(BF16)

    h_lo = (h - h_hi.astype(F32)).astype(BF16)
    fz2 = jnp.dot(h_hi, wf_ref[...], preferred_element_type=F32)
    fz = (fz2[:, :LANES] + fz2[:, LANES:] + bf_ref[...]
          + jnp.dot(h_lo, wf_ref[:, pl.ds(0, LANES)], preferred_element_type=F32))
    ls = jax.nn.log_sigmoid(fz)
    r_i = lax.broadcasted_iota(jnp.int32, (t, t), 0)
    c_i = lax.broadcasted_iota(jnp.int32, (t, t), 1)
    tri = (c_i <= r_i).astype(BF16)
    ls_hi, ls_mid, ls_lo = _split3(ls)
    cum2 = jnp.dot(tri, jnp.concatenate([ls_hi, ls_mid], axis=1).astype(BF16), preferred_element_type=F32)
    cum = (cum2[:, :LANES] + cum2[:, LANES:] + carry[0:1, :]
           + jnp.dot(tri, ls_lo.astype(BF16), preferred_element_type=F32))
    carry[...] = jnp.broadcast_to(cum[t - 1:t, :], carry.shape)
    f2 = cum * LOG2E
    row8 = lax.broadcasted_iota(jnp.int32, (8, LANES), 0)
    fb_ref[0, 0] = jnp.where(row8 == 0, f2[0:1, :], jnp.where(row8 == 1, f2[t - 1:t, :], 0.0))
    f2t = f2.T
    lane = lax.broadcasted_iota(jnp.int32, (t, LANES), 1)
    sub = lax.broadcasted_iota(jnp.int32, (LANES, t), 0)
    q_pieces = _split3(f2)
    k_pieces = _split3(-f2t)
    for hp in range(n_pairs):
        q_aug = jnp.zeros((t, LANES), F32)
        k_aug = jnp.zeros((LANES, t), F32)
        for hh in range(2):
            head = 2 * hp + hh
            o = hh * 2 * N_FPIECES
            for p in range(N_FPIECES):
                q_aug = jnp.where(lane == o + p, q_pieces[p][:, head:head + 1], q_aug)
                q_aug = jnp.where(lane == o + N_FPIECES + p, 1.0, q_aug)
                k_aug = jnp.where(sub == o + p, 1.0, k_aug)
                k_aug = jnp.where(sub == o + N_FPIECES + p, k_pieces[p][head:head + 1, :], k_aug)
        qf_ref[0, hp] = q_aug.astype(BF16)
        kt_ref[0, hp, 0, pl.ds(LANES, LANES), :] = k_aug.astype(BF16)


def _shared_kv(x, kvmod, ng, w_kvf, b_f, k_norm_g, t):
    b, s, d = x.shape
    n_pairs = N_HEADS // 2
    wkv = w_kvf[:, :2 * d].astype(BF16)
    wf = jnp.zeros((d, LANES), F32).at[:, :N_HEADS].set(w_kvf[:, 2 * d:])
    wf_hi = wf.astype(BF16)
    wf2 = jnp.concatenate([wf_hi, (wf - wf_hi.astype(F32)).astype(BF16)], axis=1)
    bf = jnp.zeros((1, LANES), F32).at[0, :N_HEADS].set(b_f)
    kg = jnp.tile(k_norm_g, N_HEADS).reshape(1, d)
    head_of = jnp.arange(d) // HEAD_DIM
    hsum = ((head_of[:, None] == head_of[None, :]).astype(F32) / HEAD_DIM).astype(BF16)
    const = lambda shape: pl.BlockSpec(shape, lambda bi, si: (0,) * len(shape))
    return pl.pallas_call(
        _kv_kernel,
        out_shape=(
            jax.ShapeDtypeStruct((b, n_pairs, s // t, 2 * LANES, t), BF16),
            jax.ShapeDtypeStruct((b, s, d), BF16),
            jax.ShapeDtypeStruct((b, n_pairs, s, LANES), BF16),
            jax.ShapeDtypeStruct((b, s // t, 8, LANES), F32),
        ),
        grid=(b, s // t),
        in_specs=[
            pl.BlockSpec((1, t, d), lambda bi, si: (bi, si, 0)),
            pl.BlockSpec((1, 2, d), lambda bi, si: (bi, 0, 0)),
            const((1, d)), const((d, 2 * d)), const((d, 2 * LANES)), const((1, LANES)), const((1, d)),
            const((d, d)),
        ],
        out_specs=(
            pl.BlockSpec((1, n_pairs, 1, 2 * LANES, t), lambda bi, si: (bi, 0, si, 0, 0)),
            pl.BlockSpec((1, t, d), lambda bi, si: (bi, si, 0)),
            pl.BlockSpec((1, n_pairs, t, LANES), lambda bi, si: (bi, 0, si, 0)),
            pl.BlockSpec((1, 1, 8, LANES), lambda bi, si: (bi, si, 0, 0)),
        ),
        scratch_shapes=[pltpu.VMEM((8, LANES), F32)],
        compiler_params=_cparams(("arbitrary", "arbitrary")),
        name="shared_kv",
    )(x, kvmod, ng.reshape(1, d), wkv, wf2, bf, kg, hsum), hsum


def _qg_kernel(x_ref, mod_ref, ng_ref, w_ref, qg_ref, hsum_ref, q_ref, g_ref):
    d = x_ref.shape[2]
    mod = mod_ref[0]
    h = _rms(x_ref[0], ng_ref[...]) * (1.0 + mod[1:2, :]) + mod[0:1, :]
    qg = jnp.dot(h.astype(BF16), w_ref[...], preferred_element_type=F32)
    q = qg[:, :d]
    ms = jnp.dot((q * q).astype(BF16), hsum_ref[...], preferred_element_type=F32)
    q = q * lax.rsqrt(ms + EPS) * qg_ref[...] * (LOG2E / math.sqrt(HEAD_DIM))
    q_ref[0] = q.astype(BF16)
    g_ref[0] = jax.nn.sigmoid(qg[:, d:]).astype(BF16)


def _qg(x, mod, ng, w_qg, q_norm_g, hsum, t):
    b, s, d = x.shape
    const = lambda shape: pl.BlockSpec(shape, lambda bi, si: (0,) * len(shape))
    tile = pl.BlockSpec((1, t, d), lambda bi, si: (bi, si, 0))
    return pl.pallas_call(
        _qg_kernel,
        out_shape=(jax.ShapeDtypeStruct((b, s, d), BF16), jax.ShapeDtypeStruct((b, s, d), BF16)),
        grid=(b, s // t),
        in_specs=[tile, pl.BlockSpec((1, 6, d), lambda bi, si: (bi, 0, 0)),
                  const((1, d)), const((d, 2 * d)), const((1, d)), const((d, d))],
        out_specs=(tile, tile),
        compiler_params=_cparams(("arbitrary", "arbitrary")),
        name="attn_qg",
    )(x, mod, ng.reshape(1, d), w_qg.astype(BF16), jnp.tile(q_norm_g, N_HEADS).reshape(1, d), hsum)


def _attn_kernel(j0_ref, q_ref, qf_ref, kt_ref, v_ref, o_ref, qa_scr, m_scr, l_scr, acc_scr, *, online):
    tq = q_ref.shape[1]
    tk = kt_ref.shape[4]
    i = pl.program_id(2)
    n_q = pl.num_programs(2)
    head0 = pl.program_id(0) * N_HEADS + 2 * pl.program_id(1)
    first = [j0_ref[(head0 + hh) * n_q + i] for hh in range(2)]
    first_both = jnp.maximum(first[0], first[1])
    lane = lax.broadcasted_iota(jnp.int32, (tq, LANES), 1)
    q2 = q_ref[0]
    qf = qf_ref[0, 0]
    zero = jnp.zeros((), BF16)
    n_aug = 2 * N_FPIECES
    qa_scr[0, :, pl.ds(0, LANES)] = jnp.where(lane < HEAD_DIM, q2, zero)
    qa_scr[0, :, pl.ds(LANES, LANES)] = jnp.where(lane < n_aug, qf, zero)
    qa_scr[1, :, pl.ds(0, LANES)] = jnp.where(lane >= HEAD_DIM, q2, zero)
    qa_scr[1, :, pl.ds(LANES, LANES)] = jnp.where(jnp.logical_and(lane >= n_aug, lane < 2 * n_aug), qf, zero)
    if online:
        m_scr[...] = jnp.full(m_scr.shape, NEG_BIG, F32)
    l_scr[...] = jnp.zeros(l_scr.shape, F32)
    acc_scr[...] = jnp.zeros(acc_scr.shape, F32)

    def scores(hh, j, masked):
        s = jnp.dot(qa_scr[hh], kt_ref[0, 0, j], preferred_element_type=F32)
        if masked:
            r_i = lax.broadcasted_iota(jnp.int32, (tq, tk), 0)
            c_i = lax.broadcasted_iota(jnp.int32, (tq, tk), 1)
            s = jnp.where(c_i <= r_i, s, NEG_BIG)
        return s

    def values(j):
        return v_ref[0, pl.ds(pl.multiple_of(j * tk, tk), tk), :]

    def tiles(js, masked, heads):
        for hh in heads:
            if online:
                for j in js:
                    s = scores(hh, j, masked)
                    m_prev = m_scr[hh]
                    m_new = jnp.maximum(m_prev, jnp.max(s, axis=-1, keepdims=True))
                    alpha = jnp.exp2(m_prev - m_new)
                    p = jnp.exp2(s - m_new[:, 0:1])
                    l_scr[hh] = alpha * l_scr[hh] + jnp.sum(p, axis=-1, keepdims=True)
                    acc_scr[hh] = alpha * acc_scr[hh] + jnp.dot(p.astype(BF16), values(j),
                                                                preferred_element_type=F32)
                    m_scr[hh] = m_new
            else:
                part, pv = l_scr[hh], acc_scr[hh]
                for j in js:
                    p = jnp.exp2(scores(hh, j, masked))
                    for c in range(tk // LANES):
                        part = part + p[:, c * LANES:(c + 1) * LANES]
                    pv = pv + jnp.dot(p.astype(BF16), values(j), preferred_element_type=F32)
                l_scr[hh], acc_scr[hh] = part, pv

    def one_tile(heads):
        def body(j, carry):
            tiles((j,), False, heads)
            return carry
        return body

    def two_tiles(pair, carry):
        j = first_both + 2 * pair
        tiles((j, j + 1), False, (0, 1))
        return carry

    lax.fori_loop(first[0], first_both, one_tile((0,)), 0)
    lax.fori_loop(first[1], first_both, one_tile((1,)), 0)
    n_both = i - first_both
    lax.fori_loop(0, n_both // 2, two_tiles, 0)

    @pl.when(n_both % 2 == 1)
    def _():
        tiles((i - 1,), False, (0, 1))

    tiles((i,), True, (0, 1))
    if online:
        l0, l1 = l_scr[0], l_scr[1]
    else:
        l0 = jnp.sum(l_scr[0], axis=-1, keepdims=True)
        l1 = jnp.sum(l_scr[1], axis=-1, keepdims=True)
    o_ref[0] = jnp.where(lane < HEAD_DIM, acc_scr[0] / l0, acc_scr[1] / l1).astype(BF16)


def _attention(j0, q, qf, kt, v, t, online):
    b, s, d = q.shape
    n_pairs = N_HEADS // 2
    nkv = s // t
    return pl.pallas_call(
        functools.partial(_attn_kernel, online=online),
        out_shape=jax.ShapeDtypeStruct((b, s, d), BF16),
        grid_spec=pltpu.PrefetchScalarGridSpec(
            num_scalar_prefetch=1,
            grid=(b, n_pairs, s // t),
            in_specs=[
                pl.BlockSpec((1, t, LANES), lambda bi, hp, i, j0r: (bi, i, hp)),
                pl.BlockSpec((1, 1, t, LANES), lambda bi, hp, i, j0r: (bi, hp, i, 0)),
                pl.BlockSpec((1, 1, nkv, 2 * LANES, t), lambda bi, hp, i, j0r: (bi, hp, 0, 0, 0)),
                pl.BlockSpec((1, s, LANES), lambda bi, hp, i, j0r: (bi, 0, hp)),
            ],
            out_specs=pl.BlockSpec((1, t, LANES), lambda bi, hp, i, j0r: (bi, i, hp)),
            scratch_shapes=[
                pltpu.VMEM((2, t, 2 * LANES), BF16),
                pltpu.VMEM((2, t, LANES), F32),
                pltpu.VMEM((2, t, LANES), F32),
                pltpu.VMEM((2, t, LANES), F32),
            ],
        ),
        compiler_params=_cparams(("arbitrary", "arbitrary", "arbitrary")),
        name="fox_attention_online" if online else "fox_attention",
    )(j0, q, qf, kt, v)


def _attn_out_kernel(x_ref, o_ref, g_ref, mod_ref, w_ref, out_ref):
    og = o_ref[0] * g_ref[0]
    y = jnp.dot(og, w_ref[...], preferred_element_type=F32)
    out_ref[0] = x_ref[0] + mod_ref[0][2:3, :] * y


def _attn_out(x, o, g, mod, w_o, t):
    b, s, d = x.shape
    tile = pl.BlockSpec((1, t, d), lambda bi, si: (bi, si, 0))
    return pl.pallas_call(
        _attn_out_kernel,
        out_shape=jax.ShapeDtypeStruct((b, s, d), F32),
        grid=(b, s // t),
        in_specs=[tile, tile, tile, pl.BlockSpec((1, 6, d), lambda bi, si: (bi, 0, 0)),
                  pl.BlockSpec((d, d), lambda bi, si: (0, 0))],
        out_specs=tile,
        compiler_params=_cparams(("arbitrary", "arbitrary")),
        name="attn_out",
    )(x, o, g, mod, w_o.astype(BF16))


def _fox_layer(x, mod, ng, w_qg, q_norm_g, k_norm_g, w_o, kv, t):
    (kt, v, qf, fb), hsum = kv
    n_t = x.shape[1] // t
    q, g = _qg(x, mod, ng, w_qg, q_norm_g, hsum, t)
    bound = (HEAD_DIM * jnp.max(jnp.abs(q_norm_g)) * jnp.max(jnp.abs(k_norm_g))
             * (LOG2E / math.sqrt(HEAD_DIM)))
    f_first, f_last = fb[:, :, 0, :N_HEADS], fb[:, :, 1, :N_HEADS]
    best = (bound * BOUND_SLACK + f_first[:, :, None, :]) - f_last[:, None, :, :]
    before = jnp.arange(n_t)[None, :] < jnp.arange(n_t)[:, None]
    dead = jnp.logical_and(best < ZERO_WEIGHT_EXPONENT, before[None, :, :, None])
    j0 = jnp.sum(dead.astype(jnp.int32), axis=2).transpose(0, 2, 1).reshape(-1)
    o = lax.cond(bound <= DIRECT_EXP_LIMIT,
                 functools.partial(_attention, t=t, online=False),
                 lambda j0_, *rest: _attention(jnp.zeros_like(j0_), *rest, t=t, online=True),
                 j0, q, qf, kt, v)
    return _attn_out(x, o, g, mod, w_o, t)


def kernel(x, c, mod_w, mod_b, norm1_g, norm2_g, conv_w_pw1, conv_b_pw1, conv_w_dw, conv_b_dw, conv_ln_g, conv_ln_b, conv_w_pw2, conv_b_pw2, kv_mod_w, kv_mod_b, kv_norm_g, w_kvf, b_f, k_norm_g, attn_w_qg, q_norm_g, attn_w_o, moe_router_w, moe_router_b, moe_w_gu, moe_b_gu, moe_w_down, moe_b_down, final_norm_g):
    b, s, d = x.shape
    depth = mod_w.shape[0]
    n_a = conv_w_pw1.shape[0]
    t = min(512, s)
    c8 = jnp.zeros((8, d), F32).at[:b].set(c)
    mods = _mods(c8, mod_w, mod_b)[:, :b].reshape(depth, b, 6, d)
    kvmod = _mods(c8, kv_mod_w[None], kv_mod_b[None])[0, :b].reshape(b, 2, d)
    kv = None
    for l in range(depth):
        if l < n_a:
            x = _conv_layer(x, mods[l], norm1_g[l], conv_w_pw1[l], conv_b_pw1[l], conv_w_dw[l],
                            conv_b_dw[l], conv_ln_g[l], conv_ln_b[l], conv_w_pw2[l], conv_b_pw2[l], t)
        else:
            lb = l - n_a
            x = _fox_layer(x, mods[l], norm1_g[l], attn_w_qg[lb], q_norm_g[lb], k_norm_g, attn_w_o[lb], kv, t)
        x = _moe_layer(x, mods[l], norm2_g[l], moe_router_w[l], moe_router_b[l], l, moe_w_gu,
                       moe_b_gu, moe_w_down, moe_b_down, final_norm_g, l == depth - 1, t)
        if l == n_a - 1:
            kv = _shared_kv(x, kvmod, kv_norm_g, w_kvf, b_f, k_norm_g, t)
    return x
```

```python
import functools
import math

import jax
import jax.numpy as jnp
from jax import lax
from jax.experimental import pallas as pl
from jax.experimental.pallas import tpu as pltpu

N_HEADS = 16
HEAD_DIM = 64
CONV_WIDTH = 31
N_EXPERTS = 32
TOP_K = 4
SWIGLU_ALPHA = 1.702
SWIGLU_LIMIT = 7.0
EPS = 1e-6

LANES = 128
HALO = 32
CONV_ROWS = 16
ISSUE_UNROLL = 4
KV_UNROLL = 4
EXPERT_ROWS = 512
VMEM_LIMIT = 56 * 1024 * 1024
LOG2E = 1.4426950408889634
NEG_BIG = -1e30
N_FPIECES = 3
DIRECT_EXP_LIMIT = 60.0
ZERO_WEIGHT_EXPONENT = -160.0
BOUND_SLACK = 1.05

F32 = jnp.float32
BF16 = jnp.bfloat16
HIGHEST = lax.Precision.HIGHEST


def _cparams(sem):
    return pltpu.CompilerParams(dimension_semantics=sem, vmem_limit_bytes=VMEM_LIMIT)


def _rms(x, g):
    return x * lax.rsqrt(jnp.mean(x * x, axis=-1, keepdims=True) + EPS) * g


def _split3(f):
    hi = f.astype(BF16).astype(F32)
    r1 = f - hi
    mid = r1.astype(BF16).astype(F32)
    lo = (r1 - mid).astype(BF16).astype(F32)
    return hi, mid, lo


def _mods_kernel(c_ref, w_ref, b_ref, o_ref):
    c = c_ref[...]
    ca = c * jax.nn.sigmoid(c)
    o_ref[0] = jnp.dot(ca, w_ref[0], precision=HIGHEST, preferred_element_type=F32) + b_ref[0]


def _mods(c8, w, b):
    n_l, d, m = w.shape
    tn = min(m, 1024)
    return pl.pallas_call(
        _mods_kernel,
        out_shape=jax.ShapeDtypeStruct((n_l, 8, m), F32),
        grid=(n_l, m // tn),
        in_specs=[
            pl.BlockSpec((8, d), lambda l, j: (0, 0)),
            pl.BlockSpec((1, d, tn), lambda l, j: (l, 0, j)),
            pl.BlockSpec((1, 1, tn), lambda l, j: (l, 0, j)),
        ],
        out_specs=pl.BlockSpec((1, 8, tn), lambda l, j: (l, 0, j)),
        compiler_params=_cparams(("arbitrary", "arbitrary")),
        name="mods",
    )(c8, w, b.reshape(n_l, 1, m))


def _conv_kernel(x_ref, mod_ref, ng_ref, w1_ref, b1_ref, wdw_ref, bdw_ref, lng_ref, lnb_ref,
                 w2_ref, b2_ref, o_ref, ubuf, cbuf):
    t, d = x_ref.shape[1], x_ref.shape[2]
    slab = ubuf.shape[1:]

    @pl.when(pl.program_id(1) == 0)
    def _():
        ubuf[pl.ds(0, HALO)] = jnp.zeros((HALO,) + slab, F32)

    x = x_ref[0]
    mod = mod_ref[0]
    h = _rms(x, ng_ref[...]) * (1.0 + mod[1:2, :]) + mod[0:1, :]
    u = jnp.dot(h.astype(BF16), w1_ref[...], preferred_element_type=F32) + b1_ref[...]
    u = u[:, :d] * jax.nn.sigmoid(u[:, d:])
    ubuf[pl.ds(HALO, t)] = u.reshape((t,) + slab)

    def conv_rows(c, carry):
        base = pl.multiple_of(c * CONV_ROWS, CONV_ROWS)
        acc = jnp.zeros((CONV_ROWS,) + slab, F32)
        for j in range(CONV_WIDTH):
            acc = acc + ubuf[pl.ds(base + (HALO - (CONV_WIDTH - 1) + j), CONV_ROWS)] * wdw_ref[j]
        cbuf[pl.ds(base, CONV_ROWS)] = acc
        return carry

    lax.fori_loop(0, t // CONV_ROWS, conv_rows, 0)
    ubuf[pl.ds(0, HALO)] = ubuf[pl.ds(t, HALO)]
    acc = cbuf[...].reshape(t, d) + bdw_ref[...]
    mu = jnp.mean(acc, axis=-1, keepdims=True)
    cen = acc - mu
    var = jnp.mean(cen * cen, axis=-1, keepdims=True)
    y = cen * lax.rsqrt(var + EPS) * lng_ref[...] + lnb_ref[...]
    y = y * jax.nn.sigmoid(y)
    y = jnp.dot(y.astype(BF16), w2_ref[...], preferred_element_type=F32) + b2_ref[...]
    o_ref[0] = x + mod[2:3, :] * y


def _conv_layer(x, mod, ng, w1, b1, wdw, bdw, lng, lnb, w2, b2, t):
    b, s, d = x.shape
    row = lambda a: a.reshape(1, -1)
    slab = (d // LANES, LANES)
    wdw_p = jnp.zeros((HALO, d), F32).at[:CONV_WIDTH].set(wdw).reshape((HALO,) + slab)
    const = lambda shape: pl.BlockSpec(shape, lambda bi, si: (0,) * len(shape))
    return pl.pallas_call(
        _conv_kernel,
        out_shape=jax.ShapeDtypeStruct((b, s, d), F32),
        grid=(b, s // t),
        in_specs=[
            pl.BlockSpec((1, t, d), lambda bi, si: (bi, si, 0)),
            pl.BlockSpec((1, 6, d), lambda bi, si: (bi, 0, 0)),
            const((1, d)), const((d, 2 * d)), const((1, 2 * d)), const((HALO,) + slab), const((1, d)),
            const((1, d)), const((1, d)), const((d, d)), const((1, d)),
        ],
        out_specs=pl.BlockSpec((1, t, d), lambda bi, si: (bi, si, 0)),
        scratch_shapes=[pltpu.VMEM((t + HALO,) + slab, F32), pltpu.VMEM((t,) + slab, F32)],
        compiler_params=_cparams(("arbitrary", "arbitrary")),
        name="conv_layer",
    )(x, mod, row(ng), w1.astype(BF16), row(b1), wdw_p, row(bdw), row(lng), row(lnb),
      w2.astype(BF16), row(b2))


def _route_kernel(x_ref, mod_ref, ng_ref, rw_ref, rb_ref, h_ref, gate_ref, meta_ref, cnt_ref):
    t = x_ref.shape[0]

    @pl.when(pl.program_id(0) == 0)
    def _():
        cnt_ref[...] = jnp.zeros(cnt_ref.shape, F32)

    mod = mod_ref[0]
    h = _rms(x_ref[...], ng_ref[...]) * (1.0 + mod[4:5, :]) + mod[3:4, :]
    h_ref[...] = h.reshape(h_ref.shape)
    logits = jnp.dot(h, rw_ref[...], precision=HIGHEST, preferred_element_type=F32) + rb_ref[...]
    lane = lax.broadcasted_iota(jnp.int32, (t, LANES), 1).astype(F32)
    work = logits
    vals, idxs = [], []
    for _ in range(TOP_K):
        m = jnp.max(work, axis=-1, keepdims=True)
        idx = jnp.min(jnp.where(work == m, lane, float(LANES)), axis=-1, keepdims=True)
        vals.append(m)
        idxs.append(idx)
        work = jnp.where(lane == idx, -jnp.inf, work)
    exps = [jnp.exp(v - vals[0]) for v in vals]
    denom = exps[0] + exps[1] + exps[2] + exps[3]
    onehot = jnp.zeros((t, LANES), F32)
    for idx in idxs:
        onehot = onehot + (lane == idx).astype(F32)
    r_i = lax.broadcasted_iota(jnp.int32, (t, t), 0)
    c_i = lax.broadcasted_iota(jnp.int32, (t, t), 1)
    tri = (c_i < r_i).astype(BF16)
    base = jnp.dot(tri, onehot.astype(BF16), preferred_element_type=F32) + cnt_ref[0:1, :]
    gate_out = jnp.zeros((t, LANES), F32)
    meta = jnp.zeros((t, LANES), F32)
    for k in range(TOP_K):
        rank = jnp.sum(jnp.where(lane == idxs[k], base, 0.0), axis=-1, keepdims=True)
        gate_out = jnp.where(lane == k, exps[k] / denom, gate_out)
        meta = jnp.where(lane == k, rank, meta)
        meta = jnp.where(lane == TOP_K + k, idxs[k], meta)
    gate_ref[...] = gate_out
    meta_ref[...] = meta.astype(jnp.int32)
    cnt_ref[...] = cnt_ref[...] + jnp.sum(onehot, axis=0, keepdims=True)


def _route(x2, mod, ng, rw, rb, t, tiles_per_batch):
    n, d = x2.shape
    rw_p = jnp.zeros((d, LANES), F32).at[:, :N_EXPERTS].set(rw)
    rb_p = jnp.full((1, LANES), NEG_BIG, F32).at[0, :N_EXPERTS].set(rb)
    return pl.pallas_call(
        _route_kernel,
        out_shape=(
            jax.ShapeDtypeStruct((n, d // LANES, LANES), F32),
            jax.ShapeDtypeStruct((n, LANES), F32),
            jax.ShapeDtypeStruct((n, LANES), jnp.int32),
            jax.ShapeDtypeStruct((8, LANES), F32),
        ),
        grid=(n // t,),
        in_specs=[
            pl.BlockSpec((t, d), lambda i: (i, 0)),
            pl.BlockSpec((1, 6, d), lambda i: (i // tiles_per_batch, 0, 0)),
            pl.BlockSpec((1, d), lambda i: (0, 0)),
            pl.BlockSpec((d, LANES), lambda i: (0, 0)),
            pl.BlockSpec((1, LANES), lambda i: (0, 0)),
        ],
        out_specs=(
            pl.BlockSpec((t, d // LANES, LANES), lambda i: (i, 0, 0)),
            pl.BlockSpec((t, LANES), lambda i: (i, 0)),
            pl.BlockSpec((t, LANES), lambda i: (i, 0)),
            pl.BlockSpec((8, LANES), lambda i: (0, 0)),
        ),
        compiler_params=_cparams(("arbitrary",)),
        name="moe_route",
    )(x2, mod, ng.reshape(1, d), rw_p, rb_p)


def _dispatch_kernel(pad0_ref, padn_ref, h_ref, dest_hbm, xs_hbm, zrow, idx_smem, sem_idx, sem_rows, sem_pad):
    t = h_ref.shape[0]
    i = pl.program_id(0)

    @pl.when(i == 0)
    def _():
        zrow[...] = jnp.zeros(zrow.shape, F32)
        for start in (True, False):
            def per_expert(e, carry, start=start):
                def per_row(r, c):
                    pad_cp = pltpu.make_async_copy(zrow, xs_hbm.at[pad0_ref[e] + r], sem_pad)
                    if start:
                        pad_cp.start()
                    else:
                        pad_cp.wait()
                    return c
                return lax.fori_loop(0, padn_ref[e], per_row, carry)
            lax.fori_loop(0, N_EXPERTS, per_expert, 0)

    cp = pltpu.make_async_copy(dest_hbm.at[i], idx_smem, sem_idx)
    cp.start()
    cp.wait()

    def row_copy(tok, dst):
        return pltpu.make_async_copy(h_ref.at[tok], xs_hbm.at[dst], sem_rows)

    def issue(group, carry):
        tok0 = pl.multiple_of(group * ISSUE_UNROLL, ISSUE_UNROLL)
        for u in range(ISSUE_UNROLL):
            for k in range(TOP_K):
                row_copy(tok0 + u, idx_smem[k * t + tok0 + u]).start(priority=(u * TOP_K + k) % 2)
        return carry

    lax.fori_loop(0, t // ISSUE_UNROLL, issue, 0)

    def drain(tok, carry):
        for k in range(TOP_K):
            row_copy(tok, idx_smem[k * t + tok]).wait()
        return carry

    lax.fori_loop(0, t, drain, 0)


def _dispatch(h, dest_tiles, pad_start, pad_n, p_rows, t):
    n, sub, _ = h.shape
    return pl.pallas_call(
        _dispatch_kernel,
        out_shape=jax.ShapeDtypeStruct((p_rows, sub, LANES), F32),
        grid_spec=pltpu.PrefetchScalarGridSpec(
            num_scalar_prefetch=2,
            grid=(n // t,),
            in_specs=[
                pl.BlockSpec((t, sub, LANES), lambda i, p0, pn: (i, 0, 0)),
                pl.BlockSpec(memory_space=pl.ANY),
            ],
            out_specs=pl.BlockSpec(memory_space=pl.ANY),
            scratch_shapes=[pltpu.VMEM((sub, LANES), F32), pltpu.SMEM((TOP_K * t,), jnp.int32),
                            pltpu.SemaphoreType.DMA, pltpu.SemaphoreType.DMA, pltpu.SemaphoreType.DMA],
        ),
        compiler_params=_cparams(("arbitrary",)),
        name="moe_dispatch",
    )(pad_start, pad_n, h, dest_tiles)


def _experts_kernel(be_ref, nu_ref, xs_ref, wgu_ref, bgu_ref, wdn_ref, bdn_ref, ys_ref, wgu_bf, wdn_bf):
    b = pl.program_id(0)
    f = wdn_ref.shape[2]
    e = be_ref[b]
    e_prev = be_ref[jnp.maximum(b - 1, 0)]

    @pl.when(jnp.logical_or(b == 0, e != e_prev))
    def _():
        wgu_bf[...] = wgu_ref[0, 0].astype(BF16)
        wdn_bf[...] = wdn_ref[0, 0].astype(BF16)

    @pl.when(b < nu_ref[0])
    def _():
        rows, sub, _ = xs_ref.shape
        x = xs_ref[...].reshape(rows, sub * LANES).astype(BF16)
        gu = jnp.dot(x, wgu_bf[...], preferred_element_type=F32) + bgu_ref[0, 0]
        x_glu = jnp.minimum(gu[:, :f], SWIGLU_LIMIT)
        x_lin = jnp.clip(gu[:, f:], -SWIGLU_LIMIT, SWIGLU_LIMIT)
        act = x_glu * jax.nn.sigmoid(SWIGLU_ALPHA * x_glu) * (x_lin + 1.0)
        y = jnp.dot(act.astype(BF16), wdn_bf[...], preferred_element_type=F32) + bdn_ref[0, 0]
        ys_ref[...] = y.reshape(ys_ref.shape)


def _experts(xs, block_e, n_used, layer, w_gu, b_gu, w_dn, b_dn):
    p_rows, sub, _ = xs.shape
    n_l, n_e, d, f2 = w_gu.shape
    f = f2 // 2
    n_blocks = p_rows // EXPERT_ROWS
    row_map = lambda b, be, nu: (jnp.minimum(b, nu[0] - 1), 0, 0)
    exp_map = lambda b, be, nu: (layer, be[b], 0, 0)
    return pl.pallas_call(
        _experts_kernel,
        out_shape=jax.ShapeDtypeStruct((p_rows, sub, LANES), F32),
        grid_spec=pltpu.PrefetchScalarGridSpec(
            num_scalar_prefetch=2,
            grid=(n_blocks,),
            in_specs=[
                pl.BlockSpec((EXPERT_ROWS, sub, LANES), row_map),
                pl.BlockSpec((1, 1, d, f2), exp_map),
                pl.BlockSpec((1, 1, 1, f2), exp_map),
                pl.BlockSpec((1, 1, f, d), exp_map),
                pl.BlockSpec((1, 1, 1, d), exp_map),
            ],
            out_specs=pl.BlockSpec((EXPERT_ROWS, sub, LANES), row_map),
            scratch_shapes=[pltpu.VMEM((d, f2), BF16), pltpu.VMEM((f, d), BF16)],
        ),
        compiler_params=_cparams(("arbitrary",)),
        name="moe_experts",
    )(block_e, n_used, xs, w_gu, b_gu.reshape(n_l, n_e, 1, f2), w_dn, b_dn.reshape(n_l, n_e, 1, d))


def _combine_kernel(x_ref, gate_ref, mod_ref, fng_ref, dest_hbm, ys_hbm, o_ref, buf, idx_smem, sem_idx, sem_rows,
                    *, final):
    t = x_ref.shape[0]
    i = pl.program_id(0)
    n_steps = pl.num_programs(0)
    slot = i % 2

    def idx_copy(step, s):
        return pltpu.make_async_copy(dest_hbm.at[step], idx_smem.at[pl.ds(s * TOP_K * t, TOP_K * t)],
                                     sem_idx.at[s])

    def start_rows(s):
        def issue(group, carry):
            tok0 = pl.multiple_of(group * ISSUE_UNROLL, ISSUE_UNROLL)
            for u in range(ISSUE_UNROLL):
                for k in range(TOP_K):
                    pltpu.make_async_copy(ys_hbm.at[idx_smem[(s * TOP_K + k) * t + tok0 + u]],
                                          buf.at[s * TOP_K + k, tok0 + u],
                                          sem_rows.at[s]).start(priority=(u * TOP_K + k) % 2)
            return carry

        lax.fori_loop(0, t // ISSUE_UNROLL, issue, 0)

    @pl.when(i == 0)
    def _():
        first = idx_copy(0, 0)
        first.start()
        first.wait()
        start_rows(0)

    for nxt in range(2):
        @pl.when(jnp.logical_and(i + 1 < n_steps, slot == 1 - nxt))
        def _(nxt=nxt):
            idx_copy(i + 1, nxt).start()

    for k in range(TOP_K):
        pltpu.make_async_copy(ys_hbm.at[pl.ds(0, t)], buf.at[slot * TOP_K + k], sem_rows.at[slot]).wait()

    for nxt in range(2):
        @pl.when(jnp.logical_and(i + 1 < n_steps, slot == 1 - nxt))
        def _(nxt=nxt):
            idx_copy(i + 1, nxt).wait()
            start_rows(nxt)

    gate = gate_ref[...]
    y = gate[:, 0:1] * buf[slot * TOP_K].reshape(x_ref.shape)
    for k in range(1, TOP_K):
        y = y + gate[:, k:k + 1] * buf[slot * TOP_K + k].reshape(x_ref.shape)
    out = x_ref[...] + mod_ref[0][5:6, :] * y
    o_ref[...] = _rms(out, fng_ref[...]) if final else out


def _combine(x2, gates, mod, final_g, dest_tiles, ys, t, tiles_per_batch, final):
    n, d = x2.shape
    return pl.pallas_call(
        functools.partial(_combine_kernel, final=final),
        out_shape=jax.ShapeDtypeStruct((n, d), F32),
        grid=(n // t,),
        in_specs=[
            pl.BlockSpec((t, d), lambda i: (i, 0)),
            pl.BlockSpec((t, LANES), lambda i: (i, 0)),
            pl.BlockSpec((1, 6, d), lambda i: (i // tiles_per_batch, 0, 0)),
            pl.BlockSpec((1, d), lambda i: (0, 0)),
            pl.BlockSpec(memory_space=pl.ANY),
            pl.BlockSpec(memory_space=pl.ANY),
        ],
        out_specs=pl.BlockSpec((t, d), lambda i: (i, 0)),
        scratch_shapes=[pltpu.VMEM((2 * TOP_K, t, d // LANES, LANES), F32), pltpu.SMEM((2 * TOP_K * t,), jnp.int32),
                        pltpu.SemaphoreType.DMA((2,)), pltpu.SemaphoreType.DMA((2,))],
        compiler_params=_cparams(("arbitrary",)),
        name="moe_combine",
    )(x2, gates, mod, final_g.reshape(1, d), dest_tiles, ys)


def _moe_layer(x, mod, ng, rw, rb, layer, w_gu, b_gu, w_dn, b_dn, final_g, final, t):
    b, s, d = x.shape
    n = b * s
    x2 = x.reshape(n, d)
    tiles_per_batch = s // t
    h, gates, meta, cnt = _route(x2, mod, ng, rw, rb, t, tiles_per_batch)
    counts = cnt[0, :N_EXPERTS].astype(jnp.int32)
    padded = (counts + EXPERT_ROWS - 1) // EXPERT_ROWS * EXPERT_ROWS
    pend = jnp.cumsum(padded)
    pstart = pend - padded
    n_blocks = -(-(n * TOP_K) // EXPERT_ROWS) + N_EXPERTS
    p_rows = n_blocks * EXPERT_ROWS
    rank = meta[:, :TOP_K]
    eidx = meta[:, TOP_K:2 * TOP_K]
    experts = jnp.arange(N_EXPERTS, dtype=jnp.int32)
    dest = rank + jnp.sum(jnp.where(eidx[..., None] == experts, pstart, 0), axis=-1)
    dest_tiles = dest.reshape(n // t, t, TOP_K).transpose(0, 2, 1).reshape(n // t, TOP_K * t)
    block_row0 = jnp.arange(n_blocks, dtype=jnp.int32) * EXPERT_ROWS
    block_e = jnp.minimum(jnp.sum((pend[None, :] <= block_row0[:, None]).astype(jnp.int32), axis=1),
                          N_EXPERTS - 1)
    n_used = (pend[-1:] // EXPERT_ROWS).astype(jnp.int32)
    xs = _dispatch(h, dest_tiles, pstart + counts, padded - counts, p_rows, t)
    ys = _experts(xs, block_e, n_used, layer, w_gu, b_gu, w_dn, b_dn)
    out = _combine(x2, gates, mod, final_g, dest_tiles, ys, t, tiles_per_batch, final)
    return out.reshape(b, s, d)


def _kv_kernel(x_ref, mod_ref, ng_ref, wkv_ref, wf_ref, bf_ref, kg_ref, hsum_ref,
               kt_ref, v_ref, qf_ref, fb_ref, carry):
    t, d = x_ref.shape[1], x_ref.shape[2]
    n_pairs = N_HEADS // 2

    @pl.when(pl.program_id(1) == 0)
    def _():
        carry[...] = jnp.zeros(carry.shape, F32)

    mod = mod_ref[0]
    h = _rms(x_ref[0], ng_ref[...]) * (1.0 + mod[1:2, :]) + mod[0:1, :]
    h_hi = h.astype(BF16)
    kv = jnp.dot(h_hi, wkv_ref[...], preferred_element_type=F32)
    k = kv[:, :d]
    v_ref[0] = kv[:, d:].astype(BF16)
    ms = jnp.dot((k * k).astype(BF16), hsum_ref[...], preferred_element_type=F32)
    k = k * lax.rsqrt(ms + EPS) * kg_ref[...]
    kt = k.T
    for hp in range(n_pairs):
        kt_ref[0, hp, 0, pl.ds(0, LANES), :] = kt[hp * LANES:(hp + 1) * LANES, :].astype(BF16)

    h_lo = (h - h_hi.astype(F32)).astype(BF16)
    fz2 = jnp.dot(h_hi, wf_ref[...], preferred_element_type=F32)
    fz = (fz2[:, :LANES] + fz2[:, LANES:] + bf_ref[...]
          + jnp.dot(h_lo, wf_ref[:, pl.ds(0, LANES)], preferred_element_type=F32))
    ls = jax.nn.log_sigmoid(fz)
    r_i = lax.broadcasted_iota(jnp.int32, (t, t), 0)
    c_i = lax.broadcasted_iota(jnp.int32, (t, t), 1)
    tri = (c_i <= r_i).astype(BF16)
    ls_hi, ls_mid, ls_lo = _split3(ls)
    cum2 = jnp.dot(tri, jnp.concatenate([ls_hi, ls_mid], axis=1).astype(BF16), preferred_element_type=F32)
    cum = (cum2[:, :LANES] + cum2[:, LANES:] + carry[0:1, :]
           + jnp.dot(tri, ls_lo.astype(BF16), preferred_element_type=F32))
    carry[...] = jnp.broadcast_to(cum[t - 1:t, :], carry.shape)
    f2 = cum * LOG2E
    row8 = lax.broadcasted_iota(jnp.int32, (8, LANES), 0)
    fb_ref[0, 0] = jnp.where(row8 == 0, f2[0:1, :], jnp.where(row8 == 1, f2[t - 1:t, :], 0.0))
    f2t = f2.T
    lane = lax.broadcasted_iota(jnp.int32, (t, LANES), 1)
    sub = lax.broadcasted_iota(jnp.int32, (LANES, t), 0)
    q_pieces = _split3(f2)
    k_pieces = _split3(-f2t)
    for hp in range(n_pairs):
        q_aug = jnp.zeros((t, LANES), F32)
        k_aug = jnp.zeros((LANES, t), F32)
        for hh in range(2):
            head = 2 * hp + hh
            o = hh * 2 * N_FPIECES
            for p in range(N_FPIECES):
                q_aug = jnp.where(lane == o + p, q_pieces[p][:, head:head + 1], q_aug)
                q_aug = jnp.where(lane == o + N_FPIECES + p, 1.0, q_aug)
                k_aug = jnp.where(sub == o + p, 1.0, k_aug)
                k_aug = jnp.where(sub == o + N_FPIECES + p, k_pieces[p][head:head + 1, :], k_aug)
        qf_ref[0, hp] = q_aug.astype(BF16)
        kt_ref[0, hp, 0, pl.ds(LANES, LANES), :] = k_aug.astype(BF16)


def _shared_kv(x, kvmod, ng, w_kvf, b_f, k_norm_g, t):
    b, s, d = x.shape
    n_pairs = N_HEADS // 2
    wkv = w_kvf[:, :2 * d].astype(BF16)
    wf = jnp.zeros((d, LANES), F32).at[:, :N_HEADS].set(w_kvf[:, 2 * d:])
    wf_hi = wf.astype(BF16)
    wf2 = jnp.concatenate([wf_hi, (wf - wf_hi.astype(F32)).astype(BF16)], axis=1)
    bf = jnp.zeros((1, LANES), F32).at[0, :N_HEADS].set(b_f)
    kg = jnp.tile(k_norm_g, N_HEADS).reshape(1, d)
    head_of = jnp.arange(d) // HEAD_DIM
    hsum = ((head_of[:, None] == head_of[None, :]).astype(F32) / HEAD_DIM).astype(BF16)
    const = lambda shape: pl.BlockSpec(shape, lambda bi, si: (0,) * len(shape))
    return pl.pallas_call(
        _kv_kernel,
        out_shape=(
            jax.ShapeDtypeStruct((b, n_pairs, s // t, 2 * LANES, t), BF16),
            jax.ShapeDtypeStruct((b, s, d), BF16),
            jax.ShapeDtypeStruct((b, n_pairs, s, LANES), BF16),
            jax.ShapeDtypeStruct((b, s // t, 8, LANES), F32),
        ),
        grid=(b, s // t),
        in_specs=[
            pl.BlockSpec((1, t, d), lambda bi, si: (bi, si, 0)),
            pl.BlockSpec((1, 2, d), lambda bi, si: (bi, 0, 0)),
            const((1, d)), const((d, 2 * d)), const((d, 2 * LANES)), const((1, LANES)), const((1, d)),
            const((d, d)),
        ],
        out_specs=(
            pl.BlockSpec((1, n_pairs, 1, 2 * LANES, t), lambda bi, si: (bi, 0, si, 0, 0)),
            pl.BlockSpec((1, t, d), lambda bi, si: (bi, si, 0)),
            pl.BlockSpec((1, n_pairs, t, LANES), lambda bi, si: (bi, 0, si, 0)),
            pl.BlockSpec((1, 1, 8, LANES), lambda bi, si: (bi, si, 0, 0)),
        ),
        scratch_shapes=[pltpu.VMEM((8, LANES), F32)],
        compiler_params=_cparams(("arbitrary", "arbitrary")),
        name="shared_kv",
    )(x, kvmod, ng.reshape(1, d), wkv, wf2, bf, kg, hsum), hsum


def _qg_kernel(x_ref, mod_ref, ng_ref, w_ref, qg_ref, hsum_ref, q_ref, g_ref):
    d = x_ref.shape[2]
    mod = mod_ref[0]
    h = _rms(x_ref[0], ng_ref[...]) * (1.0 + mod[1:2, :]) + mod[0:1, :]
    qg = jnp.dot(h.astype(BF16), w_ref[...], preferred_element_type=F32)
    q = qg[:, :d]
    ms = jnp.dot((q * q).astype(BF16), hsum_ref[...], preferred_element_type=F32)
    q = q * lax.rsqrt(ms + EPS) * qg_ref[...] * (LOG2E / math.sqrt(HEAD_DIM))
    q_ref[0] = q.astype(BF16)
    g_ref[0] = jax.nn.sigmoid(qg[:, d:]).astype(BF16)


def _qg(x, mod, ng, w_qg, q_norm_g, hsum, t):
    b, s, d = x.shape
    const = lambda shape: pl.BlockSpec(shape, lambda bi, si: (0,) * len(shape))
    tile = pl.BlockSpec((1, t, d), lambda bi, si: (bi, si, 0))
    return pl.pallas_call(
        _qg_kernel,
        out_shape=(jax.ShapeDtypeStruct((b, s, d), BF16), jax.ShapeDtypeStruct((b, s, d), BF16)),
        grid=(b, s // t),
        in_specs=[tile, pl.BlockSpec((1, 6, d), lambda bi, si: (bi, 0, 0)),
                  const((1, d)), const((d, 2 * d)), const((1, d)), const((d, d))],
        out_specs=(tile, tile),
        compiler_params=_cparams(("arbitrary", "arbitrary")),
        name="attn_qg",
    )(x, mod, ng.reshape(1, d), w_qg.astype(BF16), jnp.tile(q_norm_g, N_HEADS).reshape(1, d), hsum)


def _attn_kernel(j0_ref, q_ref, qf_ref, kt_ref, v_ref, o_ref, qa_scr, m_scr, l_scr, acc_scr, *, online):
    tq = q_ref.shape[1]
    tk = kt_ref.shape[4]
    i = pl.program_id(2)
    n_q = pl.num_programs(2)
    head0 = pl.program_id(0) * N_HEADS + 2 * pl.program_id(1)
    first = [j0_ref[(head0 + hh) * n_q + i] for hh in range(2)]
    first_both = jnp.maximum(first[0], first[1])
    lane = lax.broadcasted_iota(jnp.int32, (tq, LANES), 1)
    q2 = q_ref[0]
    qf = qf_ref[0, 0]
    zero = jnp.zeros((), BF16)
    n_aug = 2 * N_FPIECES
    qa_scr[0, :, pl.ds(0, LANES)] = jnp.where(lane < HEAD_DIM, q2, zero)
    qa_scr[0, :, pl.ds(LANES, LANES)] = jnp.where(lane < n_aug, qf, zero)
    qa_scr[1, :, pl.ds(0, LANES)] = jnp.where(lane >= HEAD_DIM, q2, zero)
    qa_scr[1, :, pl.ds(LANES, LANES)] = jnp.where(jnp.logical_and(lane >= n_aug, lane < 2 * n_aug), qf, zero)
    if online:
        m_scr[...] = jnp.full(m_scr.shape, NEG_BIG, F32)
    l_scr[...] = jnp.zeros(l_scr.shape, F32)
    acc_scr[...] = jnp.zeros(acc_scr.shape, F32)

    def scores(hh, j, masked):
        s = jnp.dot(qa_scr[hh], kt_ref[0, 0, j], preferred_element_type=F32)
        if masked:
            r_i = lax.broadcasted_iota(jnp.int32, (tq, tk), 0)
            c_i = lax.broadcasted_iota(jnp.int32, (tq, tk), 1)
            s = jnp.where(c_i <= r_i, s, NEG_BIG)
        return s

    def values(j):
        return v_ref[0, pl.ds(pl.multiple_of(j * tk, tk), tk), :]

    def tiles(js, masked, heads):
        for hh in heads:
            if online:
                for j in js:
                    s = scores(hh, j, masked)
                    m_prev = m_scr[hh]
                    m_new = jnp.maximum(m_prev, jnp.max(s, axis=-1, keepdims=True))
                    alpha = jnp.exp2(m_prev - m_new)
                    p = jnp.exp2(s - m_new[:, 0:1])
                    l_scr[hh] = alpha * l_scr[hh] + jnp.sum(p, axis=-1, keepdims=True)
                    acc_scr[hh] = alpha * acc_scr[hh] + jnp.dot(p.astype(BF16), values(j),
                                                                preferred_element_type=F32)
                    m_scr[hh] = m_new
            else:
                part, pv = l_scr[hh], acc_scr[hh]
                for j in js:
                    p = jnp.exp2(scores(hh, j, masked))
                    for c in range(tk // LANES):
                        part = part + p[:, c * LANES:(c + 1) * LANES]
                    pv = pv + jnp.dot(p.astype(BF16), values(j), preferred_element_type=F32)
                l_scr[hh], acc_scr[hh] = part, pv

    def one_tile(heads):
        def body(j, carry):
            tiles((j,), False, heads)
            return carry
        return body

    def tile_group(group, carry):
        j = first_both + KV_UNROLL * group
        tiles(tuple(j + u for u in range(KV_UNROLL)), False, (0, 1))
        return carry

    lax.fori_loop(first[0], first_both, one_tile((0,)), 0)
    lax.fori_loop(first[1], first_both, one_tile((1,)), 0)
    n_groups = (i - first_both) // KV_UNROLL
    lax.fori_loop(0, n_groups, tile_group, 0)
    lax.fori_loop(first_both + KV_UNROLL * n_groups, i, one_tile((0, 1)), 0)
    tiles((i,), True, (0, 1))
    if online:
        l0, l1 = l_scr[0], l_scr[1]
    else:
        l0 = jnp.sum(l_scr[0], axis=-1, keepdims=True)
        l1 = jnp.sum(l_scr[1], axis=-1, keepdims=True)
    o_ref[0] = jnp.where(lane < HEAD_DIM, acc_scr[0] / l0, acc_scr[1] / l1).astype(BF16)


def _attention(j0, q, qf, kt, v, t, online):
    b, s, d = q.shape
    n_pairs = N_HEADS // 2
    nkv = s // t
    return pl.pallas_call(
        functools.partial(_attn_kernel, online=online),
        out_shape=jax.ShapeDtypeStruct((b, s, d), BF16),
        grid_spec=pltpu.PrefetchScalarGridSpec(
            num_scalar_prefetch=1,
            grid=(b, n_pairs, s // t),
            in_specs=[
                pl.BlockSpec((1, t, LANES), lambda bi, hp, i, j0r: (bi, i, hp)),
                pl.BlockSpec((1, 1, t, LANES), lambda bi, hp, i, j0r: (bi, hp, i, 0)),
                pl.BlockSpec((1, 1, nkv, 2 * LANES, t), lambda bi, hp, i, j0r: (bi, hp, 0, 0, 0)),
                pl.BlockSpec((1, s, LANES), lambda bi, hp, i, j0r: (bi, 0, hp)),
            ],
            out_specs=pl.BlockSpec((1, t, LANES), lambda bi, hp, i, j0r: (bi, i, hp)),
            scratch_shapes=[
                pltpu.VMEM((2, t, 2 * LANES), BF16),
                pltpu.VMEM((2, t, LANES), F32),
                pltpu.VMEM((2, t, LANES), F32),
                pltpu.VMEM((2, t, LANES), F32),
            ],
        ),
        compiler_params=_cparams(("arbitrary", "arbitrary", "arbitrary")),
        name="fox_attention_online" if online else "fox_attention",
    )(j0, q, qf, kt, v)


def _attn_out_kernel(x_ref, o_ref, g_ref, mod_ref, w_ref, out_ref):
    og = o_ref[0] * g_ref[0]
    y = jnp.dot(og, w_ref[...], preferred_element_type=F32)
    out_ref[0] = x_ref[0] + mod_ref[0][2:3, :] * y


def _attn_out(x, o, g, mod, w_o, t):
    b, s, d = x.shape
    tile = pl.BlockSpec((1, t, d), lambda bi, si: (bi, si, 0))
    return pl.pallas_call(
        _attn_out_kernel,
        out_shape=jax.ShapeDtypeStruct((b, s, d), F32),
        grid=(b, s // t),
        in_specs=[tile, tile, tile, pl.BlockSpec((1, 6, d), lambda bi, si: (bi, 0, 0)),
                  pl.BlockSpec((d, d), lambda bi, si: (0, 0))],
        out_specs=tile,
        compiler_params=_cparams(("arbitrary", "arbitrary")),
        name="attn_out",
    )(x, o, g, mod, w_o.astype(BF16))


def _fox_layer(x, mod, ng, w_qg, q_norm_g, k_norm_g, w_o, kv, t):
    (kt, v, qf, fb), hsum = kv
    n_t = x.shape[1] // t
    q, g = _qg(x, mod, ng, w_qg, q_norm_g, hsum, t)
    bound = (HEAD_DIM * jnp.max(jnp.abs(q_norm_g)) * jnp.max(jnp.abs(k_norm_g))
             * (LOG2E / math.sqrt(HEAD_DIM)))
    f_first, f_last = fb[:, :, 0, :N_HEADS], fb[:, :, 1, :N_HEADS]
    best = (bound * BOUND_SLACK + f_first[:, :, None, :]) - f_last[:, None, :, :]
    before = jnp.arange(n_t)[None, :] < jnp.arange(n_t)[:, None]
    dead = jnp.logical_and(best < ZERO_WEIGHT_EXPONENT, before[None, :, :, None])
    j0 = jnp.sum(dead.astype(jnp.int32), axis=2).transpose(0, 2, 1).reshape(-1)
    o = lax.cond(bound <= DIRECT_EXP_LIMIT,
                 functools.partial(_attention, t=t, online=False),
                 lambda j0_, *rest: _attention(jnp.zeros_like(j0_), *rest, t=t, online=True),
                 j0, q, qf, kt, v)
    return _attn_out(x, o, g, mod, w_o, t)


def kernel(x, c, mod_w, mod_b, norm1_g, norm2_g, conv_w_pw1, conv_b_pw1, conv_w_dw, conv_b_dw, conv_ln_g, conv_ln_b, conv_w_pw2, conv_b_pw2, kv_mod_w, kv_mod_b, kv_norm_g, w_kvf, b_f, k_norm_g, attn_w_qg, q_norm_g, attn_w_o, moe_router_w, moe_router_b, moe_w_gu, moe_b_gu, moe_w_down, moe_b_down, final_norm_g):
    b, s, d = x.shape
    depth = mod_w.shape[0]
    n_a = conv_w_pw1.shape[0]
    t = min(512, s)
    c8 = jnp.zeros((8, d), F32).at[:b].set(c)
    mods = _mods(c8, mod_w, mod_b)[:, :b].reshape(depth, b, 6, d)
    kvmod = _mods(c8, kv_mod_w[None], kv_mod_b[None])[0, :b].reshape(b, 2, d)
    kv = None
    for l in range(depth):
        if l < n_a:
            x = _conv_layer(x, mods[l], norm1_g[l], conv_w_pw1[l], conv_b_pw1[l], conv_w_dw[l],
                            conv_b_dw[l], conv_ln_g[l], conv_ln_b[l], conv_w_pw2[l], conv_b_pw2[l], t)
        else:
            lb = l - n_a
            x = _fox_layer(x, mods[l], norm1_g[l], attn_w_qg[lb], q_norm_g[lb], k_norm_g, attn_w_o[lb], kv, t)
        x = _moe_layer(x, mods[l], norm2_g[l], moe_router_w[l], moe_router_b[l], l, moe_w_gu,
                       moe_b_gu, moe_w_down, moe_b_down, final_norm_g, l == depth - 1, t)
        if l == n_a - 1:
            kv = _shared_kv(x, kvmod, kv_norm_g, w_kvf, b_f, k_norm_g, t)
    return x
```

```python
import functools
import math

import jax
import jax.numpy as jnp
from jax import lax
from jax.experimental import pallas as pl
from jax.experimental.pallas import tpu as pltpu

N_HEADS = 16
HEAD_DIM = 64
CONV_WIDTH = 31
N_EXPERTS = 32
TOP_K = 4
SWIGLU_ALPHA = 1.702
SWIGLU_LIMIT = 7.0
EPS = 1e-6

LANES = 128
HALO = 32
CONV_ROWS = 16
ISSUE_UNROLL = 4
KV_UNROLL = 4
EXPERT_ROWS = 512
VMEM_LIMIT = 56 * 1024 * 1024
LOG2E = 1.4426950408889634
NEG_BIG = -1e30
N_FPIECES = 3
DIRECT_EXP_LIMIT = 60.0
ZERO_WEIGHT_EXPONENT = -160.0
BOUND_SLACK = 1.05

F32 = jnp.float32
BF16 = jnp.bfloat16
HIGHEST = lax.Precision.HIGHEST


def _cparams(sem):
    return pltpu.CompilerParams(dimension_semantics=sem, vmem_limit_bytes=VMEM_LIMIT)


def _rms(x, g):
    return x * lax.rsqrt(jnp.mean(x * x, axis=-1, keepdims=True) + EPS) * g


def _split3(f):
    hi = f.astype(BF16).astype(F32)
    r1 = f - hi
    mid = r1.astype(BF16).astype(F32)
    lo = (r1 - mid).astype(BF16).astype(F32)
    return hi, mid, lo


def _mods_kernel(c_ref, w_ref, b_ref, o_ref):
    c = c_ref[...]
    ca = c * jax.nn.sigmoid(c)
    o_ref[0] = jnp.dot(ca, w_ref[0], precision=HIGHEST, preferred_element_type=F32) + b_ref[0]


def _mods(c8, w, b):
    n_l, d, m = w.shape
    tn = min(m, 1024)
    return pl.pallas_call(
        _mods_kernel,
        out_shape=jax.ShapeDtypeStruct((n_l, 8, m), F32),
        grid=(n_l, m // tn),
        in_specs=[
            pl.BlockSpec((8, d), lambda l, j: (0, 0)),
            pl.BlockSpec((1, d, tn), lambda l, j: (l, 0, j)),
            pl.BlockSpec((1, 1, tn), lambda l, j: (l, 0, j)),
        ],
        out_specs=pl.BlockSpec((1, 8, tn), lambda l, j: (l, 0, j)),
        compiler_params=_cparams(("arbitrary", "arbitrary")),
        name="mods",
    )(c8, w, b.reshape(n_l, 1, m))


def _conv_kernel(x_ref, mod_ref, ng_ref, w1_ref, b1_ref, wdw_ref, bdw_ref, lng_ref, lnb_ref,
                 w2_ref, b2_ref, o_ref, ubuf, cbuf):
    t, d = x_ref.shape[1], x_ref.shape[2]
    slab = ubuf.shape[1:]

    @pl.when(pl.program_id(1) == 0)
    def _():
        ubuf[pl.ds(0, HALO)] = jnp.zeros((HALO,) + slab, F32)

    x = x_ref[0]
    mod = mod_ref[0]
    h = _rms(x, ng_ref[...]) * (1.0 + mod[1:2, :]) + mod[0:1, :]
    u = jnp.dot(h.astype(BF16), w1_ref[...], preferred_element_type=F32) + b1_ref[...]
    u = u[:, :d] * jax.nn.sigmoid(u[:, d:])
    ubuf[pl.ds(HALO, t)] = u.reshape((t,) + slab)

    def conv_rows(c, carry):
        base = pl.multiple_of(c * CONV_ROWS, CONV_ROWS)
        acc = jnp.zeros((CONV_ROWS,) + slab, F32)
        for j in range(CONV_WIDTH):
            acc = acc + ubuf[pl.ds(base + (HALO - (CONV_WIDTH - 1) + j), CONV_ROWS)] * wdw_ref[j]
        cbuf[pl.ds(base, CONV_ROWS)] = acc
        return carry

    lax.fori_loop(0, t // CONV_ROWS, conv_rows, 0)
    ubuf[pl.ds(0, HALO)] = ubuf[pl.ds(t, HALO)]
    acc = cbuf[...].reshape(t, d) + bdw_ref[...]
    mu = jnp.mean(acc, axis=-1, keepdims=True)
    cen = acc - mu
    var = jnp.mean(cen * cen, axis=-1, keepdims=True)
    y = cen * lax.rsqrt(var + EPS) * lng_ref[...] + lnb_ref[...]
    y = y * jax.nn.sigmoid(y)
    y = jnp.dot(y.astype(BF16), w2_ref[...], preferred_element_type=F32) + b2_ref[...]
    o_ref[0] = x + mod[2:3, :] * y


def _conv_layer(x, mod, ng, w1, b1, wdw, bdw, lng, lnb, w2, b2, t):
    b, s, d = x.shape
    row = lambda a: a.reshape(1, -1)
    slab = (d // LANES, LANES)
    wdw_p = jnp.zeros((HALO, d), F32).at[:CONV_WIDTH].set(wdw).reshape((HALO,) + slab)
    const = lambda shape: pl.BlockSpec(shape, lambda bi, si: (0,) * len(shape))
    return pl.pallas_call(
        _conv_kernel,
        out_shape=jax.ShapeDtypeStruct((b, s, d), F32),
        grid=(b, s // t),
        in_specs=[
            pl.BlockSpec((1, t, d), lambda bi, si: (bi, si, 0)),
            pl.BlockSpec((1, 6, d), lambda bi, si: (bi, 0, 0)),
            const((1, d)), const((d, 2 * d)), const((1, 2 * d)), const((HALO,) + slab), const((1, d)),
            const((1, d)), const((1, d)), const((d, d)), const((1, d)),
        ],
        out_specs=pl.BlockSpec((1, t, d), lambda bi, si: (bi, si, 0)),
        scratch_shapes=[pltpu.VMEM((t + HALO,) + slab, F32), pltpu.VMEM((t,) + slab, F32)],
        compiler_params=_cparams(("arbitrary", "arbitrary")),
        name="conv_layer",
    )(x, mod, row(ng), w1.astype(BF16), row(b1), wdw_p, row(bdw), row(lng), row(lnb),
      w2.astype(BF16), row(b2))


def _route_kernel(x_ref, mod_ref, ng_ref, rw_ref, rb_ref, h_ref, gate_ref, meta_ref, cnt_ref):
    t = x_ref.shape[0]

    @pl.when(pl.program_id(0) == 0)
    def _():
        cnt_ref[...] = jnp.zeros(cnt_ref.shape, F32)

    mod = mod_ref[0]
    h = _rms(x_ref[...], ng_ref[...]) * (1.0 + mod[4:5, :]) + mod[3:4, :]
    h_ref[...] = h.reshape(h_ref.shape)
    logits = jnp.dot(h, rw_ref[...], precision=HIGHEST, preferred_element_type=F32) + rb_ref[...]
    lane = lax.broadcasted_iota(jnp.int32, (t, LANES), 1).astype(F32)
    work = logits
    vals, idxs = [], []
    for _ in range(TOP_K):
        m = jnp.max(work, axis=-1, keepdims=True)
        idx = jnp.min(jnp.where(work == m, lane, float(LANES)), axis=-1, keepdims=True)
        vals.append(m)
        idxs.append(idx)
        work = jnp.where(lane == idx, -jnp.inf, work)
    exps = [jnp.exp(v - vals[0]) for v in vals]
    denom = exps[0] + exps[1] + exps[2] + exps[3]
    onehot = jnp.zeros((t, LANES), F32)
    for idx in idxs:
        onehot = onehot + (lane == idx).astype(F32)
    r_i = lax.broadcasted_iota(jnp.int32, (t, t), 0)
    c_i = lax.broadcasted_iota(jnp.int32, (t, t), 1)
    tri = (c_i < r_i).astype(BF16)
    base = jnp.dot(tri, onehot.astype(BF16), preferred_element_type=F32) + cnt_ref[0:1, :]
    gate_out = jnp.zeros((t, LANES), F32)
    meta = jnp.zeros((t, LANES), F32)
    for k in range(TOP_K):
        rank = jnp.sum(jnp.where(lane == idxs[k], base, 0.0), axis=-1, keepdims=True)
        gate_out = jnp.where(lane == k, exps[k] / denom, gate_out)
        meta = jnp.where(lane == k, rank, meta)
        meta = jnp.where(lane == TOP_K + k, idxs[k], meta)
    gate_ref[...] = gate_out
    meta_ref[...] = meta.astype(jnp.int32)
    cnt_ref[...] = cnt_ref[...] + jnp.sum(onehot, axis=0, keepdims=True)


def _route(x2, mod, ng, rw, rb, t, tiles_per_batch):
    n, d = x2.shape
    rw_p = jnp.zeros((d, LANES), F32).at[:, :N_EXPERTS].set(rw)
    rb_p = jnp.full((1, LANES), NEG_BIG, F32).at[0, :N_EXPERTS].set(rb)
    return pl.pallas_call(
        _route_kernel,
        out_shape=(
            jax.ShapeDtypeStruct((n, d // LANES, LANES), F32),
            jax.ShapeDtypeStruct((n, LANES), F32),
            jax.ShapeDtypeStruct((n, LANES), jnp.int32),
            jax.ShapeDtypeStruct((8, LANES), F32),
        ),
        grid=(n // t,),
        in_specs=[
            pl.BlockSpec((t, d), lambda i: (i, 0)),
            pl.BlockSpec((1, 6, d), lambda i: (i // tiles_per_batch, 0, 0)),
            pl.BlockSpec((1, d), lambda i: (0, 0)),
            pl.BlockSpec((d, LANES), lambda i: (0, 0)),
            pl.BlockSpec((1, LANES), lambda i: (0, 0)),
        ],
        out_specs=(
            pl.BlockSpec((t, d // LANES, LANES), lambda i: (i, 0, 0)),
            pl.BlockSpec((t, LANES), lambda i: (i, 0)),
            pl.BlockSpec((t, LANES), lambda i: (i, 0)),
            pl.BlockSpec((8, LANES), lambda i: (0, 0)),
        ),
        compiler_params=_cparams(("arbitrary",)),
        name="moe_route",
    )(x2, mod, ng.reshape(1, d), rw_p, rb_p)


def _dispatch_kernel(pad0_ref, padn_ref, h_ref, dest_hbm, xs_hbm, zrow, idx_smem, sem_idx, sem_rows, sem_pad):
    t = h_ref.shape[0]
    i = pl.program_id(0)

    @pl.when(i == 0)
    def _():
        zrow[...] = jnp.zeros(zrow.shape, F32)
        for start in (True, False):
            def per_expert(e, carry, start=start):
                def per_row(r, c):
                    pad_cp = pltpu.make_async_copy(zrow, xs_hbm.at[pad0_ref[e] + r], sem_pad)
                    if start:
                        pad_cp.start()
                    else:
                        pad_cp.wait()
                    return c
                return lax.fori_loop(0, padn_ref[e], per_row, carry)
            lax.fori_loop(0, N_EXPERTS, per_expert, 0)

    n_steps = pl.num_programs(0)
    n_idx = TOP_K * t

    def idx_copy(step, s):
        return pltpu.make_async_copy(dest_hbm.at[step], idx_smem.at[pl.ds(s * n_idx, n_idx)], sem_idx.at[s])

    @pl.when(i == 0)
    def _():
        idx_copy(0, 0).start()

    def scatter_rows(s):
        idx_copy(i, s).wait()

        @pl.when(i + 1 < n_steps)
        def _():
            idx_copy(i + 1, 1 - s).start()

        def issue(group, carry):
            tok0 = pl.multiple_of(group * ISSUE_UNROLL, ISSUE_UNROLL)
            for u in range(ISSUE_UNROLL):
                for k in range(TOP_K):
                    pltpu.make_async_copy(h_ref.at[tok0 + u], xs_hbm.at[idx_smem[s * n_idx + k * t + tok0 + u]],
                                          sem_rows).start(priority=(u * TOP_K + k) % 2)
            return carry

        lax.fori_loop(0, t // ISSUE_UNROLL, issue, 0)

    for s in range(2):
        @pl.when(i % 2 == s)
        def _(s=s):
            scatter_rows(s)

    for k in range(TOP_K):
        pltpu.make_async_copy(h_ref, xs_hbm.at[pl.ds(0, t)], sem_rows).wait()


def _dispatch(h, dest_tiles, pad_start, pad_n, p_rows, t):
    n, sub, _ = h.shape
    return pl.pallas_call(
        _dispatch_kernel,
        out_shape=jax.ShapeDtypeStruct((p_rows, sub, LANES), F32),
        grid_spec=pltpu.PrefetchScalarGridSpec(
            num_scalar_prefetch=2,
            grid=(n // t,),
            in_specs=[
                pl.BlockSpec((t, sub, LANES), lambda i, p0, pn: (i, 0, 0)),
                pl.BlockSpec(memory_space=pl.ANY),
            ],
            out_specs=pl.BlockSpec(memory_space=pl.ANY),
            scratch_shapes=[pltpu.VMEM((sub, LANES), F32), pltpu.SMEM((2 * TOP_K * t,), jnp.int32),
                            pltpu.SemaphoreType.DMA((2,)), pltpu.SemaphoreType.DMA, pltpu.SemaphoreType.DMA],
        ),
        compiler_params=_cparams(("arbitrary",)),
        name="moe_dispatch",
    )(pad_start, pad_n, h, dest_tiles)


def _experts_kernel(be_ref, nu_ref, xs_ref, wgu_ref, bgu_ref, wdn_ref, bdn_ref, ys_ref, wgu_bf, wdn_bf):
    b = pl.program_id(0)
    f = wdn_ref.shape[2]
    e = be_ref[b]
    e_prev = be_ref[jnp.maximum(b - 1, 0)]

    @pl.when(jnp.logical_or(b == 0, e != e_prev))
    def _():
        wgu_bf[...] = wgu_ref[0, 0].astype(BF16)
        wdn_bf[...] = wdn_ref[0, 0].astype(BF16)

    @pl.when(b < nu_ref[0])
    def _():
        rows, sub, _ = xs_ref.shape
        x = xs_ref[...].reshape(rows, sub * LANES).astype(BF16)
        gu = jnp.dot(x, wgu_bf[...], preferred_element_type=F32) + bgu_ref[0, 0]
        x_glu = jnp.minimum(gu[:, :f], SWIGLU_LIMIT)
        x_lin = jnp.clip(gu[:, f:], -SWIGLU_LIMIT, SWIGLU_LIMIT)
        act = x_glu * jax.nn.sigmoid(SWIGLU_ALPHA * x_glu) * (x_lin + 1.0)
        y = jnp.dot(act.astype(BF16), wdn_bf[...], preferred_element_type=F32) + bdn_ref[0, 0]
        ys_ref[...] = y.reshape(ys_ref.shape)


def _experts(xs, block_e, n_used, layer, w_gu, b_gu, w_dn, b_dn):
    p_rows, sub, _ = xs.shape
    n_l, n_e, d, f2 = w_gu.shape
    f = f2 // 2
    n_blocks = p_rows // EXPERT_ROWS
    row_map = lambda b, be, nu: (jnp.minimum(b, nu[0] - 1), 0, 0)
    exp_map = lambda b, be, nu: (layer, be[b], 0, 0)
    return pl.pallas_call(
        _experts_kernel,
        out_shape=jax.ShapeDtypeStruct((p_rows, sub, LANES), F32),
        grid_spec=pltpu.PrefetchScalarGridSpec(
            num_scalar_prefetch=2,
            grid=(n_blocks,),
            in_specs=[
                pl.BlockSpec((EXPERT_ROWS, sub, LANES), row_map),
                pl.BlockSpec((1, 1, d, f2), exp_map),
                pl.BlockSpec((1, 1, 1, f2), exp_map),
                pl.BlockSpec((1, 1, f, d), exp_map),
                pl.BlockSpec((1, 1, 1, d), exp_map),
            ],
            out_specs=pl.BlockSpec((EXPERT_ROWS, sub, LANES), row_map),
            scratch_shapes=[pltpu.VMEM((d, f2), BF16), pltpu.VMEM((f, d), BF16)],
        ),
        compiler_params=_cparams(("arbitrary",)),
        name="moe_experts",
    )(block_e, n_used, xs, w_gu, b_gu.reshape(n_l, n_e, 1, f2), w_dn, b_dn.reshape(n_l, n_e, 1, d))


def _combine_kernel(x_ref, gate_ref, mod_ref, fng_ref, dest_hbm, ys_hbm, o_ref, buf, idx_smem, sem_idx, sem_rows,
                    *, final):
    t = x_ref.shape[0]
    i = pl.program_id(0)
    n_steps = pl.num_programs(0)
    slot = i % 2

    def idx_copy(step, s):
        return pltpu.make_async_copy(dest_hbm.at[step], idx_smem.at[pl.ds(s * TOP_K * t, TOP_K * t)],
                                     sem_idx.at[s])

    def start_rows(s):
        def issue(group, carry):
            tok0 = pl.multiple_of(group * ISSUE_UNROLL, ISSUE_UNROLL)
            for u in range(ISSUE_UNROLL):
                for k in range(TOP_K):
                    pltpu.make_async_copy(ys_hbm.at[idx_smem[(s * TOP_K + k) * t + tok0 + u]],
                                          buf.at[s * TOP_K + k, tok0 + u],
                                          sem_rows.at[s]).start(priority=(u * TOP_K + k) % 2)
            return carry

        lax.fori_loop(0, t // ISSUE_UNROLL, issue, 0)

    @pl.when(i == 0)
    def _():
        first = idx_copy(0, 0)
        first.start()
        first.wait()
        start_rows(0)

    for nxt in range(2):
        @pl.when(jnp.logical_and(i + 1 < n_steps, slot == 1 - nxt))
        def _(nxt=nxt):
            idx_copy(i + 1, nxt).start()

    for k in range(TOP_K):
        pltpu.make_async_copy(ys_hbm.at[pl.ds(0, t)], buf.at[slot * TOP_K + k], sem_rows.at[slot]).wait()

    for nxt in range(2):
        @pl.when(jnp.logical_and(i + 1 < n_steps, slot == 1 - nxt))
        def _(nxt=nxt):
            idx_copy(i + 1, nxt).wait()
            start_rows(nxt)

    gate = gate_ref[...]
    y = gate[:, 0:1] * buf[slot * TOP_K].reshape(x_ref.shape)
    for k in range(1, TOP_K):
        y = y + gate[:, k:k + 1] * buf[slot * TOP_K + k].reshape(x_ref.shape)
    out = x_ref[...] + mod_ref[0][5:6, :] * y
    o_ref[...] = _rms(out, fng_ref[...]) if final else out


def _combine(x2, gates, mod, final_g, dest_tiles, ys, t, tiles_per_batch, final):
    n, d = x2.shape
    return pl.pallas_call(
        functools.partial(_combine_kernel, final=final),
        out_shape=jax.ShapeDtypeStruct((n, d), F32),
        grid=(n // t,),
        in_specs=[
            pl.BlockSpec((t, d), lambda i: (i, 0)),
            pl.BlockSpec((t, LANES), lambda i: (i, 0)),
            pl.BlockSpec((1, 6, d), lambda i: (i // tiles_per_batch, 0, 0)),
            pl.BlockSpec((1, d), lambda i: (0, 0)),
            pl.BlockSpec(memory_space=pl.ANY),
            pl.BlockSpec(memory_space=pl.ANY),
        ],
        out_specs=pl.BlockSpec((t, d), lambda i: (i, 0)),
        scratch_shapes=[pltpu.VMEM((2 * TOP_K, t, d // LANES, LANES), F32), pltpu.SMEM((2 * TOP_K * t,), jnp.int32),
                        pltpu.SemaphoreType.DMA((2,)), pltpu.SemaphoreType.DMA((2,))],
        compiler_params=_cparams(("arbitrary",)),
        name="moe_combine",
    )(x2, gates, mod, final_g.reshape(1, d), dest_tiles, ys)


def _moe_layer(x, mod, ng, rw, rb, layer, w_gu, b_gu, w_dn, b_dn, final_g, final, t):
    b, s, d = x.shape
    n = b * s
    x2 = x.reshape(n, d)
    tiles_per_batch = s // t
    h, gates, meta, cnt = _route(x2, mod, ng, rw, rb, t, tiles_per_batch)
    counts = cnt[0, :N_EXPERTS].astype(jnp.int32)
    padded = (counts + EXPERT_ROWS - 1) // EXPERT_ROWS * EXPERT_ROWS
    pend = jnp.cumsum(padded)
    pstart = pend - padded
    n_blocks = -(-(n * TOP_K) // EXPERT_ROWS) + N_EXPERTS
    p_rows = n_blocks * EXPERT_ROWS
    rank = meta[:, :TOP_K]
    eidx = meta[:, TOP_K:2 * TOP_K]
    experts = jnp.arange(N_EXPERTS, dtype=jnp.int32)
    dest = rank + jnp.sum(jnp.where(eidx[..., None] == experts, pstart, 0), axis=-1)
    dest_tiles = dest.reshape(n // t, t, TOP_K).transpose(0, 2, 1).reshape(n // t, TOP_K * t)
    block_row0 = jnp.arange(n_blocks, dtype=jnp.int32) * EXPERT_ROWS
    block_e = jnp.minimum(jnp.sum((pend[None, :] <= block_row0[:, None]).astype(jnp.int32), axis=1),
                          N_EXPERTS - 1)
    n_used = (pend[-1:] // EXPERT_ROWS).astype(jnp.int32)
    xs = _dispatch(h, dest_tiles, pstart + counts, padded - counts, p_rows, t)
    ys = _experts(xs, block_e, n_used, layer, w_gu, b_gu, w_dn, b_dn)
    out = _combine(x2, gates, mod, final_g, dest_tiles, ys, t, tiles_per_batch, final)
    return out.reshape(b, s, d)


def _kv_kernel(x_ref, mod_ref, ng_ref, wkv_ref, wf_ref, bf_ref, kg_ref, hsum_ref,
               kt_ref, v_ref, qf_ref, fb_ref, carry):
    t, d = x_ref.shape[1], x_ref.shape[2]
    n_pairs = N_HEADS // 2

    @pl.when(pl.program_id(1) == 0)
    def _():
        carry[...] = jnp.zeros(carry.shape, F32)

    mod = mod_ref[0]
    h = _rms(x_ref[0], ng_ref[...]) * (1.0 + mod[1:2, :]) + mod[0:1, :]
    h_hi = h.astype(BF16)
    kv = jnp.dot(h_hi, wkv_ref[...], preferred_element_type=F32)
    k = kv[:, :d]
    v_ref[0] = kv[:, d:].astype(BF16)
    ms = jnp.dot((k * k).astype(BF16), hsum_ref[...], preferred_element_type=F32)
    k = k * lax.rsqrt(ms + EPS) * kg_ref[...]
    kt = k.T
    for hp in range(n_pairs):
        kt_ref[0, hp, 0, pl.ds(0, LANES), :] = kt[hp * LANES:(hp + 1) * LANES, :].astype(BF16)

    h_lo = (h - h_hi.astype(F32)).astype(BF16)
    fz2 = jnp.dot(h_hi, wf_ref[...], preferred_element_type=F32)
    fz = (fz2[:, :LANES] + fz2[:, LANES:] + bf_ref[...]
          + jnp.dot(h_lo, wf_ref[:, pl.ds(0, LANES)], preferred_element_type=F32))
    ls = jax.nn.log_sigmoid(fz)
    r_i = lax.broadcasted_iota(jnp.int32, (t, t), 0)
    c_i = lax.broadcasted_iota(jnp.int32, (t, t), 1)
    tri = (c_i <= r_i).astype(BF16)
    ls_hi, ls_mid, ls_lo = _split3(ls)
    cum2 = jnp.dot(tri, jnp.concatenate([ls_hi, ls_mid], axis=1).astype(BF16), preferred_element_type=F32)
    cum = (cum2[:, :LANES] + cum2[:, LANES:] + carry[0:1, :]
           + jnp.dot(tri, ls_lo.astype(BF16), preferred_element_type=F32))
    carry[...] = jnp.broadcast_to(cum[t - 1:t, :], carry.shape)
    f2 = cum * LOG2E
    row8 = lax.broadcasted_iota(jnp.int32, (8, LANES), 0)
    fb_ref[0, 0] = jnp.where(row8 == 0, f2[0:1, :], jnp.where(row8 == 1, f2[t - 1:t, :], 0.0))
    f2t = f2.T
    lane = lax.broadcasted_iota(jnp.int32, (t, LANES), 1)
    sub = lax.broadcasted_iota(jnp.int32, (LANES, t), 0)
    q_pieces = _split3(f2)
    k_pieces = _split3(-f2t)
    for hp in range(n_pairs):
        q_aug = jnp.zeros((t, LANES), F32)
        k_aug = jnp.zeros((LANES, t), F32)
        for hh in range(2):
            head = 2 * hp + hh
            o = hh * 2 * N_FPIECES
            for p in range(N_FPIECES):
                q_aug = jnp.where(lane == o + p, q_pieces[p][:, head:head + 1], q_aug)
                q_aug = jnp.where(lane == o + N_FPIECES + p, 1.0, q_aug)
                k_aug = jnp.where(sub == o + p, 1.0, k_aug)
                k_aug = jnp.where(sub == o + N_FPIECES + p, k_pieces[p][head:head + 1, :], k_aug)
        qf_ref[0, hp] = q_aug.astype(BF16)
        kt_ref[0, hp, 0, pl.ds(LANES, LANES), :] = k_aug.astype(BF16)


def _shared_kv(x, kvmod, ng, w_kvf, b_f, k_norm_g, t):
    b, s, d = x.shape
    n_pairs = N_HEADS // 2
    wkv = w_kvf[:, :2 * d].astype(BF16)
    wf = jnp.zeros((d, LANES), F32).at[:, :N_HEADS].set(w_kvf[:, 2 * d:])
    wf_hi = wf.astype(BF16)
    wf2 = jnp.concatenate([wf_hi, (wf - wf_hi.astype(F32)).astype(BF16)], axis=1)
    bf = jnp.zeros((1, LANES), F32).at[0, :N_HEADS].set(b_f)
    kg = jnp.tile(k_norm_g, N_HEADS).reshape(1, d)
    head_of = jnp.arange(d) // HEAD_DIM
    hsum = ((head_of[:, None] == head_of[None, :]).astype(F32) / HEAD_DIM).astype(BF16)
    const = lambda shape: pl.BlockSpec(shape, lambda bi, si: (0,) * len(shape))
    return pl.pallas_call(
        _kv_kernel,
        out_shape=(
            jax.ShapeDtypeStruct((b, n_pairs, s // t, 2 * LANES, t), BF16),
            jax.ShapeDtypeStruct((b, s, d), BF16),
            jax.ShapeDtypeStruct((b, n_pairs, s, LANES), BF16),
            jax.ShapeDtypeStruct((b, s // t, 8, LANES), F32),
        ),
        grid=(b, s // t),
        in_specs=[
            pl.BlockSpec((1, t, d), lambda bi, si: (bi, si, 0)),
            pl.BlockSpec((1, 2, d), lambda bi, si: (bi, 0, 0)),
            const((1, d)), const((d, 2 * d)), const((d, 2 * LANES)), const((1, LANES)), const((1, d)),
            const((d, d)),
        ],
        out_specs=(
            pl.BlockSpec((1, n_pairs, 1, 2 * LANES, t), lambda bi, si: (bi, 0, si, 0, 0)),
            pl.BlockSpec((1, t, d), lambda bi, si: (bi, si, 0)),
            pl.BlockSpec((1, n_pairs, t, LANES), lambda bi, si: (bi, 0, si, 0)),
            pl.BlockSpec((1, 1, 8, LANES), lambda bi, si: (bi, si, 0, 0)),
        ),
        scratch_shapes=[pltpu.VMEM((8, LANES), F32)],
        compiler_params=_cparams(("arbitrary", "arbitrary")),
        name="shared_kv",
    )(x, kvmod, ng.reshape(1, d), wkv, wf2, bf, kg, hsum), hsum


def _qg_kernel(x_ref, mod_ref, ng_ref, w_ref, qg_ref, hsum_ref, q_ref, g_ref):
    d = x_ref.shape[2]
    mod = mod_ref[0]
    h = _rms(x_ref[0], ng_ref[...]) * (1.0 + mod[1:2, :]) + mod[0:1, :]
    qg = jnp.dot(h.astype(BF16), w_ref[...], preferred_element_type=F32)
    q = qg[:, :d]
    ms = jnp.dot((q * q).astype(BF16), hsum_ref[...], preferred_element_type=F32)
    q = q * lax.rsqrt(ms + EPS) * qg_ref[...] * (LOG2E / math.sqrt(HEAD_DIM))
    q_ref[0] = q.astype(BF16)
    g_ref[0] = jax.nn.sigmoid(qg[:, d:]).astype(BF16)


def _qg(x, mod, ng, w_qg, q_norm_g, hsum, t):
    b, s, d = x.shape
    const = lambda shape: pl.BlockSpec(shape, lambda bi, si: (0,) * len(shape))
    tile = pl.BlockSpec((1, t, d), lambda bi, si: (bi, si, 0))
    return pl.pallas_call(
        _qg_kernel,
        out_shape=(jax.ShapeDtypeStruct((b, s, d), BF16), jax.ShapeDtypeStruct((b, s, d), BF16)),
        grid=(b, s // t),
        in_specs=[tile, pl.BlockSpec((1, 6, d), lambda bi, si: (bi, 0, 0)),
                  const((1, d)), const((d, 2 * d)), const((1, d)), const((d, d))],
        out_specs=(tile, tile),
        compiler_params=_cparams(("arbitrary", "arbitrary")),
        name="attn_qg",
    )(x, mod, ng.reshape(1, d), w_qg.astype(BF16), jnp.tile(q_norm_g, N_HEADS).reshape(1, d), hsum)


def _attn_kernel(j0_ref, q_ref, qf_ref, kt_ref, v_ref, o_ref, qa_scr, m_scr, l_scr, acc_scr, *, online):
    tq = q_ref.shape[1]
    tk = kt_ref.shape[4]
    i = pl.program_id(2)
    n_q = pl.num_programs(2)
    head0 = pl.program_id(0) * N_HEADS + 2 * pl.program_id(1)
    first = [j0_ref[(head0 + hh) * n_q + i] for hh in range(2)]
    first_both = jnp.maximum(first[0], first[1])
    lane = lax.broadcasted_iota(jnp.int32, (tq, LANES), 1)
    q2 = q_ref[0]
    qf = qf_ref[0, 0]
    zero = jnp.zeros((), BF16)
    n_aug = 2 * N_FPIECES
    qa_scr[0, :, pl.ds(0, LANES)] = jnp.where(lane < HEAD_DIM, q2, zero)
    qa_scr[0, :, pl.ds(LANES, LANES)] = jnp.where(lane < n_aug, qf, zero)
    qa_scr[1, :, pl.ds(0, LANES)] = jnp.where(lane >= HEAD_DIM, q2, zero)
    qa_scr[1, :, pl.ds(LANES, LANES)] = jnp.where(jnp.logical_and(lane >= n_aug, lane < 2 * n_aug), qf, zero)
    if online:
        m_scr[...] = jnp.full(m_scr.shape, NEG_BIG, F32)
    l_scr[...] = jnp.zeros(l_scr.shape, F32)
    acc_scr[...] = jnp.zeros(acc_scr.shape, F32)

    def scores(hh, j, masked):
        s = jnp.dot(qa_scr[hh], kt_ref[0, 0, j], preferred_element_type=F32)
        if masked:
            r_i = lax.broadcasted_iota(jnp.int32, (tq, tk), 0)
            c_i = lax.broadcasted_iota(jnp.int32, (tq, tk), 1)
            s = jnp.where(c_i <= r_i, s, NEG_BIG)
        return s

    def values(j):
        return v_ref[0, pl.ds(pl.multiple_of(j * tk, tk), tk), :]

    def tiles(js, masked, heads):
        for hh in heads:
            if online:
                for j in js:
                    s = scores(hh, j, masked)
                    m_prev = m_scr[hh]
                    m_new = jnp.maximum(m_prev, jnp.max(s, axis=-1, keepdims=True))
                    alpha = jnp.exp2(m_prev - m_new)
                    p = jnp.exp2(s - m_new[:, 0:1])
                    l_scr[hh] = alpha * l_scr[hh] + jnp.sum(p, axis=-1, keepdims=True)
                    acc_scr[hh] = alpha * acc_scr[hh] + jnp.dot(p.astype(BF16), values(j),
                                                                preferred_element_type=F32)
                    m_scr[hh] = m_new
            else:
                part, pv = l_scr[hh], acc_scr[hh]
                for j in js:
                    p = jnp.exp2(scores(hh, j, masked))
                    for c in range(tk // LANES):
                        part = part + p[:, c * LANES:(c + 1) * LANES]
                    pv = pv + jnp.dot(p.astype(BF16), values(j), preferred_element_type=F32)
                l_scr[hh], acc_scr[hh] = part, pv

    def one_tile(heads):
        def body(j, carry):
            tiles((j,), False, heads)
            return carry
        return body

    def tile_group(group, carry):
        j = first_both + KV_UNROLL * group
        tiles(tuple(j + u for u in range(KV_UNROLL)), False, (0, 1))
        return carry

    for hh in range(2):
        n_pairs_hh = (first_both - first[hh]) // 2

        def tile_pair(pair, carry, hh=hh):
            j = first[hh] + 2 * pair
            tiles((j, j + 1), False, (hh,))
            return carry

        lax.fori_loop(0, n_pairs_hh, tile_pair, 0)
        lax.fori_loop(first[hh] + 2 * n_pairs_hh, first_both, one_tile((hh,)), 0)
    n_groups = (i - first_both) // KV_UNROLL
    lax.fori_loop(0, n_groups, tile_group, 0)
    lax.fori_loop(first_both + KV_UNROLL * n_groups, i, one_tile((0, 1)), 0)
    tiles((i,), True, (0, 1))
    if online:
        l0, l1 = l_scr[0], l_scr[1]
    else:
        l0 = jnp.sum(l_scr[0], axis=-1, keepdims=True)
        l1 = jnp.sum(l_scr[1], axis=-1, keepdims=True)
    o_ref[0] = jnp.where(lane < HEAD_DIM, acc_scr[0] / l0, acc_scr[1] / l1).astype(BF16)


def _attention(j0, q, qf, kt, v, t, online):
    b, s, d = q.shape
    n_pairs = N_HEADS // 2
    nkv = s // t
    return pl.pallas_call(
        functools.partial(_attn_kernel, online=online),
        out_shape=jax.ShapeDtypeStruct((b, s, d), BF16),
        grid_spec=pltpu.PrefetchScalarGridSpec(
            num_scalar_prefetch=1,
            grid=(b, n_pairs, s // t),
            in_specs=[
                pl.BlockSpec((1, t, LANES), lambda bi, hp, i, j0r: (bi, i, hp)),
                pl.BlockSpec((1, 1, t, LANES), lambda bi, hp, i, j0r: (bi, hp, i, 0)),
                pl.BlockSpec((1, 1, nkv, 2 * LANES, t), lambda bi, hp, i, j0r: (bi, hp, 0, 0, 0)),
                pl.BlockSpec((1, s, LANES), lambda bi, hp, i, j0r: (bi, 0, hp)),
            ],
            out_specs=pl.BlockSpec((1, t, LANES), lambda bi, hp, i, j0r: (bi, i, hp)),
            scratch_shapes=[
                pltpu.VMEM((2, t, 2 * LANES), BF16),
                pltpu.VMEM((2, t, LANES), F32),
                pltpu.VMEM((2, t, LANES), F32),
                pltpu.VMEM((2, t, LANES), F32),
            ],
        ),
        compiler_params=_cparams(("arbitrary", "arbitrary", "arbitrary")),
        name="fox_attention_online" if online else "fox_attention",
    )(j0, q, qf, kt, v)


def _attn_out_kernel(x_ref, o_ref, g_ref, mod_ref, w_ref, out_ref):
    og = o_ref[0] * g_ref[0]
    y = jnp.dot(og, w_ref[...], preferred_element_type=F32)
    out_ref[0] = x_ref[0] + mod_ref[0][2:3, :] * y


def _attn_out(x, o, g, mod, w_o, t):
    b, s, d = x.shape
    tile = pl.BlockSpec((1, t, d), lambda bi, si: (bi, si, 0))
    return pl.pallas_call(
        _attn_out_kernel,
        out_shape=jax.ShapeDtypeStruct((b, s, d), F32),
        grid=(b, s // t),
        in_specs=[tile, tile, tile, pl.BlockSpec((1, 6, d), lambda bi, si: (bi, 0, 0)),
                  pl.BlockSpec((d, d), lambda bi, si: (0, 0))],
        out_specs=tile,
        compiler_params=_cparams(("arbitrary", "arbitrary")),
        name="attn_out",
    )(x, o, g, mod, w_o.astype(BF16))


def _fox_layer(x, mod, ng, w_qg, q_norm_g, k_norm_g, w_o, kv, t):
    (kt, v, qf, fb), hsum = kv
    n_t = x.shape[1] // t
    q, g = _qg(x, mod, ng, w_qg, q_norm_g, hsum, t)
    bound = (HEAD_DIM * jnp.max(jnp.abs(q_norm_g)) * jnp.max(jnp.abs(k_norm_g))
             * (LOG2E / math.sqrt(HEAD_DIM)))
    f_first, f_last = fb[:, :, 0, :N_HEADS], fb[:, :, 1, :N_HEADS]
    best = (bound * BOUND_SLACK + f_first[:, :, None, :]) - f_last[:, None, :, :]
    before = jnp.arange(n_t)[None, :] < jnp.arange(n_t)[:, None]
    dead = jnp.logical_and(best < ZERO_WEIGHT_EXPONENT, before[None, :, :, None])
    j0 = jnp.sum(dead.astype(jnp.int32), axis=2).transpose(0, 2, 1).reshape(-1)
    o = lax.cond(bound <= DIRECT_EXP_LIMIT,
                 functools.partial(_attention, t=t, online=False),
                 lambda j0_, *rest: _attention(jnp.zeros_like(j0_), *rest, t=t, online=True),
                 j0, q, qf, kt, v)
    return _attn_out(x, o, g, mod, w_o, t)


def kernel(x, c, mod_w, mod_b, norm1_g, norm2_g, conv_w_pw1, conv_b_pw1, conv_w_dw, conv_b_dw, conv_ln_g, conv_ln_b, conv_w_pw2, conv_b_pw2, kv_mod_w, kv_mod_b, kv_norm_g, w_kvf, b_f, k_norm_g, attn_w_qg, q_norm_g, attn_w_o, moe_router_w, moe_router_b, moe_w_gu, moe_b_gu, moe_w_down, moe_b_down, final_norm_g):
    b, s, d = x.shape
    depth = mod_w.shape[0]
    n_a = conv_w_pw1.shape[0]
    t = min(512, s)
    c8 = jnp.zeros((8, d), F32).at[:b].set(c)
    mods = _mods(c8, mod_w, mod_b)[:, :b].reshape(depth, b, 6, d)
    kvmod = _mods(c8, kv_mod_w[None], kv_mod_b[None])[0, :b].reshape(b, 2, d)
    kv = None
    for l in range(depth):
        if l < n_a:
            x = _conv_layer(x, mods[l], norm1_g[l], conv_w_pw1[l], conv_b_pw1[l], conv_w_dw[l],
                            conv_b_dw[l], conv_ln_g[l], conv_ln_b[l], conv_w_pw2[l], conv_b_pw2[l], t)
        else:
            lb = l - n_a
            x = _fox_layer(x, mods[l], norm1_g[l], attn_w_qg[lb], q_norm_g[lb], k_norm_g, attn_w_o[lb], kv, t)
        x = _moe_layer(x, mods[l], norm2_g[l], moe_router_w[l], moe_router_b[l], l, moe_w_gu,
                       moe_b_gu, moe_w_down, moe_b_down, final_norm_g, l == depth - 1, t)
        if l == n_a - 1:
            kv = _shared_kv(x, kvmod, kv_norm_g, w_kvf, b_f, k_norm_g, t)
    return x
```

```python
import functools
import math

import jax
import jax.numpy as jnp
from jax import lax
from jax.experimental import pallas as pl
from jax.experimental.pallas import tpu as pltpu

N_HEADS = 16
HEAD_DIM = 64
CONV_WIDTH = 31
N_EXPERTS = 32
TOP_K = 4
SWIGLU_ALPHA = 1.702
SWIGLU_LIMIT = 7.0
EPS = 1e-6

LANES = 128
HALO = 32
CONV_ROWS = 16
ISSUE_UNROLL = 4
KV_UNROLL = 4
EXPERT_ROWS = 512
VMEM_LIMIT = 56 * 1024 * 1024
LOG2E = 1.4426950408889634
NEG_BIG = -1e30
N_FPIECES = 3
DIRECT_EXP_LIMIT = 60.0
ZERO_WEIGHT_EXPONENT = -160.0
BOUND_SLACK = 1.05

F32 = jnp.float32
BF16 = jnp.bfloat16
HIGHEST = lax.Precision.HIGHEST


def _cparams(sem):
    return pltpu.CompilerParams(dimension_semantics=sem, vmem_limit_bytes=VMEM_LIMIT)


def _rms(x, g):
    return x * lax.rsqrt(jnp.mean(x * x, axis=-1, keepdims=True) + EPS) * g


def _split3(f):
    hi = f.astype(BF16).astype(F32)
    r1 = f - hi
    mid = r1.astype(BF16).astype(F32)
    lo = (r1 - mid).astype(BF16).astype(F32)
    return hi, mid, lo


def _mods_kernel(c_ref, w_ref, b_ref, o_ref):
    c = c_ref[...]
    ca = c * jax.nn.sigmoid(c)
    o_ref[0] = jnp.dot(ca, w_ref[0], precision=HIGHEST, preferred_element_type=F32) + b_ref[0]


def _mods(c8, w, b):
    n_l, d, m = w.shape
    tn = min(m, 1024)
    return pl.pallas_call(
        _mods_kernel,
        out_shape=jax.ShapeDtypeStruct((n_l, 8, m), F32),
        grid=(n_l, m // tn),
        in_specs=[
            pl.BlockSpec((8, d), lambda l, j: (0, 0)),
            pl.BlockSpec((1, d, tn), lambda l, j: (l, 0, j)),
            pl.BlockSpec((1, 1, tn), lambda l, j: (l, 0, j)),
        ],
        out_specs=pl.BlockSpec((1, 8, tn), lambda l, j: (l, 0, j)),
        compiler_params=_cparams(("arbitrary", "arbitrary")),
        name="mods",
    )(c8, w, b.reshape(n_l, 1, m))


def _conv_kernel(x_ref, mod_ref, ng_ref, w1_ref, b1_ref, wdw_ref, bdw_ref, lng_ref, lnb_ref,
                 w2_ref, b2_ref, ng2_ref, rw_ref, rb_ref,
                 o_ref, h_ref, gate_ref, meta_ref, cnt_ref, ubuf, cbuf):
    t, d = x_ref.shape[1], x_ref.shape[2]
    slab = ubuf.shape[1:]

    @pl.when(pl.program_id(1) == 0)
    def _():
        ubuf[pl.ds(0, HALO)] = jnp.zeros((HALO,) + slab, F32)

    x = x_ref[0]
    mod = mod_ref[0]
    h = _rms(x, ng_ref[...]) * (1.0 + mod[1:2, :]) + mod[0:1, :]
    u = jnp.dot(h.astype(BF16), w1_ref[...], preferred_element_type=F32) + b1_ref[...]
    u = u[:, :d] * jax.nn.sigmoid(u[:, d:])
    ubuf[pl.ds(HALO, t)] = u.reshape((t,) + slab)

    def conv_rows(c, carry):
        base = pl.multiple_of(c * CONV_ROWS, CONV_ROWS)
        acc = jnp.zeros((CONV_ROWS,) + slab, F32)
        for j in range(CONV_WIDTH):
            acc = acc + ubuf[pl.ds(base + (HALO - (CONV_WIDTH - 1) + j), CONV_ROWS)] * wdw_ref[j]
        cbuf[pl.ds(base, CONV_ROWS)] = acc
        return carry

    lax.fori_loop(0, t // CONV_ROWS, conv_rows, 0)
    ubuf[pl.ds(0, HALO)] = ubuf[pl.ds(t, HALO)]
    acc = cbuf[...].reshape(t, d) + bdw_ref[...]
    mu = jnp.mean(acc, axis=-1, keepdims=True)
    cen = acc - mu
    var = jnp.mean(cen * cen, axis=-1, keepdims=True)
    y = cen * lax.rsqrt(var + EPS) * lng_ref[...] + lnb_ref[...]
    y = y * jax.nn.sigmoid(y)
    y = jnp.dot(y.astype(BF16), w2_ref[...], preferred_element_type=F32) + b2_ref[...]
    x_new = x + mod[2:3, :] * y
    o_ref[0] = x_new
    _route_tail(x_new, mod, ng2_ref, rw_ref, rb_ref, h_ref, gate_ref, meta_ref, cnt_ref)


def _conv_layer(x, mod, ng, w1, b1, wdw, bdw, lng, lnb, w2, b2, route, t):
    b, s, d = x.shape
    row = lambda a: a.reshape(1, -1)
    slab = (d // LANES, LANES)
    wdw_p = jnp.zeros((HALO, d), F32).at[:CONV_WIDTH].set(wdw).reshape((HALO,) + slab)
    const = lambda shape: pl.BlockSpec(shape, lambda bi, si: (0,) * len(shape))
    r_ops, r_in, r_shapes, r_out = _route_plumbing(*route, b, s, d, t)
    outs = pl.pallas_call(
        _conv_kernel,
        out_shape=(jax.ShapeDtypeStruct((b, s, d), F32),) + r_shapes,
        grid=(b, s // t),
        in_specs=[
            pl.BlockSpec((1, t, d), lambda bi, si: (bi, si, 0)),
            pl.BlockSpec((1, 6, d), lambda bi, si: (bi, 0, 0)),
            const((1, d)), const((d, 2 * d)), const((1, 2 * d)), const((HALO,) + slab), const((1, d)),
            const((1, d)), const((1, d)), const((d, d)), const((1, d)),
        ] + r_in,
        out_specs=(pl.BlockSpec((1, t, d), lambda bi, si: (bi, si, 0)),) + r_out,
        scratch_shapes=[pltpu.VMEM((t + HALO,) + slab, F32), pltpu.VMEM((t,) + slab, F32)],
        compiler_params=_cparams(("arbitrary", "arbitrary")),
        name="conv_layer",
    )(x, mod, row(ng), w1.astype(BF16), row(b1), wdw_p, row(bdw), row(lng), row(lnb),
      w2.astype(BF16), row(b2), *r_ops)
    return outs[0], outs[1:]


def _route_tail(x, mod, ng_ref, rw_ref, rb_ref, h_ref, gate_ref, meta_ref, cnt_ref):
    t = x.shape[0]

    @pl.when(jnp.logical_and(pl.program_id(0) == 0, pl.program_id(1) == 0))
    def _():
        cnt_ref[...] = jnp.zeros(cnt_ref.shape, F32)

    h = _rms(x, ng_ref[...]) * (1.0 + mod[4:5, :]) + mod[3:4, :]
    h_ref[...] = h.reshape(h_ref.shape)
    h_hi = h.astype(BF16)
    h_lo = (h - h_hi.astype(F32)).astype(BF16)
    lg2 = jnp.dot(h_hi, rw_ref[...], preferred_element_type=F32)
    logits = (lg2[:, :LANES] + lg2[:, LANES:] + rb_ref[...]
              + jnp.dot(h_lo, rw_ref[:, pl.ds(0, LANES)], preferred_element_type=F32))
    lane = lax.broadcasted_iota(jnp.int32, (t, LANES), 1).astype(F32)
    work = logits
    vals, idxs = [], []
    for _ in range(TOP_K):
        m = jnp.max(work, axis=-1, keepdims=True)
        idx = jnp.min(jnp.where(work == m, lane, float(LANES)), axis=-1, keepdims=True)
        vals.append(m)
        idxs.append(idx)
        work = jnp.where(lane == idx, -jnp.inf, work)
    exps = [jnp.exp(v - vals[0]) for v in vals]
    denom = exps[0] + exps[1] + exps[2] + exps[3]
    onehot = jnp.zeros((t, LANES), F32)
    for idx in idxs:
        onehot = onehot + (lane == idx).astype(F32)
    r_i = lax.broadcasted_iota(jnp.int32, (t, t), 0)
    c_i = lax.broadcasted_iota(jnp.int32, (t, t), 1)
    tri = (c_i < r_i).astype(BF16)
    base = jnp.dot(tri, onehot.astype(BF16), preferred_element_type=F32) + cnt_ref[0:1, :]
    gate_out = jnp.zeros((t, LANES), F32)
    meta = jnp.zeros((t, LANES), F32)
    for k in range(TOP_K):
        rank = jnp.sum(jnp.where(lane == idxs[k], base, 0.0), axis=-1, keepdims=True)
        gate_out = jnp.where(lane == k, exps[k] / denom, gate_out)
        meta = jnp.where(lane == k, rank, meta)
        meta = jnp.where(lane == TOP_K + k, idxs[k], meta)
    gate_ref[...] = gate_out
    meta_ref[...] = meta.astype(jnp.int32)
    cnt_ref[...] = cnt_ref[...] + jnp.sum(onehot, axis=0, keepdims=True)


def _route_plumbing(ng, rw, rb, b, s, d, t):
    n = b * s
    tiles_per_batch = s // t
    rw_p = jnp.zeros((d, LANES), F32).at[:, :N_EXPERTS].set(rw)
    rw_hi = rw_p.astype(BF16)
    rw2 = jnp.concatenate([rw_hi, (rw_p - rw_hi.astype(F32)).astype(BF16)], axis=1)
    rb_p = jnp.full((1, LANES), NEG_BIG, F32).at[0, :N_EXPERTS].set(rb)
    const = lambda shape: pl.BlockSpec(shape, lambda bi, si: (0,) * len(shape))
    flat = lambda bi, si: bi * tiles_per_batch + si
    operands = (ng.reshape(1, d), rw2, rb_p)
    in_specs = [const((1, d)), const((d, 2 * LANES)), const((1, LANES))]
    out_shapes = (
        jax.ShapeDtypeStruct((n, d // LANES, LANES), F32),
        jax.ShapeDtypeStruct((n, LANES), F32),
        jax.ShapeDtypeStruct((n, LANES), jnp.int32),
        jax.ShapeDtypeStruct((8, LANES), F32),
    )
    out_specs = (
        pl.BlockSpec((t, d // LANES, LANES), lambda bi, si: (flat(bi, si), 0, 0)),
        pl.BlockSpec((t, LANES), lambda bi, si: (flat(bi, si), 0)),
        pl.BlockSpec((t, LANES), lambda bi, si: (flat(bi, si), 0)),
        const((8, LANES)),
    )
    return operands, in_specs, out_shapes, out_specs


def _dispatch_kernel(pad0_ref, padn_ref, h_ref, dest_hbm, xs_hbm, zrow, idx_smem, sem_idx, sem_rows, sem_pad):
    t = h_ref.shape[0]
    i = pl.program_id(0)

    @pl.when(i == 0)
    def _():
        zrow[...] = jnp.zeros(zrow.shape, F32)
        for start in (True, False):
            def per_expert(e, carry, start=start):
                def per_row(r, c):
                    pad_cp = pltpu.make_async_copy(zrow, xs_hbm.at[pad0_ref[e] + r], sem_pad)
                    if start:
                        pad_cp.start()
                    else:
                        pad_cp.wait()
                    return c
                return lax.fori_loop(0, padn_ref[e], per_row, carry)
            lax.fori_loop(0, N_EXPERTS, per_expert, 0)

    n_steps = pl.num_programs(0)
    n_idx = TOP_K * t

    def idx_copy(step, s):
        return pltpu.make_async_copy(dest_hbm.at[step], idx_smem.at[pl.ds(s * n_idx, n_idx)], sem_idx.at[s])

    @pl.when(i == 0)
    def _():
        idx_copy(0, 0).start()

    def scatter_rows(s):
        idx_copy(i, s).wait()

        @pl.when(i + 1 < n_steps)
        def _():
            idx_copy(i + 1, 1 - s).start()

        def issue(group, carry):
            tok0 = pl.multiple_of(group * ISSUE_UNROLL, ISSUE_UNROLL)
            for u in range(ISSUE_UNROLL):
                for k in range(TOP_K):
                    pltpu.make_async_copy(h_ref.at[tok0 + u], xs_hbm.at[idx_smem[s * n_idx + k * t + tok0 + u]],
                                          sem_rows).start(priority=(u * TOP_K + k) % 2)
            return carry

        lax.fori_loop(0, t // ISSUE_UNROLL, issue, 0)

    for s in range(2):
        @pl.when(i % 2 == s)
        def _(s=s):
            scatter_rows(s)

    for k in range(TOP_K):
        pltpu.make_async_copy(h_ref, xs_hbm.at[pl.ds(0, t)], sem_rows).wait()


def _dispatch(h, dest_tiles, pad_start, pad_n, p_rows, t):
    n, sub, _ = h.shape
    return pl.pallas_call(
        _dispatch_kernel,
        out_shape=jax.ShapeDtypeStruct((p_rows, sub, LANES), F32),
        grid_spec=pltpu.PrefetchScalarGridSpec(
            num_scalar_prefetch=2,
            grid=(n // t,),
            in_specs=[
                pl.BlockSpec((t, sub, LANES), lambda i, p0, pn: (i, 0, 0)),
                pl.BlockSpec(memory_space=pl.ANY),
            ],
            out_specs=pl.BlockSpec(memory_space=pl.ANY),
            scratch_shapes=[pltpu.VMEM((sub, LANES), F32), pltpu.SMEM((2 * TOP_K * t,), jnp.int32),
                            pltpu.SemaphoreType.DMA((2,)), pltpu.SemaphoreType.DMA, pltpu.SemaphoreType.DMA],
        ),
        compiler_params=_cparams(("arbitrary",)),
        name="moe_dispatch",
    )(pad_start, pad_n, h, dest_tiles)


def _experts_kernel(be_ref, nu_ref, xs_ref, wgu_ref, bgu_ref, wdn_ref, bdn_ref, ys_ref, wgu_bf, wdn_bf):
    b = pl.program_id(0)
    f = wdn_ref.shape[2]
    e = be_ref[b]
    e_prev = be_ref[jnp.maximum(b - 1, 0)]

    @pl.when(jnp.logical_or(b == 0, e != e_prev))
    def _():
        wgu_bf[...] = wgu_ref[0, 0].astype(BF16)
        wdn_bf[...] = wdn_ref[0, 0].astype(BF16)

    @pl.when(b < nu_ref[0])
    def _():
        rows, sub, _ = xs_ref.shape
        x = xs_ref[...].reshape(rows, sub * LANES).astype(BF16)
        gu = jnp.dot(x, wgu_bf[...], preferred_element_type=F32) + bgu_ref[0, 0]
        x_glu = jnp.minimum(gu[:, :f], SWIGLU_LIMIT)
        x_lin = jnp.clip(gu[:, f:], -SWIGLU_LIMIT, SWIGLU_LIMIT)
        act = x_glu * jax.nn.sigmoid(SWIGLU_ALPHA * x_glu) * (x_lin + 1.0)
        y = jnp.dot(act.astype(BF16), wdn_bf[...], preferred_element_type=F32) + bdn_ref[0, 0]
        ys_ref[...] = y.reshape(ys_ref.shape)


def _experts(xs, block_e, n_used, layer, w_gu, b_gu, w_dn, b_dn):
    p_rows, sub, _ = xs.shape
    n_l, n_e, d, f2 = w_gu.shape
    f = f2 // 2
    n_blocks = p_rows // EXPERT_ROWS
    row_map = lambda b, be, nu: (jnp.minimum(b, nu[0] - 1), 0, 0)
    exp_map = lambda b, be, nu: (layer, be[b], 0, 0)
    return pl.pallas_call(
        _experts_kernel,
        out_shape=jax.ShapeDtypeStruct((p_rows, sub, LANES), F32),
        grid_spec=pltpu.PrefetchScalarGridSpec(
            num_scalar_prefetch=2,
            grid=(n_blocks,),
            in_specs=[
                pl.BlockSpec((EXPERT_ROWS, sub, LANES), row_map),
                pl.BlockSpec((1, 1, d, f2), exp_map),
                pl.BlockSpec((1, 1, 1, f2), exp_map),
                pl.BlockSpec((1, 1, f, d), exp_map),
                pl.BlockSpec((1, 1, 1, d), exp_map),
            ],
            out_specs=pl.BlockSpec((EXPERT_ROWS, sub, LANES), row_map),
            scratch_shapes=[pltpu.VMEM((d, f2), BF16), pltpu.VMEM((f, d), BF16)],
        ),
        compiler_params=_cparams(("arbitrary",)),
        name="moe_experts",
    )(block_e, n_used, xs, w_gu, b_gu.reshape(n_l, n_e, 1, f2), w_dn, b_dn.reshape(n_l, n_e, 1, d))


def _combine_kernel(x_ref, gate_ref, mod_ref, fng_ref, dest_hbm, ys_hbm, o_ref, buf, idx_smem, sem_idx, sem_rows,
                    *, final):
    t = x_ref.shape[0]
    i = pl.program_id(0)
    n_steps = pl.num_programs(0)
    slot = i % 2

    def idx_copy(step, s):
        return pltpu.make_async_copy(dest_hbm.at[step], idx_smem.at[pl.ds(s * TOP_K * t, TOP_K * t)],
                                     sem_idx.at[s])

    def start_rows(s):
        def issue(group, carry):
            tok0 = pl.multiple_of(group * ISSUE_UNROLL, ISSUE_UNROLL)
            for u in range(ISSUE_UNROLL):
                for k in range(TOP_K):
                    pltpu.make_async_copy(ys_hbm.at[idx_smem[(s * TOP_K + k) * t + tok0 + u]],
                                          buf.at[s * TOP_K + k, tok0 + u],
                                          sem_rows.at[s]).start(priority=(u * TOP_K + k) % 2)
            return carry

        lax.fori_loop(0, t // ISSUE_UNROLL, issue, 0)

    @pl.when(i == 0)
    def _():
        first = idx_copy(0, 0)
        first.start()
        first.wait()
        start_rows(0)

    for nxt in range(2):
        @pl.when(jnp.logical_and(i + 1 < n_steps, slot == 1 - nxt))
        def _(nxt=nxt):
            idx_copy(i + 1, nxt).start()

    for k in range(TOP_K):
        pltpu.make_async_copy(ys_hbm.at[pl.ds(0, t)], buf.at[slot * TOP_K + k], sem_rows.at[slot]).wait()

    for nxt in range(2):
        @pl.when(jnp.logical_and(i + 1 < n_steps, slot == 1 - nxt))
        def _(nxt=nxt):
            idx_copy(i + 1, nxt).wait()
            start_rows(nxt)

    gate = gate_ref[...]
    y = gate[:, 0:1] * buf[slot * TOP_K].reshape(x_ref.shape)
    for k in range(1, TOP_K):
        y = y + gate[:, k:k + 1] * buf[slot * TOP_K + k].reshape(x_ref.shape)
    out = x_ref[...] + mod_ref[0][5:6, :] * y
    o_ref[...] = _rms(out, fng_ref[...]) if final else out


def _combine(x2, gates, mod, final_g, dest_tiles, ys, t, tiles_per_batch, final):
    n, d = x2.shape
    return pl.pallas_call(
        functools.partial(_combine_kernel, final=final),
        out_shape=jax.ShapeDtypeStruct((n, d), F32),
        grid=(n // t,),
        in_specs=[
            pl.BlockSpec((t, d), lambda i: (i, 0)),
            pl.BlockSpec((t, LANES), lambda i: (i, 0)),
            pl.BlockSpec((1, 6, d), lambda i: (i // tiles_per_batch, 0, 0)),
            pl.BlockSpec((1, d), lambda i: (0, 0)),
            pl.BlockSpec(memory_space=pl.ANY),
            pl.BlockSpec(memory_space=pl.ANY),
        ],
        out_specs=pl.BlockSpec((t, d), lambda i: (i, 0)),
        scratch_shapes=[pltpu.VMEM((2 * TOP_K, t, d // LANES, LANES), F32), pltpu.SMEM((2 * TOP_K * t,), jnp.int32),
                        pltpu.SemaphoreType.DMA((2,)), pltpu.SemaphoreType.DMA((2,))],
        compiler_params=_cparams(("arbitrary",)),
        name="moe_combine",
    )(x2, gates, mod, final_g.reshape(1, d), dest_tiles, ys)


def _moe_layer(x, routed, mod, layer, w_gu, b_gu, w_dn, b_dn, final_g, final, t):
    b, s, d = x.shape
    n = b * s
    x2 = x.reshape(n, d)
    tiles_per_batch = s // t
    h, gates, meta, cnt = routed
    counts = cnt[0, :N_EXPERTS].astype(jnp.int32)
    padded = (counts + EXPERT_ROWS - 1) // EXPERT_ROWS * EXPERT_ROWS
    pend = jnp.cumsum(padded)
    pstart = pend - padded
    n_blocks = -(-(n * TOP_K) // EXPERT_ROWS) + N_EXPERTS
    p_rows = n_blocks * EXPERT_ROWS
    rank = meta[:, :TOP_K]
    eidx = meta[:, TOP_K:2 * TOP_K]
    experts = jnp.arange(N_EXPERTS, dtype=jnp.int32)
    dest = rank + jnp.sum(jnp.where(eidx[..., None] == experts, pstart, 0), axis=-1)
    dest_tiles = dest.reshape(n // t, t, TOP_K).transpose(0, 2, 1).reshape(n // t, TOP_K * t)
    block_row0 = jnp.arange(n_blocks, dtype=jnp.int32) * EXPERT_ROWS
    block_e = jnp.minimum(jnp.sum((pend[None, :] <= block_row0[:, None]).astype(jnp.int32), axis=1),
                          N_EXPERTS - 1)
    n_used = (pend[-1:] // EXPERT_ROWS).astype(jnp.int32)
    xs = _dispatch(h, dest_tiles, pstart + counts, padded - counts, p_rows, t)
    ys = _experts(xs, block_e, n_used, layer, w_gu, b_gu, w_dn, b_dn)
    out = _combine(x2, gates, mod, final_g, dest_tiles, ys, t, tiles_per_batch, final)
    return out.reshape(b, s, d)


def _kv_kernel(x_ref, mod_ref, ng_ref, wkv_ref, wf_ref, bf_ref, kg_ref, hsum_ref,
               kt_ref, v_ref, qf_ref, fb_ref, carry):
    t, d = x_ref.shape[1], x_ref.shape[2]
    n_pairs = N_HEADS // 2

    @pl.when(pl.program_id(1) == 0)
    def _():
        carry[...] = jnp.zeros(carry.shape, F32)

    mod = mod_ref[0]
    h = _rms(x_ref[0], ng_ref[...]) * (1.0 + mod[1:2, :]) + mod[0:1, :]
    h_hi = h.astype(BF16)
    kv = jnp.dot(h_hi, wkv_ref[...], preferred_element_type=F32)
    k = kv[:, :d]
    v_ref[0] = kv[:, d:].astype(BF16)
    ms = jnp.dot((k * k).astype(BF16), hsum_ref[...], preferred_element_type=F32)
    k = k * lax.rsqrt(ms + EPS) * kg_ref[...]
    kt = k.T
    for hp in range(n_pairs):
        kt_ref[0, hp, 0, pl.ds(0, LANES), :] = kt[hp * LANES:(hp + 1) * LANES, :].astype(BF16)

    h_lo = (h - h_hi.astype(F32)).astype(BF16)
    fz2 = jnp.dot(h_hi, wf_ref[...], preferred_element_type=F32)
    fz = (fz2[:, :LANES] + fz2[:, LANES:] + bf_ref[...]
          + jnp.dot(h_lo, wf_ref[:, pl.ds(0, LANES)], preferred_element_type=F32))
    ls = jax.nn.log_sigmoid(fz)
    r_i = lax.broadcasted_iota(jnp.int32, (t, t), 0)
    c_i = lax.broadcasted_iota(jnp.int32, (t, t), 1)
    tri = (c_i <= r_i).astype(BF16)
    ls_hi, ls_mid, ls_lo = _split3(ls)
    cum2 = jnp.dot(tri, jnp.concatenate([ls_hi, ls_mid], axis=1).astype(BF16), preferred_element_type=F32)
    cum = (cum2[:, :LANES] + cum2[:, LANES:] + carry[0:1, :]
           + jnp.dot(tri, ls_lo.astype(BF16), preferred_element_type=F32))
    carry[...] = jnp.broadcast_to(cum[t - 1:t, :], carry.shape)
    f2 = cum * LOG2E
    row8 = lax.broadcasted_iota(jnp.int32, (8, LANES), 0)
    fb_ref[0, 0] = jnp.where(row8 == 0, f2[0:1, :], jnp.where(row8 == 1, f2[t - 1:t, :], 0.0))
    f2t = f2.T
    lane = lax.broadcasted_iota(jnp.int32, (t, LANES), 1)
    sub = lax.broadcasted_iota(jnp.int32, (LANES, t), 0)
    q_pieces = _split3(f2)
    k_pieces = _split3(-f2t)
    for hp in range(n_pairs):
        q_aug = jnp.zeros((t, LANES), F32)
        k_aug = jnp.zeros((LANES, t), F32)
        for hh in range(2):
            head = 2 * hp + hh
            o = hh * 2 * N_FPIECES
            for p in range(N_FPIECES):
                q_aug = jnp.where(lane == o + p, q_pieces[p][:, head:head + 1], q_aug)
                q_aug = jnp.where(lane == o + N_FPIECES + p, 1.0, q_aug)
                k_aug = jnp.where(sub == o + p, 1.0, k_aug)
                k_aug = jnp.where(sub == o + N_FPIECES + p, k_pieces[p][head:head + 1, :], k_aug)
        qf_ref[0, hp] = q_aug.astype(BF16)
        kt_ref[0, hp, 0, pl.ds(LANES, LANES), :] = k_aug.astype(BF16)


def _shared_kv(x, kvmod, ng, w_kvf, b_f, k_norm_g, t):
    b, s, d = x.shape
    n_pairs = N_HEADS // 2
    wkv = w_kvf[:, :2 * d].astype(BF16)
    wf = jnp.zeros((d, LANES), F32).at[:, :N_HEADS].set(w_kvf[:, 2 * d:])
    wf_hi = wf.astype(BF16)
    wf2 = jnp.concatenate([wf_hi, (wf - wf_hi.astype(F32)).astype(BF16)], axis=1)
    bf = jnp.zeros((1, LANES), F32).at[0, :N_HEADS].set(b_f)
    kg = jnp.tile(k_norm_g, N_HEADS).reshape(1, d)
    head_of = jnp.arange(d) // HEAD_DIM
    hsum = ((head_of[:, None] == head_of[None, :]).astype(F32) / HEAD_DIM).astype(BF16)
    const = lambda shape: pl.BlockSpec(shape, lambda bi, si: (0,) * len(shape))
    return pl.pallas_call(
        _kv_kernel,
        out_shape=(
            jax.ShapeDtypeStruct((b, n_pairs, s // t, 2 * LANES, t), BF16),
            jax.ShapeDtypeStruct((b, s, d), BF16),
            jax.ShapeDtypeStruct((b, n_pairs, s, LANES), BF16),
            jax.ShapeDtypeStruct((b, s // t, 8, LANES), F32),
        ),
        grid=(b, s // t),
        in_specs=[
            pl.BlockSpec((1, t, d), lambda bi, si: (bi, si, 0)),
            pl.BlockSpec((1, 2, d), lambda bi, si: (bi, 0, 0)),
            const((1, d)), const((d, 2 * d)), const((d, 2 * LANES)), const((1, LANES)), const((1, d)),
            const((d, d)),
        ],
        out_specs=(
            pl.BlockSpec((1, n_pairs, 1, 2 * LANES, t), lambda bi, si: (bi, 0, si, 0, 0)),
            pl.BlockSpec((1, t, d), lambda bi, si: (bi, si, 0)),
            pl.BlockSpec((1, n_pairs, t, LANES), lambda bi, si: (bi, 0, si, 0)),
            pl.BlockSpec((1, 1, 8, LANES), lambda bi, si: (bi, si, 0, 0)),
        ),
        scratch_shapes=[pltpu.VMEM((8, LANES), F32)],
        compiler_params=_cparams(("arbitrary", "arbitrary")),
        name="shared_kv",
    )(x, kvmod, ng.reshape(1, d), wkv, wf2, bf, kg, hsum), hsum


def _qg_kernel(x_ref, mod_ref, ng_ref, w_ref, qg_ref, hsum_ref, q_ref, g_ref):
    d = x_ref.shape[2]
    mod = mod_ref[0]
    h = _rms(x_ref[0], ng_ref[...]) * (1.0 + mod[1:2, :]) + mod[0:1, :]
    qg = jnp.dot(h.astype(BF16), w_ref[...], preferred_element_type=F32)
    q = qg[:, :d]
    ms = jnp.dot((q * q).astype(BF16), hsum_ref[...], preferred_element_type=F32)
    q = q * lax.rsqrt(ms + EPS) * qg_ref[...] * (LOG2E / math.sqrt(HEAD_DIM))
    q_ref[0] = q.astype(BF16)
    g_ref[0] = jax.nn.sigmoid(qg[:, d:]).astype(BF16)


def _qg(x, mod, ng, w_qg, q_norm_g, hsum, t):
    b, s, d = x.shape
    const = lambda shape: pl.BlockSpec(shape, lambda bi, si: (0,) * len(shape))
    tile = pl.BlockSpec((1, t, d), lambda bi, si: (bi, si, 0))
    return pl.pallas_call(
        _qg_kernel,
        out_shape=(jax.ShapeDtypeStruct((b, s, d), BF16), jax.ShapeDtypeStruct((b, s, d), BF16)),
        grid=(b, s // t),
        in_specs=[tile, pl.BlockSpec((1, 6, d), lambda bi, si: (bi, 0, 0)),
                  const((1, d)), const((d, 2 * d)), const((1, d)), const((d, d))],
        out_specs=(tile, tile),
        compiler_params=_cparams(("arbitrary", "arbitrary")),
        name="attn_qg",
    )(x, mod, ng.reshape(1, d), w_qg.astype(BF16), jnp.tile(q_norm_g, N_HEADS).reshape(1, d), hsum)


def _attn_kernel(j0_ref, q_ref, qf_ref, kt_ref, v_ref, o_ref, qa_scr, m_scr, l_scr, acc_scr, *, online):
    tq = q_ref.shape[1]
    tk = kt_ref.shape[4]
    i = pl.program_id(2)
    n_q = pl.num_programs(2)
    head0 = pl.program_id(0) * N_HEADS + 2 * pl.program_id(1)
    first = [j0_ref[(head0 + hh) * n_q + i] for hh in range(2)]
    first_both = jnp.maximum(first[0], first[1])
    lane = lax.broadcasted_iota(jnp.int32, (tq, LANES), 1)
    q2 = q_ref[0]
    qf = qf_ref[0, 0]
    zero = jnp.zeros((), BF16)
    n_aug = 2 * N_FPIECES
    qa_scr[0, :, pl.ds(0, LANES)] = jnp.where(lane < HEAD_DIM, q2, zero)
    qa_scr[0, :, pl.ds(LANES, LANES)] = jnp.where(lane < n_aug, qf, zero)
    qa_scr[1, :, pl.ds(0, LANES)] = jnp.where(lane >= HEAD_DIM, q2, zero)
    qa_scr[1, :, pl.ds(LANES, LANES)] = jnp.where(jnp.logical_and(lane >= n_aug, lane < 2 * n_aug), qf, zero)
    if online:
        m_scr[...] = jnp.full(m_scr.shape, NEG_BIG, F32)
    l_scr[...] = jnp.zeros(l_scr.shape, F32)
    acc_scr[...] = jnp.zeros(acc_scr.shape, F32)

    def scores(hh, j, masked):
        s = jnp.dot(qa_scr[hh], kt_ref[0, 0, j], preferred_element_type=F32)
        if masked:
            r_i = lax.broadcasted_iota(jnp.int32, (tq, tk), 0)
            c_i = lax.broadcasted_iota(jnp.int32, (tq, tk), 1)
            s = jnp.where(c_i <= r_i, s, NEG_BIG)
        return s

    def values(j):
        return v_ref[0, pl.ds(pl.multiple_of(j * tk, tk), tk), :]

    def tiles(js, masked, heads):
        for hh in heads:
            if online:
                for j in js:
                    s = scores(hh, j, masked)
                    m_prev = m_scr[hh]
                    m_new = jnp.maximum(m_prev, jnp.max(s, axis=-1, keepdims=True))
                    alpha = jnp.exp2(m_prev - m_new)
                    p = jnp.exp2(s - m_new[:, 0:1])
                    l_scr[hh] = alpha * l_scr[hh] + jnp.sum(p, axis=-1, keepdims=True)
                    acc_scr[hh] = alpha * acc_scr[hh] + jnp.dot(p.astype(BF16), values(j),
                                                                preferred_element_type=F32)
                    m_scr[hh] = m_new
            else:
                part, pv = l_scr[hh], acc_scr[hh]
                for j in js:
                    p = jnp.exp2(scores(hh, j, masked))
                    for c in range(tk // LANES):
                        part = part + p[:, c * LANES:(c + 1) * LANES]
                    pv = pv + jnp.dot(p.astype(BF16), values(j), preferred_element_type=F32)
                l_scr[hh], acc_scr[hh] = part, pv

    def one_tile(heads):
        def body(j, carry):
            tiles((j,), False, heads)
            return carry
        return body

    def tile_group(group, carry):
        j = first_both + KV_UNROLL * group
        tiles(tuple(j + u for u in range(KV_UNROLL)), False, (0, 1))
        return carry

    for hh in range(2):
        n_pairs_hh = (first_both - first[hh]) // 2

        def tile_pair(pair, carry, hh=hh):
            j = first[hh] + 2 * pair
            tiles((j, j + 1), False, (hh,))
            return carry

        lax.fori_loop(0, n_pairs_hh, tile_pair, 0)
        lax.fori_loop(first[hh] + 2 * n_pairs_hh, first_both, one_tile((hh,)), 0)
    n_groups = (i - first_both) // KV_UNROLL
    lax.fori_loop(0, n_groups, tile_group, 0)
    lax.fori_loop(first_both + KV_UNROLL * n_groups, i, one_tile((0, 1)), 0)
    tiles((i,), True, (0, 1))
    if online:
        l0, l1 = l_scr[0], l_scr[1]
    else:
        l0 = jnp.sum(l_scr[0], axis=-1, keepdims=True)
        l1 = jnp.sum(l_scr[1], axis=-1, keepdims=True)
    o_ref[0] = jnp.where(lane < HEAD_DIM, acc_scr[0] / l0, acc_scr[1] / l1).astype(BF16)


def _attention(j0, q, qf, kt, v, t, online):
    b, s, d = q.shape
    n_pairs = N_HEADS // 2
    nkv = s // t
    return pl.pallas_call(
        functools.partial(_attn_kernel, online=online),
        out_shape=jax.ShapeDtypeStruct((b, s, d), BF16),
        grid_spec=pltpu.PrefetchScalarGridSpec(
            num_scalar_prefetch=1,
            grid=(b, n_pairs, s // t),
            in_specs=[
                pl.BlockSpec((1, t, LANES), lambda bi, hp, i, j0r: (bi, i, hp)),
                pl.BlockSpec((1, 1, t, LANES), lambda bi, hp, i, j0r: (bi, hp, i, 0)),
                pl.BlockSpec((1, 1, nkv, 2 * LANES, t), lambda bi, hp, i, j0r: (bi, hp, 0, 0, 0)),
                pl.BlockSpec((1, s, LANES), lambda bi, hp, i, j0r: (bi, 0, hp)),
            ],
            out_specs=pl.BlockSpec((1, t, LANES), lambda bi, hp, i, j0r: (bi, i, hp)),
            scratch_shapes=[
                pltpu.VMEM((2, t, 2 * LANES), BF16),
                pltpu.VMEM((2, t, LANES), F32),
                pltpu.VMEM((2, t, LANES), F32),
                pltpu.VMEM((2, t, LANES), F32),
            ],
        ),
        compiler_params=_cparams(("arbitrary", "arbitrary", "arbitrary")),
        name="fox_attention_online" if online else "fox_attention",
    )(j0, q, qf, kt, v)


def _attn_out_kernel(x_ref, o_ref, g_ref, mod_ref, w_ref, ng2_ref, rw_ref, rb_ref,
                     out_ref, h_ref, gate_ref, meta_ref, cnt_ref):
    mod = mod_ref[0]
    og = o_ref[0] * g_ref[0]
    y = jnp.dot(og, w_ref[...], preferred_element_type=F32)
    x_new = x_ref[0] + mod[2:3, :] * y
    out_ref[0] = x_new
    _route_tail(x_new, mod, ng2_ref, rw_ref, rb_ref, h_ref, gate_ref, meta_ref, cnt_ref)


def _attn_out(x, o, g, mod, w_o, route, t):
    b, s, d = x.shape
    tile = pl.BlockSpec((1, t, d), lambda bi, si: (bi, si, 0))
    r_ops, r_in, r_shapes, r_out = _route_plumbing(*route, b, s, d, t)
    outs = pl.pallas_call(
        _attn_out_kernel,
        out_shape=(jax.ShapeDtypeStruct((b, s, d), F32),) + r_shapes,
        grid=(b, s // t),
        in_specs=[tile, tile, tile, pl.BlockSpec((1, 6, d), lambda bi, si: (bi, 0, 0)),
                  pl.BlockSpec((d, d), lambda bi, si: (0, 0))] + r_in,
        out_specs=(tile,) + r_out,
        compiler_params=_cparams(("arbitrary", "arbitrary")),
        name="attn_out",
    )(x, o, g, mod, w_o.astype(BF16), *r_ops)
    return outs[0], outs[1:]


def _fox_layer(x, mod, ng, w_qg, q_norm_g, k_norm_g, w_o, kv, route, t):
    (kt, v, qf, fb), hsum = kv
    n_t = x.shape[1] // t
    q, g = _qg(x, mod, ng, w_qg, q_norm_g, hsum, t)
    bound = (HEAD_DIM * jnp.max(jnp.abs(q_norm_g)) * jnp.max(jnp.abs(k_norm_g))
             * (LOG2E / math.sqrt(HEAD_DIM)))
    f_first, f_last = fb[:, :, 0, :N_HEADS], fb[:, :, 1, :N_HEADS]
    best = (bound * BOUND_SLACK + f_first[:, :, None, :]) - f_last[:, None, :, :]
    before = jnp.arange(n_t)[None, :] < jnp.arange(n_t)[:, None]
    dead = jnp.logical_and(best < ZERO_WEIGHT_EXPONENT, before[None, :, :, None])
    j0 = jnp.sum(dead.astype(jnp.int32), axis=2).transpose(0, 2, 1).reshape(-1)
    o = lax.cond(bound <= DIRECT_EXP_LIMIT,
                 functools.partial(_attention, t=t, online=False),
                 lambda j0_, *rest: _attention(jnp.zeros_like(j0_), *rest, t=t, online=True),
                 j0, q, qf, kt, v)
    return _attn_out(x, o, g, mod, w_o, route, t)


def kernel(x, c, mod_w, mod_b, norm1_g, norm2_g, conv_w_pw1, conv_b_pw1, conv_w_dw, conv_b_dw, conv_ln_g, conv_ln_b, conv_w_pw2, conv_b_pw2, kv_mod_w, kv_mod_b, kv_norm_g, w_kvf, b_f, k_norm_g, attn_w_qg, q_norm_g, attn_w_o, moe_router_w, moe_router_b, moe_w_gu, moe_b_gu, moe_w_down, moe_b_down, final_norm_g):
    b, s, d = x.shape
    depth = mod_w.shape[0]
    n_a = conv_w_pw1.shape[0]
    t = min(512, s)
    c8 = jnp.zeros((8, d), F32).at[:b].set(c)
    mods = _mods(c8, mod_w, mod_b)[:, :b].reshape(depth, b, 6, d)
    kvmod = _mods(c8, kv_mod_w[None], kv_mod_b[None])[0, :b].reshape(b, 2, d)
    kv = None
    for l in range(depth):
        route = (norm2_g[l], moe_router_w[l], moe_router_b[l])
        if l < n_a:
            x, routed = _conv_layer(x, mods[l], norm1_g[l], conv_w_pw1[l], conv_b_pw1[l], conv_w_dw[l],
                                    conv_b_dw[l], conv_ln_g[l], conv_ln_b[l], conv_w_pw2[l],
                                    conv_b_pw2[l], route, t)
        else:
            lb = l - n_a
            x, routed = _fox_layer(x, mods[l], norm1_g[l], attn_w_qg[lb], q_norm_g[lb], k_norm_g,
                                   attn_w_o[lb], kv, route, t)
        x = _moe_layer(x, routed, mods[l], l, moe_w_gu, moe_b_gu, moe_w_down, moe_b_down,
                       final_norm_g, l == depth - 1, t)
        if l == n_a - 1:
            kv = _shared_kv(x, kvmod, kv_norm_g, w_kvf, b_f, k_norm_g, t)
    return x
```

```python
import functools
import math

import jax
import jax.numpy as jnp
from jax import lax
from jax.experimental import pallas as pl
from jax.experimental.pallas import tpu as pltpu

N_HEADS = 16
HEAD_DIM = 64
CONV_WIDTH = 31
N_EXPERTS = 32
TOP_K = 4
SWIGLU_ALPHA = 1.702
SWIGLU_LIMIT = 7.0
EPS = 1e-6

LANES = 128
HALO = 32
CONV_ROWS = 16
ISSUE_UNROLL = 4
KV_UNROLL = 4
EXPERT_ROWS = 512
VMEM_LIMIT = 56 * 1024 * 1024
LOG2E = 1.4426950408889634
NEG_BIG = -1e30
N_FPIECES = 3
DIRECT_EXP_LIMIT = 60.0
ZERO_WEIGHT_EXPONENT = -160.0
BOUND_SLACK = 1.05

F32 = jnp.float32
BF16 = jnp.bfloat16
HIGHEST = lax.Precision.HIGHEST


def _cparams(sem):
    return pltpu.CompilerParams(dimension_semantics=sem, vmem_limit_bytes=VMEM_LIMIT)


def _rms(x, g):
    return x * lax.rsqrt(jnp.mean(x * x, axis=-1, keepdims=True) + EPS) * g


def _split3(f):
    hi = f.astype(BF16).astype(F32)
    r1 = f - hi
    mid = r1.astype(BF16).astype(F32)
    lo = (r1 - mid).astype(BF16).astype(F32)
    return hi, mid, lo


def _mods_kernel(c_ref, w_ref, b_ref, o_ref):
    c = c_ref[...]
    ca = c * jax.nn.sigmoid(c)
    o_ref[0] = jnp.dot(ca, w_ref[0], precision=HIGHEST, preferred_element_type=F32) + b_ref[0]


def _mods(c8, w, b):
    n_l, d, m = w.shape
    tn = min(m, 1024)
    return pl.pallas_call(
        _mods_kernel,
        out_shape=jax.ShapeDtypeStruct((n_l, 8, m), F32),
        grid=(n_l, m // tn),
        in_specs=[
            pl.BlockSpec((8, d), lambda l, j: (0, 0)),
            pl.BlockSpec((1, d, tn), lambda l, j: (l, 0, j)),
            pl.BlockSpec((1, 1, tn), lambda l, j: (l, 0, j)),
        ],
        out_specs=pl.BlockSpec((1, 8, tn), lambda l, j: (l, 0, j)),
        compiler_params=_cparams(("arbitrary", "arbitrary")),
        name="mods",
    )(c8, w, b.reshape(n_l, 1, m))


def _conv_kernel(x_ref, mod_ref, ng_ref, w1_ref, b1_ref, wdw_ref, bdw_ref, lng_ref, lnb_ref,
                 w2_ref, b2_ref, ng2_ref, rw_ref, rb_ref,
                 o_ref, h_ref, gate_ref, meta_ref, cnt_ref, ubuf, cbuf):
    t, d = x_ref.shape[1], x_ref.shape[2]
    slab = ubuf.shape[1:]

    @pl.when(pl.program_id(1) == 0)
    def _():
        ubuf[pl.ds(0, HALO)] = jnp.zeros((HALO,) + slab, F32)

    x = x_ref[0]
    mod = mod_ref[0]
    h = _rms(x, ng_ref[...]) * (1.0 + mod[1:2, :]) + mod[0:1, :]
    u = jnp.dot(h.astype(BF16), w1_ref[...], preferred_element_type=F32) + b1_ref[...]
    u = u[:, :d] * jax.nn.sigmoid(u[:, d:])
    ubuf[pl.ds(HALO, t)] = u.reshape((t,) + slab)

    def conv_rows(c, carry):
        base = pl.multiple_of(c * CONV_ROWS, CONV_ROWS)
        acc = jnp.zeros((CONV_ROWS,) + slab, F32)
        for j in range(CONV_WIDTH):
            acc = acc + ubuf[pl.ds(base + (HALO - (CONV_WIDTH - 1) + j), CONV_ROWS)] * wdw_ref[j]
        cbuf[pl.ds(base, CONV_ROWS)] = acc
        return carry

    lax.fori_loop(0, t // CONV_ROWS, conv_rows, 0)
    ubuf[pl.ds(0, HALO)] = ubuf[pl.ds(t, HALO)]
    acc = cbuf[...].reshape(t, d) + bdw_ref[...]
    mu = jnp.mean(acc, axis=-1, keepdims=True)
    cen = acc - mu
    var = jnp.mean(cen * cen, axis=-1, keepdims=True)
    y = cen * lax.rsqrt(var + EPS) * lng_ref[...] + lnb_ref[...]
    y = y * jax.nn.sigmoid(y)
    y = jnp.dot(y.astype(BF16), w2_ref[...], preferred_element_type=F32) + b2_ref[...]
    x_new = x + mod[2:3, :] * y
    o_ref[0] = x_new
    _route_tail(x_new, mod, ng2_ref, rw_ref, rb_ref, h_ref, gate_ref, meta_ref, cnt_ref)


def _conv_layer(x, mod, ng, w1, b1, wdw, bdw, lng, lnb, w2, b2, route, t):
    b, s, d = x.shape
    row = lambda a: a.reshape(1, -1)
    slab = (d // LANES, LANES)
    wdw_p = jnp.zeros((HALO, d), F32).at[:CONV_WIDTH].set(wdw).reshape((HALO,) + slab)
    const = lambda shape: pl.BlockSpec(shape, lambda bi, si: (0,) * len(shape))
    r_ops, r_in, r_shapes, r_out = _route_plumbing(*route, b, s, d, t)
    outs = pl.pallas_call(
        _conv_kernel,
        out_shape=(jax.ShapeDtypeStruct((b, s, d), F32),) + r_shapes,
        grid=(b, s // t),
        in_specs=[
            pl.BlockSpec((1, t, d), lambda bi, si: (bi, si, 0)),
            pl.BlockSpec((1, 6, d), lambda bi, si: (bi, 0, 0)),
            const((1, d)), const((d, 2 * d)), const((1, 2 * d)), const((HALO,) + slab), const((1, d)),
            const((1, d)), const((1, d)), const((d, d)), const((1, d)),
        ] + r_in,
        out_specs=(pl.BlockSpec((1, t, d), lambda bi, si: (bi, si, 0)),) + r_out,
        scratch_shapes=[pltpu.VMEM((t + HALO,) + slab, F32), pltpu.VMEM((t,) + slab, F32)],
        compiler_params=_cparams(("arbitrary", "arbitrary")),
        name="conv_layer",
    )(x, mod, row(ng), w1.astype(BF16), row(b1), wdw_p, row(bdw), row(lng), row(lnb),
      w2.astype(BF16), row(b2), *r_ops)
    return outs[0], outs[1:]


def _route_tail(x, mod, ng_ref, rw_ref, rb_ref, h_ref, gate_ref, meta_ref, cnt_ref):
    t = x.shape[0]

    @pl.when(jnp.logical_and(pl.program_id(0) == 0, pl.program_id(1) == 0))
    def _():
        cnt_ref[...] = jnp.zeros(cnt_ref.shape, F32)

    h = _rms(x, ng_ref[...]) * (1.0 + mod[4:5, :]) + mod[3:4, :]
    h_ref[...] = h.reshape(h_ref.shape)
    h_hi = h.astype(BF16)
    h_lo = (h - h_hi.astype(F32)).astype(BF16)
    lg2 = jnp.dot(h_hi, rw_ref[...], preferred_element_type=F32)
    logits = (lg2[:, :LANES] + lg2[:, LANES:] + rb_ref[...]
              + jnp.dot(h_lo, rw_ref[:, pl.ds(0, LANES)], preferred_element_type=F32))
    lane = lax.broadcasted_iota(jnp.int32, (t, LANES), 1).astype(F32)
    work = logits
    vals, idxs = [], []
    for _ in range(TOP_K):
        m = jnp.max(work, axis=-1, keepdims=True)
        idx = jnp.min(jnp.where(work == m, lane, float(LANES)), axis=-1, keepdims=True)
        vals.append(m)
        idxs.append(idx)
        work = jnp.where(lane == idx, -jnp.inf, work)
    exps = [jnp.exp(v - vals[0]) for v in vals]
    denom = exps[0] + exps[1] + exps[2] + exps[3]
    onehot = jnp.zeros((t, LANES), F32)
    for idx in idxs:
        onehot = onehot + (lane == idx).astype(F32)
    r_i = lax.broadcasted_iota(jnp.int32, (t, t), 0)
    c_i = lax.broadcasted_iota(jnp.int32, (t, t), 1)
    tri = (c_i < r_i).astype(BF16)
    base = jnp.dot(tri, onehot.astype(BF16), preferred_element_type=F32) + cnt_ref[0:1, :]
    gate_out = jnp.zeros((t, LANES), F32)
    meta = jnp.zeros((t, LANES), F32)
    for k in range(TOP_K):
        rank = jnp.sum(jnp.where(lane == idxs[k], base, 0.0), axis=-1, keepdims=True)
        gate_out = jnp.where(lane == k, exps[k] / denom, gate_out)
        meta = jnp.where(lane == k, rank, meta)
        meta = jnp.where(lane == TOP_K + k, idxs[k], meta)
    gate_ref[...] = gate_out
    meta_ref[0] = meta.T[0:2 * TOP_K, :].astype(jnp.int32)
    cnt_ref[...] = cnt_ref[...] + jnp.sum(onehot, axis=0, keepdims=True)


def _route_plumbing(ng, rw, rb, b, s, d, t):
    n = b * s
    tiles_per_batch = s // t
    rw_p = jnp.zeros((d, LANES), F32).at[:, :N_EXPERTS].set(rw)
    rw_hi = rw_p.astype(BF16)
    rw2 = jnp.concatenate([rw_hi, (rw_p - rw_hi.astype(F32)).astype(BF16)], axis=1)
    rb_p = jnp.full((1, LANES), NEG_BIG, F32).at[0, :N_EXPERTS].set(rb)
    const = lambda shape: pl.BlockSpec(shape, lambda bi, si: (0,) * len(shape))
    flat = lambda bi, si: bi * tiles_per_batch + si
    operands = (ng.reshape(1, d), rw2, rb_p)
    in_specs = [const((1, d)), const((d, 2 * LANES)), const((1, LANES))]
    out_shapes = (
        jax.ShapeDtypeStruct((n, d // LANES, LANES), F32),
        jax.ShapeDtypeStruct((n, LANES), F32),
        jax.ShapeDtypeStruct((n // t, 2 * TOP_K, t), jnp.int32),
        jax.ShapeDtypeStruct((8, LANES), F32),
    )
    out_specs = (
        pl.BlockSpec((t, d // LANES, LANES), lambda bi, si: (flat(bi, si), 0, 0)),
        pl.BlockSpec((t, LANES), lambda bi, si: (flat(bi, si), 0)),
        pl.BlockSpec((1, 2 * TOP_K, t), lambda bi, si: (flat(bi, si), 0, 0)),
        const((8, LANES)),
    )
    return operands, in_specs, out_shapes, out_specs


def _dispatch_kernel(pad0_ref, padn_ref, h_ref, dest_hbm, xs_hbm, zrow, idx_smem, sem_idx, sem_rows, sem_pad):
    t = h_ref.shape[0]
    i = pl.program_id(0)

    @pl.when(i == 0)
    def _():
        zrow[...] = jnp.zeros(zrow.shape, F32)
        for start in (True, False):
            def per_expert(e, carry, start=start):
                def per_row(r, c):
                    pad_cp = pltpu.make_async_copy(zrow, xs_hbm.at[pad0_ref[e] + r], sem_pad)
                    if start:
                        pad_cp.start()
                    else:
                        pad_cp.wait()
                    return c
                return lax.fori_loop(0, padn_ref[e], per_row, carry)
            lax.fori_loop(0, N_EXPERTS, per_expert, 0)

    n_steps = pl.num_programs(0)
    n_idx = TOP_K * t

    def idx_copy(step, s):
        return pltpu.make_async_copy(dest_hbm.at[step], idx_smem.at[pl.ds(s * n_idx, n_idx)], sem_idx.at[s])

    @pl.when(i == 0)
    def _():
        idx_copy(0, 0).start()

    def scatter_rows(s):
        idx_copy(i, s).wait()

        @pl.when(i + 1 < n_steps)
        def _():
            idx_copy(i + 1, 1 - s).start()

        def issue(group, carry):
            tok0 = pl.multiple_of(group * ISSUE_UNROLL, ISSUE_UNROLL)
            for u in range(ISSUE_UNROLL):
                for k in range(TOP_K):
                    pltpu.make_async_copy(h_ref.at[tok0 + u], xs_hbm.at[idx_smem[s * n_idx + k * t + tok0 + u]],
                                          sem_rows).start(priority=(u * TOP_K + k) % 2)
            return carry

        lax.fori_loop(0, t // ISSUE_UNROLL, issue, 0)

    for s in range(2):
        @pl.when(i % 2 == s)
        def _(s=s):
            scatter_rows(s)

    for k in range(TOP_K):
        pltpu.make_async_copy(h_ref, xs_hbm.at[pl.ds(0, t)], sem_rows).wait()


def _dispatch(h, dest_tiles, pad_start, pad_n, p_rows, t):
    n, sub, _ = h.shape
    return pl.pallas_call(
        _dispatch_kernel,
        out_shape=jax.ShapeDtypeStruct((p_rows, sub, LANES), F32),
        grid_spec=pltpu.PrefetchScalarGridSpec(
            num_scalar_prefetch=2,
            grid=(n // t,),
            in_specs=[
                pl.BlockSpec((t, sub, LANES), lambda i, p0, pn: (i, 0, 0)),
                pl.BlockSpec(memory_space=pl.ANY),
            ],
            out_specs=pl.BlockSpec(memory_space=pl.ANY),
            scratch_shapes=[pltpu.VMEM((sub, LANES), F32), pltpu.SMEM((2 * TOP_K * t,), jnp.int32),
                            pltpu.SemaphoreType.DMA((2,)), pltpu.SemaphoreType.DMA, pltpu.SemaphoreType.DMA],
        ),
        compiler_params=_cparams(("arbitrary",)),
        name="moe_dispatch",
    )(pad_start, pad_n, h, dest_tiles)


def _experts_kernel(be_ref, nv_ref, nu_ref, xs_ref, wgu_ref, bgu_ref, wdn_ref, bdn_ref, ys_ref, wgu_bf, wdn_bf):
    b = pl.program_id(0)
    f = wdn_ref.shape[2]
    sub = xs_ref.shape[1]
    half = EXPERT_ROWS // 2
    e = be_ref[b]
    e_prev = be_ref[jnp.maximum(b - 1, 0)]

    @pl.when(jnp.logical_or(b == 0, e != e_prev))
    def _():
        wgu_bf[...] = wgu_ref[0, 0].astype(BF16)
        wdn_bf[...] = wdn_ref[0, 0].astype(BF16)

    def mlp(rows):
        x = xs_ref[pl.ds(0, rows)].reshape(rows, sub * LANES).astype(BF16)
        gu = jnp.dot(x, wgu_bf[...], preferred_element_type=F32) + bgu_ref[0, 0]
        x_glu = jnp.minimum(gu[:, :f], SWIGLU_LIMIT)
        x_lin = jnp.clip(gu[:, f:], -SWIGLU_LIMIT, SWIGLU_LIMIT)
        act = x_glu * jax.nn.sigmoid(SWIGLU_ALPHA * x_glu) * (x_lin + 1.0)
        y = jnp.dot(act.astype(BF16), wdn_bf[...], preferred_element_type=F32) + bdn_ref[0, 0]
        ys_ref[pl.ds(0, rows)] = y.reshape(rows, sub, LANES)

    used = b < nu_ref[0]

    @pl.when(jnp.logical_and(used, nv_ref[b] > half))
    def _():
        mlp(EXPERT_ROWS)

    @pl.when(jnp.logical_and(used, nv_ref[b] <= half))
    def _():
        mlp(half)


def _experts(xs, block_e, n_valid, n_used, layer, w_gu, b_gu, w_dn, b_dn):
    p_rows, sub, _ = xs.shape
    n_l, n_e, d, f2 = w_gu.shape
    f = f2 // 2
    n_blocks = p_rows // EXPERT_ROWS
    row_map = lambda b, be, nv, nu: (jnp.minimum(b, nu[0] - 1), 0, 0)
    exp_map = lambda b, be, nv, nu: (layer, be[b], 0, 0)
    return pl.pallas_call(
        _experts_kernel,
        out_shape=jax.ShapeDtypeStruct((p_rows, sub, LANES), F32),
        grid_spec=pltpu.PrefetchScalarGridSpec(
            num_scalar_prefetch=3,
            grid=(n_blocks,),
            in_specs=[
                pl.BlockSpec((EXPERT_ROWS, sub, LANES), row_map),
                pl.BlockSpec((1, 1, d, f2), exp_map),
                pl.BlockSpec((1, 1, 1, f2), exp_map),
                pl.BlockSpec((1, 1, f, d), exp_map),
                pl.BlockSpec((1, 1, 1, d), exp_map),
            ],
            out_specs=pl.BlockSpec((EXPERT_ROWS, sub, LANES), row_map),
            scratch_shapes=[pltpu.VMEM((d, f2), BF16), pltpu.VMEM((f, d), BF16)],
        ),
        compiler_params=_cparams(("arbitrary",)),
        name="moe_experts",
    )(block_e, n_valid, n_used, xs, w_gu, b_gu.reshape(n_l, n_e, 1, f2), w_dn, b_dn.reshape(n_l, n_e, 1, d))


def _combine_kernel(x_ref, gate_ref, mod_ref, fng_ref, dest_hbm, ys_hbm, o_ref, buf, idx_smem, sem_idx, sem_rows,
                    *, final):
    t = x_ref.shape[0]
    i = pl.program_id(0)
    n_steps = pl.num_programs(0)
    slot = i % 2

    def idx_copy(step, s):
        return pltpu.make_async_copy(dest_hbm.at[step], idx_smem.at[pl.ds(s * TOP_K * t, TOP_K * t)],
                                     sem_idx.at[s])

    def start_rows(s):
        def issue(group, carry):
            tok0 = pl.multiple_of(group * ISSUE_UNROLL, ISSUE_UNROLL)
            for u in range(ISSUE_UNROLL):
                for k in range(TOP_K):
                    pltpu.make_async_copy(ys_hbm.at[idx_smem[(s * TOP_K + k) * t + tok0 + u]],
                                          buf.at[s * TOP_K + k, tok0 + u],
                                          sem_rows.at[s]).start(priority=(u * TOP_K + k) % 2)
            return carry

        lax.fori_loop(0, t // ISSUE_UNROLL, issue, 0)

    @pl.when(i == 0)
    def _():
        first = idx_copy(0, 0)
        first.start()
        first.wait()
        start_rows(0)

    for nxt in range(2):
        @pl.when(jnp.logical_and(i + 1 < n_steps, slot == 1 - nxt))
        def _(nxt=nxt):
            idx_copy(i + 1, nxt).start()

    for k in range(TOP_K):
        pltpu.make_async_copy(ys_hbm.at[pl.ds(0, t)], buf.at[slot * TOP_K + k], sem_rows.at[slot]).wait()

    for nxt in range(2):
        @pl.when(jnp.logical_and(i + 1 < n_steps, slot == 1 - nxt))
        def _(nxt=nxt):
            idx_copy(i + 1, nxt).wait()
            start_rows(nxt)

    gate = gate_ref[...]
    y = gate[:, 0:1] * buf[slot * TOP_K].reshape(x_ref.shape)
    for k in range(1, TOP_K):
        y = y + gate[:, k:k + 1] * buf[slot * TOP_K + k].reshape(x_ref.shape)
    out = x_ref[...] + mod_ref[0][5:6, :] * y
    o_ref[...] = _rms(out, fng_ref[...]) if final else out


def _combine(x2, gates, mod, final_g, dest_tiles, ys, t, tiles_per_batch, final):
    n, d = x2.shape
    return pl.pallas_call(
        functools.partial(_combine_kernel, final=final),
        out_shape=jax.ShapeDtypeStruct((n, d), F32),
        grid=(n // t,),
        in_specs=[
            pl.BlockSpec((t, d), lambda i: (i, 0)),
            pl.BlockSpec((t, LANES), lambda i: (i, 0)),
            pl.BlockSpec((1, 6, d), lambda i: (i // tiles_per_batch, 0, 0)),
            pl.BlockSpec((1, d), lambda i: (0, 0)),
            pl.BlockSpec(memory_space=pl.ANY),
            pl.BlockSpec(memory_space=pl.ANY),
        ],
        out_specs=pl.BlockSpec((t, d), lambda i: (i, 0)),
        scratch_shapes=[pltpu.VMEM((2 * TOP_K, t, d // LANES, LANES), F32), pltpu.SMEM((2 * TOP_K * t,), jnp.int32),
                        pltpu.SemaphoreType.DMA((2,)), pltpu.SemaphoreType.DMA((2,))],
        compiler_params=_cparams(("arbitrary",)),
        name="moe_combine",
    )(x2, gates, mod, final_g.reshape(1, d), dest_tiles, ys)


def _moe_layer(x, routed, mod, layer, w_gu, b_gu, w_dn, b_dn, final_g, final, t):
    b, s, d = x.shape
    n = b * s
    x2 = x.reshape(n, d)
    tiles_per_batch = s // t
    h, gates, meta, cnt = routed
    counts = cnt[0, :N_EXPERTS].astype(jnp.int32)
    padded = (counts + EXPERT_ROWS - 1) // EXPERT_ROWS * EXPERT_ROWS
    pend = jnp.cumsum(padded)
    pstart = pend - padded
    n_blocks = -(-(n * TOP_K) // EXPERT_ROWS) + N_EXPERTS
    p_rows = n_blocks * EXPERT_ROWS
    rank = meta[:, :TOP_K, :]
    eidx = meta[:, TOP_K:, :]
    experts = jnp.arange(N_EXPERTS, dtype=jnp.int32)
    dest = rank + jnp.sum(jnp.where(eidx[..., None] == experts, pstart, 0), axis=-1)
    dest_tiles = dest.reshape(n // t, TOP_K * t)
    block_row0 = jnp.arange(n_blocks, dtype=jnp.int32) * EXPERT_ROWS
    block_e = jnp.minimum(jnp.sum((pend[None, :] <= block_row0[:, None]).astype(jnp.int32), axis=1),
                          N_EXPERTS - 1)
    n_used = (pend[-1:] // EXPERT_ROWS).astype(jnp.int32)
    row_end = jnp.sum(jnp.where(block_e[:, None] == experts, pstart + counts, 0), axis=1)
    n_valid = jnp.clip(row_end - block_row0, 0, EXPERT_ROWS).astype(jnp.int32)
    half = EXPERT_ROWS // 2
    pad_n = (counts + half - 1) // half * half - counts
    xs = _dispatch(h, dest_tiles, pstart + counts, pad_n, p_rows, t)
    ys = _experts(xs, block_e, n_valid, n_used, layer, w_gu, b_gu, w_dn, b_dn)
    out = _combine(x2, gates, mod, final_g, dest_tiles, ys, t, tiles_per_batch, final)
    return out.reshape(b, s, d)


def _kv_kernel(x_ref, mod_ref, ng_ref, wkv_ref, wf_ref, bf_ref, kg_ref, hsum_ref,
               kt_ref, v_ref, qf_ref, fb_ref, carry):
    t, d = x_ref.shape[1], x_ref.shape[2]
    n_pairs = N_HEADS // 2

    @pl.when(pl.program_id(1) == 0)
    def _():
        carry[...] = jnp.zeros(carry.shape, F32)

    mod = mod_ref[0]
    h = _rms(x_ref[0], ng_ref[...]) * (1.0 + mod[1:2, :]) + mod[0:1, :]
    h_hi = h.astype(BF16)
    kv = jnp.dot(h_hi, wkv_ref[...], preferred_element_type=F32)
    k = kv[:, :d]
    v_ref[0] = kv[:, d:].astype(BF16)
    ms = jnp.dot((k * k).astype(BF16), hsum_ref[...], preferred_element_type=F32)
    k = k * lax.rsqrt(ms + EPS) * kg_ref[...]
    kt = k.T
    for hp in range(n_pairs):
        kt_ref[0, hp, 0, pl.ds(0, LANES), :] = kt[hp * LANES:(hp + 1) * LANES, :].astype(BF16)

    h_lo = (h - h_hi.astype(F32)).astype(BF16)
    fz2 = jnp.dot(h_hi, wf_ref[...], preferred_element_type=F32)
    fz = (fz2[:, :LANES] + fz2[:, LANES:] + bf_ref[...]
          + jnp.dot(h_lo, wf_ref[:, pl.ds(0, LANES)], preferred_element_type=F32))
    ls = jax.nn.log_sigmoid(fz)
    r_i = lax.broadcasted_iota(jnp.int32, (t, t), 0)
    c_i = lax.broadcasted_iota(jnp.int32, (t, t), 1)
    tri = (c_i <= r_i).astype(BF16)
    ls_hi, ls_mid, ls_lo = _split3(ls)
    cum2 = jnp.dot(tri, jnp.concatenate([ls_hi, ls_mid], axis=1).astype(BF16), preferred_element_type=F32)
    cum = (cum2[:, :LANES] + cum2[:, LANES:] + carry[0:1, :]
           + jnp.dot(tri, ls_lo.astype(BF16), preferred_element_type=F32))
    carry[...] = jnp.broadcast_to(cum[t - 1:t, :], carry.shape)
    f2 = cum * LOG2E
    row8 = lax.broadcasted_iota(jnp.int32, (8, LANES), 0)
    fb_ref[0, 0] = jnp.where(row8 == 0, f2[0:1, :], jnp.where(row8 == 1, f2[t - 1:t, :], 0.0))
    f2t = f2.T
    lane = lax.broadcasted_iota(jnp.int32, (t, LANES), 1)
    sub = lax.broadcasted_iota(jnp.int32, (LANES, t), 0)
    q_pieces = _split3(f2)
    k_pieces = _split3(-f2t)
    for hp in range(n_pairs):
        q_aug = jnp.zeros((t, LANES), F32)
        k_aug = jnp.zeros((LANES, t), F32)
        for hh in range(2):
            head = 2 * hp + hh
            o = hh * 2 * N_FPIECES
            for p in range(N_FPIECES):
                q_aug = jnp.where(lane == o + p, q_pieces[p][:, head:head + 1], q_aug)
                q_aug = jnp.where(lane == o + N_FPIECES + p, 1.0, q_aug)
                k_aug = jnp.where(sub == o + p, 1.0, k_aug)
                k_aug = jnp.where(sub == o + N_FPIECES + p, k_pieces[p][head:head + 1, :], k_aug)
        qf_ref[0, hp] = q_aug.astype(BF16)
        kt_ref[0, hp, 0, pl.ds(LANES, LANES), :] = k_aug.astype(BF16)


def _shared_kv(x, kvmod, ng, w_kvf, b_f, k_norm_g, t):
    b, s, d = x.shape
    n_pairs = N_HEADS // 2
    wkv = w_kvf[:, :2 * d].astype(BF16)
    wf = jnp.zeros((d, LANES), F32).at[:, :N_HEADS].set(w_kvf[:, 2 * d:])
    wf_hi = wf.astype(BF16)
    wf2 = jnp.concatenate([wf_hi, (wf - wf_hi.astype(F32)).astype(BF16)], axis=1)
    bf = jnp.zeros((1, LANES), F32).at[0, :N_HEADS].set(b_f)
    kg = jnp.tile(k_norm_g, N_HEADS).reshape(1, d)
    head_of = jnp.arange(d) // HEAD_DIM
    hsum = ((head_of[:, None] == head_of[None, :]).astype(F32) / HEAD_DIM).astype(BF16)
    const = lambda shape: pl.BlockSpec(shape, lambda bi, si: (0,) * len(shape))
    return pl.pallas_call(
        _kv_kernel,
        out_shape=(
            jax.ShapeDtypeStruct((b, n_pairs, s // t, 2 * LANES, t), BF16),
            jax.ShapeDtypeStruct((b, s, d), BF16),
            jax.ShapeDtypeStruct((b, n_pairs, s, LANES), BF16),
            jax.ShapeDtypeStruct((b, s // t, 8, LANES), F32),
        ),
        grid=(b, s // t),
        in_specs=[
            pl.BlockSpec((1, t, d), lambda bi, si: (bi, si, 0)),
            pl.BlockSpec((1, 2, d), lambda bi, si: (bi, 0, 0)),
            const((1, d)), const((d, 2 * d)), const((d, 2 * LANES)), const((1, LANES)), const((1, d)),
            const((d, d)),
        ],
        out_specs=(
            pl.BlockSpec((1, n_pairs, 1, 2 * LANES, t), lambda bi, si: (bi, 0, si, 0, 0)),
            pl.BlockSpec((1, t, d), lambda bi, si: (bi, si, 0)),
            pl.BlockSpec((1, n_pairs, t, LANES), lambda bi, si: (bi, 0, si, 0)),
            pl.BlockSpec((1, 1, 8, LANES), lambda bi, si: (bi, si, 0, 0)),
        ),
        scratch_shapes=[pltpu.VMEM((8, LANES), F32)],
        compiler_params=_cparams(("arbitrary", "arbitrary")),
        name="shared_kv",
    )(x, kvmod, ng.reshape(1, d), wkv, wf2, bf, kg, hsum), hsum


def _qg_kernel(x_ref, mod_ref, ng_ref, w_ref, qg_ref, hsum_ref, q_ref, g_ref):
    d = x_ref.shape[2]
    mod = mod_ref[0]
    h = _rms(x_ref[0], ng_ref[...]) * (1.0 + mod[1:2, :]) + mod[0:1, :]
    qg = jnp.dot(h.astype(BF16), w_ref[...], preferred_element_type=F32)
    q = qg[:, :d]
    ms = jnp.dot((q * q).astype(BF16), hsum_ref[...], preferred_element_type=F32)
    q = q * lax.rsqrt(ms + EPS) * qg_ref[...] * (LOG2E / math.sqrt(HEAD_DIM))
    q_ref[0] = q.astype(BF16)
    g_ref[0] = jax.nn.sigmoid(qg[:, d:]).astype(BF16)


def _qg(x, mod, ng, w_qg, q_norm_g, hsum, t):
    b, s, d = x.shape
    const = lambda shape: pl.BlockSpec(shape, lambda bi, si: (0,) * len(shape))
    tile = pl.BlockSpec((1, t, d), lambda bi, si: (bi, si, 0))
    return pl.pallas_call(
        _qg_kernel,
        out_shape=(jax.ShapeDtypeStruct((b, s, d), BF16), jax.ShapeDtypeStruct((b, s, d), BF16)),
        grid=(b, s // t),
        in_specs=[tile, pl.BlockSpec((1, 6, d), lambda bi, si: (bi, 0, 0)),
                  const((1, d)), const((d, 2 * d)), const((1, d)), const((d, d))],
        out_specs=(tile, tile),
        compiler_params=_cparams(("arbitrary", "arbitrary")),
        name="attn_qg",
    )(x, mod, ng.reshape(1, d), w_qg.astype(BF16), jnp.tile(q_norm_g, N_HEADS).reshape(1, d), hsum)


def _attn_kernel(j0_ref, q_ref, qf_ref, kt_ref, v_ref, o_ref, qa_scr, m_scr, l_scr, acc_scr, *, online):
    tq = q_ref.shape[1]
    tk = kt_ref.shape[4]
    i = pl.program_id(2)
    n_q = pl.num_programs(2)
    head0 = pl.program_id(0) * N_HEADS + 2 * pl.program_id(1)
    first = [j0_ref[(head0 + hh) * n_q + i] for hh in range(2)]
    first_both = jnp.maximum(first[0], first[1])
    lane = lax.broadcasted_iota(jnp.int32, (tq, LANES), 1)
    q2 = q_ref[0]
    qf = qf_ref[0, 0]
    zero = jnp.zeros((), BF16)
    n_aug = 2 * N_FPIECES
    qa_scr[0, :, pl.ds(0, LANES)] = jnp.where(lane < HEAD_DIM, q2, zero)
    qa_scr[0, :, pl.ds(LANES, LANES)] = jnp.where(lane < n_aug, qf, zero)
    qa_scr[1, :, pl.ds(0, LANES)] = jnp.where(lane >= HEAD_DIM, q2, zero)
    qa_scr[1, :, pl.ds(LANES, LANES)] = jnp.where(jnp.logical_and(lane >= n_aug, lane < 2 * n_aug), qf, zero)
    if online:
        m_scr[...] = jnp.full(m_scr.shape, NEG_BIG, F32)
    l_scr[...] = jnp.zeros(l_scr.shape, F32)
    acc_scr[...] = jnp.zeros(acc_scr.shape, F32)

    def scores(hh, j, masked):
        s = jnp.dot(qa_scr[hh], kt_ref[0, 0, j], preferred_element_type=F32)
        if masked:
            r_i = lax.broadcasted_iota(jnp.int32, (tq, tk), 0)
            c_i = lax.broadcasted_iota(jnp.int32, (tq, tk), 1)
            s = jnp.where(c_i <= r_i, s, NEG_BIG)
        return s

    def values(j):
        return v_ref[0, pl.ds(pl.multiple_of(j * tk, tk), tk), :]

    def tiles(js, masked, heads):
        for hh in heads:
            if online:
                for j in js:
                    s = scores(hh, j, masked)
                    m_prev = m_scr[hh]
                    m_new = jnp.maximum(m_prev, jnp.max(s, axis=-1, keepdims=True))
                    alpha = jnp.exp2(m_prev - m_new)
                    p = jnp.exp2(s - m_new[:, 0:1])
                    l_scr[hh] = alpha * l_scr[hh] + jnp.sum(p, axis=-1, keepdims=True)
                    acc_scr[hh] = alpha * acc_scr[hh] + jnp.dot(p.astype(BF16), values(j),
                                                                preferred_element_type=F32)
                    m_scr[hh] = m_new
            else:
                part, pv = l_scr[hh], acc_scr[hh]
                for j in js:
                    p = jnp.exp2(scores(hh, j, masked))
                    for c in range(tk // LANES):
                        part = part + p[:, c * LANES:(c + 1) * LANES]
                    pv = pv + jnp.dot(p.astype(BF16), values(j), preferred_element_type=F32)
                l_scr[hh], acc_scr[hh] = part, pv

    def one_tile(heads):
        def body(j, carry):
            tiles((j,), False, heads)
            return carry
        return body

    def tile_group(group, carry):
        j = first_both + KV_UNROLL * group
        tiles(tuple(j + u for u in range(KV_UNROLL)), False, (0, 1))
        return carry

    for hh in range(2):
        n_pairs_hh = (first_both - first[hh]) // 2

        def tile_pair(pair, carry, hh=hh):
            j = first[hh] + 2 * pair
            tiles((j, j + 1), False, (hh,))
            return carry

        lax.fori_loop(0, n_pairs_hh, tile_pair, 0)
        lax.fori_loop(first[hh] + 2 * n_pairs_hh, first_both, one_tile((hh,)), 0)
    n_groups = (i - first_both) // KV_UNROLL
    lax.fori_loop(0, n_groups, tile_group, 0)
    lax.fori_loop(first_both + KV_UNROLL * n_groups, i, one_tile((0, 1)), 0)
    tiles((i,), True, (0, 1))
    if online:
        l0, l1 = l_scr[0], l_scr[1]
    else:
        l0 = jnp.sum(l_scr[0], axis=-1, keepdims=True)
        l1 = jnp.sum(l_scr[1], axis=-1, keepdims=True)
    o_ref[0] = jnp.where(lane < HEAD_DIM, acc_scr[0] / l0, acc_scr[1] / l1).astype(BF16)


def _attention(j0, q, qf, kt, v, t, online):
    b, s, d = q.shape
    n_pairs = N_HEADS // 2
    nkv = s // t
    return pl.pallas_call(
        functools.partial(_attn_kernel, online=online),
        out_shape=jax.ShapeDtypeStruct((b, s, d), BF16),
        grid_spec=pltpu.PrefetchScalarGridSpec(
            num_scalar_prefetch=1,
            grid=(b, n_pairs, s // t),
            in_specs=[
                pl.BlockSpec((1, t, LANES), lambda bi, hp, i, j0r: (bi, i, hp)),
                pl.BlockSpec((1, 1, t, LANES), lambda bi, hp, i, j0r: (bi, hp, i, 0)),
                pl.BlockSpec((1, 1, nkv, 2 * LANES, t), lambda bi, hp, i, j0r: (bi, hp, 0, 0, 0)),
                pl.BlockSpec((1, s, LANES), lambda bi, hp, i, j0r: (bi, 0, hp)),
            ],
            out_specs=pl.BlockSpec((1, t, LANES), lambda bi, hp, i, j0r: (bi, i, hp)),
            scratch_shapes=[
                pltpu.VMEM((2, t, 2 * LANES), BF16),
                pltpu.VMEM((2, t, LANES), F32),
                pltpu.VMEM((2, t, LANES), F32),
                pltpu.VMEM((2, t, LANES), F32),
            ],
        ),
        compiler_params=_cparams(("arbitrary", "arbitrary", "arbitrary")),
        name="fox_attention_online" if online else "fox_attention",
    )(j0, q, qf, kt, v)


def _attn_out_kernel(x_ref, o_ref, g_ref, mod_ref, w_ref, ng2_ref, rw_ref, rb_ref,
                     out_ref, h_ref, gate_ref, meta_ref, cnt_ref):
    mod = mod_ref[0]
    og = o_ref[0] * g_ref[0]
    y = jnp.dot(og, w_ref[...], preferred_element_type=F32)
    x_new = x_ref[0] + mod[2:3, :] * y
    out_ref[0] = x_new
    _route_tail(x_new, mod, ng2_ref, rw_ref, rb_ref, h_ref, gate_ref, meta_ref, cnt_ref)


def _attn_out(x, o, g, mod, w_o, route, t):
    b, s, d = x.shape
    tile = pl.BlockSpec((1, t, d), lambda bi, si: (bi, si, 0))
    r_ops, r_in, r_shapes, r_out = _route_plumbing(*route, b, s, d, t)
    outs = pl.pallas_call(
        _attn_out_kernel,
        out_shape=(jax.ShapeDtypeStruct((b, s, d), F32),) + r_shapes,
        grid=(b, s // t),
        in_specs=[tile, tile, tile, pl.BlockSpec((1, 6, d), lambda bi, si: (bi, 0, 0)),
                  pl.BlockSpec((d, d), lambda bi, si: (0, 0))] + r_in,
        out_specs=(tile,) + r_out,
        compiler_params=_cparams(("arbitrary", "arbitrary")),
        name="attn_out",
    )(x, o, g, mod, w_o.astype(BF16), *r_ops)
    return outs[0], outs[1:]


def _fox_layer(x, mod, ng, w_qg, q_norm_g, k_norm_g, w_o, kv, route, t):
    (kt, v, qf, fb), hsum = kv
    n_t = x.shape[1] // t
    q, g = _qg(x, mod, ng, w_qg, q_norm_g, hsum, t)
    bound = (HEAD_DIM * jnp.max(jnp.abs(q_norm_g)) * jnp.max(jnp.abs(k_norm_g))
             * (LOG2E / math.sqrt(HEAD_DIM)))
    f_first, f_last = fb[:, :, 0, :N_HEADS], fb[:, :, 1, :N_HEADS]
    best = (bound * BOUND_SLACK + f_first[:, :, None, :]) - f_last[:, None, :, :]
    before = jnp.arange(n_t)[None, :] < jnp.arange(n_t)[:, None]
    dead = jnp.logical_and(best < ZERO_WEIGHT_EXPONENT, before[None, :, :, None])
    j0 = jnp.sum(dead.astype(jnp.int32), axis=2).transpose(0, 2, 1).reshape(-1)
    o = lax.cond(bound <= DIRECT_EXP_LIMIT,
                 functools.partial(_attention, t=t, online=False),
                 lambda j0_, *rest: _attention(jnp.zeros_like(j0_), *rest, t=t, online=True),
                 j0, q, qf, kt, v)
    return _attn_out(x, o, g, mod, w_o, route, t)


def kernel(x, c, mod_w, mod_b, norm1_g, norm2_g, conv_w_pw1, conv_b_pw1, conv_w_dw, conv_b_dw, conv_ln_g, conv_ln_b, conv_w_pw2, conv_b_pw2, kv_mod_w, kv_mod_b, kv_norm_g, w_kvf, b_f, k_norm_g, attn_w_qg, q_norm_g, attn_w_o, moe_router_w, moe_router_b, moe_w_gu, moe_b_gu, moe_w_down, moe_b_down, final_norm_g):
    b, s, d = x.shape
    depth = mod_w.shape[0]
    n_a = conv_w_pw1.shape[0]
    t = min(512, s)
    c8 = jnp.zeros((8, d), F32).at[:b].set(c)
    mods = _mods(c8, mod_w, mod_b)[:, :b].reshape(depth, b, 6, d)
    kvmod = _mods(c8, kv_mod_w[None], kv_mod_b[None])[0, :b].reshape(b, 2, d)
    kv = None
    for l in range(depth):
        route = (norm2_g[l], moe_router_w[l], moe_router_b[l])
        if l < n_a:
            x, routed = _conv_layer(x, mods[l], norm1_g[l], conv_w_pw1[l], conv_b_pw1[l], conv_w_dw[l],
                                    conv_b_dw[l], conv_ln_g[l], conv_ln_b[l], conv_w_pw2[l],
                                    conv_b_pw2[l], route, t)
        else:
            lb = l - n_a
            x, routed = _fox_layer(x, mods[l], norm1_g[l], attn_w_qg[lb], q_norm_g[lb], k_norm_g,
                                   attn_w_o[lb], kv, route, t)
        x = _moe_layer(x, routed, mods[l], l, moe_w_gu, moe_b_gu, moe_w_down, moe_b_down,
                       final_norm_g, l == depth - 1, t)
        if l == n_a - 1:
            kv = _shared_kv(x, kvmod, kv_norm_g, w_kvf, b_f, k_norm_g, t)
    return x
```

```python
import functools
import math

import jax
import jax.numpy as jnp
from jax import lax
from jax.experimental import pallas as pl
from jax.experimental.pallas import tpu as pltpu

N_HEADS = 16
HEAD_DIM = 64
CONV_WIDTH = 31
N_EXPERTS = 32
TOP_K = 4
SWIGLU_ALPHA = 1.702
SWIGLU_LIMIT = 7.0
EPS = 1e-6

LANES = 128
HALO = 32
CONV_ROWS = 16
ISSUE_UNROLL = 4
KV_UNROLL = 4
EXPERT_ROWS = 512
VMEM_LIMIT = 56 * 1024 * 1024
LOG2E = 1.4426950408889634
NEG_BIG = -1e30
N_FPIECES = 3
DIRECT_EXP_LIMIT = 60.0
ZERO_WEIGHT_EXPONENT = -160.0
BOUND_SLACK = 1.05

F32 = jnp.float32
BF16 = jnp.bfloat16
HIGHEST = lax.Precision.HIGHEST


def _cparams(sem):
    return pltpu.CompilerParams(dimension_semantics=sem, vmem_limit_bytes=VMEM_LIMIT)


def _rms(x, g):
    return x * lax.rsqrt(jnp.mean(x * x, axis=-1, keepdims=True) + EPS) * g


def _split3(f):
    hi = f.astype(BF16).astype(F32)
    r1 = f - hi
    mid = r1.astype(BF16).astype(F32)
    lo = (r1 - mid).astype(BF16).astype(F32)
    return hi, mid, lo


def _mods_kernel(c_ref, w_ref, b_ref, o_ref):
    c = c_ref[...]
    ca = c * jax.nn.sigmoid(c)
    o_ref[0] = jnp.dot(ca, w_ref[0], precision=HIGHEST, preferred_element_type=F32) + b_ref[0]


def _mods(c8, w, b):
    n_l, d, m = w.shape
    tn = min(m, 1024)
    return pl.pallas_call(
        _mods_kernel,
        out_shape=jax.ShapeDtypeStruct((n_l, 8, m), F32),
        grid=(n_l, m // tn),
        in_specs=[
            pl.BlockSpec((8, d), lambda l, j: (0, 0)),
            pl.BlockSpec((1, d, tn), lambda l, j: (l, 0, j)),
            pl.BlockSpec((1, 1, tn), lambda l, j: (l, 0, j)),
        ],
        out_specs=pl.BlockSpec((1, 8, tn), lambda l, j: (l, 0, j)),
        compiler_params=_cparams(("arbitrary", "arbitrary")),
        name="mods",
    )(c8, w, b.reshape(n_l, 1, m))


def _conv_kernel(x_ref, mod_ref, ng_ref, w1_ref, b1_ref, wdw_ref, bdw_ref, lng_ref, lnb_ref,
                 w2_ref, b2_ref, ng2_ref, rw_ref, rb_ref,
                 o_ref, h_ref, gate_ref, meta_ref, cnt_ref, ubuf, cbuf):
    t, d = x_ref.shape[1], x_ref.shape[2]
    slab = ubuf.shape[1:]

    @pl.when(pl.program_id(1) == 0)
    def _():
        ubuf[pl.ds(0, HALO)] = jnp.zeros((HALO,) + slab, F32)

    x = x_ref[0]
    mod = mod_ref[0]
    h = _rms(x, ng_ref[...]) * (1.0 + mod[1:2, :]) + mod[0:1, :]
    u = jnp.dot(h.astype(BF16), w1_ref[...], preferred_element_type=F32) + b1_ref[...]
    u = u[:, :d] * jax.nn.sigmoid(u[:, d:])
    ubuf[pl.ds(HALO, t)] = u.reshape((t,) + slab)

    def conv_rows(c, carry):
        base = pl.multiple_of(c * CONV_ROWS, CONV_ROWS)
        acc = jnp.zeros((CONV_ROWS,) + slab, F32)
        for j in range(CONV_WIDTH):
            acc = acc + ubuf[pl.ds(base + (HALO - (CONV_WIDTH - 1) + j), CONV_ROWS)] * wdw_ref[j]
        cbuf[pl.ds(base, CONV_ROWS)] = acc
        return carry

    lax.fori_loop(0, t // CONV_ROWS, conv_rows, 0)
    ubuf[pl.ds(0, HALO)] = ubuf[pl.ds(t, HALO)]
    acc = cbuf[...].reshape(t, d) + bdw_ref[...]
    mu = jnp.mean(acc, axis=-1, keepdims=True)
    cen = acc - mu
    var = jnp.mean(cen * cen, axis=-1, keepdims=True)
    y = cen * lax.rsqrt(var + EPS) * lng_ref[...] + lnb_ref[...]
    y = y * jax.nn.sigmoid(y)
    y = jnp.dot(y.astype(BF16), w2_ref[...], preferred_element_type=F32) + b2_ref[...]
    x_new = x + mod[2:3, :] * y
    o_ref[0] = x_new
    _route_tail(x_new, mod, ng2_ref, rw_ref, rb_ref, h_ref, gate_ref, meta_ref, cnt_ref)


def _conv_layer(x, mod, ng, w1, b1, wdw, bdw, lng, lnb, w2, b2, route, t):
    b, s, d = x.shape
    row = lambda a: a.reshape(1, -1)
    slab = (d // LANES, LANES)
    wdw_p = jnp.zeros((HALO, d), F32).at[:CONV_WIDTH].set(wdw).reshape((HALO,) + slab)
    const = lambda shape: pl.BlockSpec(shape, lambda bi, si: (0,) * len(shape))
    r_ops, r_in, r_shapes, r_out = _route_plumbing(*route, b, s, d, t)
    outs = pl.pallas_call(
        _conv_kernel,
        out_shape=(jax.ShapeDtypeStruct((b, s, d), F32),) + r_shapes,
        grid=(b, s // t),
        in_specs=[
            pl.BlockSpec((1, t, d), lambda bi, si: (bi, si, 0)),
            pl.BlockSpec((1, 6, d), lambda bi, si: (bi, 0, 0)),
            const((1, d)), const((d, 2 * d)), const((1, 2 * d)), const((HALO,) + slab), const((1, d)),
            const((1, d)), const((1, d)), const((d, d)), const((1, d)),
        ] + r_in,
        out_specs=(pl.BlockSpec((1, t, d), lambda bi, si: (bi, si, 0)),) + r_out,
        scratch_shapes=[pltpu.VMEM((t + HALO,) + slab, F32), pltpu.VMEM((t,) + slab, F32)],
        compiler_params=_cparams(("arbitrary", "arbitrary")),
        name="conv_layer",
    )(x, mod, row(ng), w1.astype(BF16), row(b1), wdw_p, row(bdw), row(lng), row(lnb),
      w2.astype(BF16), row(b2), *r_ops)
    return outs[0], outs[1:]


def _route_tail(x, mod, ng_ref, rw_ref, rb_ref, h_ref, gate_ref, meta_ref, cnt_ref):
    t = x.shape[0]

    @pl.when(jnp.logical_and(pl.program_id(0) == 0, pl.program_id(1) == 0))
    def _():
        cnt_ref[...] = jnp.zeros(cnt_ref.shape, F32)

    h = _rms(x, ng_ref[...]) * (1.0 + mod[4:5, :]) + mod[3:4, :]
    h_ref[...] = h.reshape(h_ref.shape)
    h_hi = h.astype(BF16)
    h_lo = (h - h_hi.astype(F32)).astype(BF16)
    lg2 = jnp.dot(h_hi, rw_ref[...], preferred_element_type=F32)
    logits = (lg2[:, :LANES] + lg2[:, LANES:] + rb_ref[...]
              + jnp.dot(h_lo, rw_ref[:, pl.ds(0, LANES)], preferred_element_type=F32))
    lane = lax.broadcasted_iota(jnp.int32, (t, LANES), 1).astype(F32)
    work = logits
    vals, idxs = [], []
    for _ in range(TOP_K):
        m = jnp.max(work, axis=-1, keepdims=True)
        idx = jnp.min(jnp.where(work == m, lane, float(LANES)), axis=-1, keepdims=True)
        vals.append(m)
        idxs.append(idx)
        work = jnp.where(lane == idx, -jnp.inf, work)
    exps = [jnp.exp(v - vals[0]) for v in vals]
    denom = exps[0] + exps[1] + exps[2] + exps[3]
    onehot = jnp.zeros((t, LANES), F32)
    for idx in idxs:
        onehot = onehot + (lane == idx).astype(F32)
    r_i = lax.broadcasted_iota(jnp.int32, (t, t), 0)
    c_i = lax.broadcasted_iota(jnp.int32, (t, t), 1)
    tri = (c_i < r_i).astype(BF16)
    base = jnp.dot(tri, onehot.astype(BF16), preferred_element_type=F32) + cnt_ref[0:1, :]
    gate_out = jnp.zeros((t, LANES), F32)
    meta = jnp.zeros((t, LANES), F32)
    for k in range(TOP_K):
        rank = jnp.sum(jnp.where(lane == idxs[k], base, 0.0), axis=-1, keepdims=True)
        gate_out = jnp.where(lane == k, exps[k] / denom, gate_out)
        meta = jnp.where(lane == k, rank, meta)
        meta = jnp.where(lane == TOP_K + k, idxs[k], meta)
    gate_ref[...] = gate_out
    meta_ref[0] = meta.T[0:2 * TOP_K, :].astype(jnp.int32)
    cnt_ref[...] = cnt_ref[...] + jnp.sum(onehot, axis=0, keepdims=True)


def _route_plumbing(ng, rw, rb, b, s, d, t):
    n = b * s
    tiles_per_batch = s // t
    rw_p = jnp.zeros((d, LANES), F32).at[:, :N_EXPERTS].set(rw)
    rw_hi = rw_p.astype(BF16)
    rw2 = jnp.concatenate([rw_hi, (rw_p - rw_hi.astype(F32)).astype(BF16)], axis=1)
    rb_p = jnp.full((1, LANES), NEG_BIG, F32).at[0, :N_EXPERTS].set(rb)
    const = lambda shape: pl.BlockSpec(shape, lambda bi, si: (0,) * len(shape))
    flat = lambda bi, si: bi * tiles_per_batch + si
    operands = (ng.reshape(1, d), rw2, rb_p)
    in_specs = [const((1, d)), const((d, 2 * LANES)), const((1, LANES))]
    out_shapes = (
        jax.ShapeDtypeStruct((n, d // LANES, LANES), F32),
        jax.ShapeDtypeStruct((n, LANES), F32),
        jax.ShapeDtypeStruct((n // t, 2 * TOP_K, t), jnp.int32),
        jax.ShapeDtypeStruct((8, LANES), F32),
    )
    out_specs = (
        pl.BlockSpec((t, d // LANES, LANES), lambda bi, si: (flat(bi, si), 0, 0)),
        pl.BlockSpec((t, LANES), lambda bi, si: (flat(bi, si), 0)),
        pl.BlockSpec((1, 2 * TOP_K, t), lambda bi, si: (flat(bi, si), 0, 0)),
        const((8, LANES)),
    )
    return operands, in_specs, out_shapes, out_specs


def _dispatch_kernel(pad0_ref, padn_ref, h_ref, dest_hbm, xs_hbm, zrow, idx_smem, sem_idx, sem_rows, sem_pad):
    t = h_ref.shape[0]
    i = pl.program_id(0)

    @pl.when(i == 0)
    def _():
        zrow[...] = jnp.zeros(zrow.shape, F32)
        for start in (True, False):
            def per_expert(e, carry, start=start):
                def per_row(r, c):
                    pad_cp = pltpu.make_async_copy(zrow, xs_hbm.at[pad0_ref[e] + r], sem_pad)
                    if start:
                        pad_cp.start()
                    else:
                        pad_cp.wait()
                    return c
                return lax.fori_loop(0, padn_ref[e], per_row, carry)
            lax.fori_loop(0, N_EXPERTS, per_expert, 0)

    n_steps = pl.num_programs(0)
    n_idx = TOP_K * t

    def idx_copy(step, s):
        return pltpu.make_async_copy(dest_hbm.at[step], idx_smem.at[pl.ds(s * n_idx, n_idx)], sem_idx.at[s])

    @pl.when(i == 0)
    def _():
        idx_copy(0, 0).start()

    def scatter_rows(s):
        idx_copy(i, s).wait()

        @pl.when(i + 1 < n_steps)
        def _():
            idx_copy(i + 1, 1 - s).start()

        def issue(group, carry):
            tok0 = pl.multiple_of(group * ISSUE_UNROLL, ISSUE_UNROLL)
            for u in range(ISSUE_UNROLL):
                for k in range(TOP_K):
                    pltpu.make_async_copy(h_ref.at[tok0 + u], xs_hbm.at[idx_smem[s * n_idx + k * t + tok0 + u]],
                                          sem_rows).start(priority=(u * TOP_K + k) % 2)
            return carry

        lax.fori_loop(0, t // ISSUE_UNROLL, issue, 0)

    for s in range(2):
        @pl.when(i % 2 == s)
        def _(s=s):
            scatter_rows(s)

    for k in range(TOP_K):
        pltpu.make_async_copy(h_ref, xs_hbm.at[pl.ds(0, t)], sem_rows).wait()


def _dispatch(h, dest_tiles, pad_start, pad_n, p_rows, t):
    n, sub, _ = h.shape
    return pl.pallas_call(
        _dispatch_kernel,
        out_shape=jax.ShapeDtypeStruct((p_rows, sub, LANES), F32),
        grid_spec=pltpu.PrefetchScalarGridSpec(
            num_scalar_prefetch=2,
            grid=(n // t,),
            in_specs=[
                pl.BlockSpec((t, sub, LANES), lambda i, p0, pn: (i, 0, 0)),
                pl.BlockSpec(memory_space=pl.ANY),
            ],
            out_specs=pl.BlockSpec(memory_space=pl.ANY),
            scratch_shapes=[pltpu.VMEM((sub, LANES), F32), pltpu.SMEM((2 * TOP_K * t,), jnp.int32),
                            pltpu.SemaphoreType.DMA((2,)), pltpu.SemaphoreType.DMA, pltpu.SemaphoreType.DMA],
        ),
        compiler_params=_cparams(("arbitrary",)),
        name="moe_dispatch",
    )(pad_start, pad_n, h, dest_tiles)


def _experts_kernel(be_ref, nv_ref, nu_ref, nx_ref, xs_ref, wgu_hbm, bgu_ref, wdn_hbm, bdn_ref, ys_ref,
                    wgu_f32, wdn_f32, wgu_bf, wdn_bf, sem_w, *, layer):
    b = pl.program_id(0)
    f = wdn_bf.shape[0]
    sub = xs_ref.shape[1]
    half = EXPERT_ROWS // 2
    e = be_ref[b]
    e_prev = be_ref[jnp.maximum(b - 1, 0)]
    used = b < nu_ref[0]

    def weight_copies(ex):
        return (pltpu.make_async_copy(wgu_hbm.at[layer, ex], wgu_f32, sem_w.at[0]),
                pltpu.make_async_copy(wdn_hbm.at[layer, ex], wdn_f32, sem_w.at[1]))

    @pl.when(b == 0)
    def _():
        for cp in weight_copies(e):
            cp.start()

    @pl.when(jnp.logical_and(used, jnp.logical_or(b == 0, e != e_prev)))
    def _():
        for cp in weight_copies(e):
            cp.wait()
        wgu_bf[...] = wgu_f32[...].astype(BF16)
        wdn_bf[...] = wdn_f32[...].astype(BF16)

        @pl.when(nx_ref[b] >= 0)
        def _():
            for cp in weight_copies(nx_ref[b]):
                cp.start()

    def mlp(rows):
        x = xs_ref[pl.ds(0, rows)].reshape(rows, sub * LANES).astype(BF16)
        gu = jnp.dot(x, wgu_bf[...], preferred_element_type=F32) + bgu_ref[0, 0]
        x_glu = jnp.minimum(gu[:, :f], SWIGLU_LIMIT)
        x_lin = jnp.clip(gu[:, f:], -SWIGLU_LIMIT, SWIGLU_LIMIT)
        act = x_glu * jax.nn.sigmoid(SWIGLU_ALPHA * x_glu) * (x_lin + 1.0)
        y = jnp.dot(act.astype(BF16), wdn_bf[...], preferred_element_type=F32) + bdn_ref[0, 0]
        ys_ref[pl.ds(0, rows)] = y.reshape(rows, sub, LANES)

    @pl.when(jnp.logical_and(used, nv_ref[b] > half))
    def _():
        mlp(EXPERT_ROWS)

    @pl.when(jnp.logical_and(used, nv_ref[b] <= half))
    def _():
        mlp(half)


def _experts(xs, block_e, n_valid, n_used, next_e, layer, w_gu, b_gu, w_dn, b_dn):
    p_rows, sub, _ = xs.shape
    n_l, n_e, d, f2 = w_gu.shape
    f = f2 // 2
    n_blocks = p_rows // EXPERT_ROWS
    row_map = lambda b, be, nv, nu, nx: (jnp.minimum(b, nu[0] - 1), 0, 0)
    exp_map = lambda b, be, nv, nu, nx: (layer, be[b], 0, 0)
    return pl.pallas_call(
        functools.partial(_experts_kernel, layer=layer),
        out_shape=jax.ShapeDtypeStruct((p_rows, sub, LANES), F32),
        grid_spec=pltpu.PrefetchScalarGridSpec(
            num_scalar_prefetch=4,
            grid=(n_blocks,),
            in_specs=[
                pl.BlockSpec((EXPERT_ROWS, sub, LANES), row_map),
                pl.BlockSpec(memory_space=pl.ANY),
                pl.BlockSpec((1, 1, 1, f2), exp_map),
                pl.BlockSpec(memory_space=pl.ANY),
                pl.BlockSpec((1, 1, 1, d), exp_map),
            ],
            out_specs=pl.BlockSpec((EXPERT_ROWS, sub, LANES), row_map),
            scratch_shapes=[pltpu.VMEM((d, f2), F32), pltpu.VMEM((f, d), F32),
                            pltpu.VMEM((d, f2), BF16), pltpu.VMEM((f, d), BF16),
                            pltpu.SemaphoreType.DMA((2,))],
        ),
        compiler_params=_cparams(("arbitrary",)),
        name="moe_experts",
    )(block_e, n_valid, n_used, next_e, xs, w_gu, b_gu.reshape(n_l, n_e, 1, f2), w_dn,
      b_dn.reshape(n_l, n_e, 1, d))


def _combine_kernel(x_ref, gate_ref, mod_ref, fng_ref, dest_hbm, ys_hbm, o_ref, buf, idx_smem, sem_idx, sem_rows,
                    *, final):
    t = x_ref.shape[0]
    i = pl.program_id(0)
    n_steps = pl.num_programs(0)
    slot = i % 2

    def idx_copy(step, s):
        return pltpu.make_async_copy(dest_hbm.at[step], idx_smem.at[pl.ds(s * TOP_K * t, TOP_K * t)],
                                     sem_idx.at[s])

    def start_rows(s):
        def issue(group, carry):
            tok0 = pl.multiple_of(group * ISSUE_UNROLL, ISSUE_UNROLL)
            for u in range(ISSUE_UNROLL):
                for k in range(TOP_K):
                    pltpu.make_async_copy(ys_hbm.at[idx_smem[(s * TOP_K + k) * t + tok0 + u]],
                                          buf.at[s * TOP_K + k, tok0 + u],
                                          sem_rows.at[s]).start(priority=(u * TOP_K + k) % 2)
            return carry

        lax.fori_loop(0, t // ISSUE_UNROLL, issue, 0)

    @pl.when(i == 0)
    def _():
        first = idx_copy(0, 0)
        first.start()
        first.wait()
        start_rows(0)

    for nxt in range(2):
        @pl.when(jnp.logical_and(i + 1 < n_steps, slot == 1 - nxt))
        def _(nxt=nxt):
            idx_copy(i + 1, nxt).start()

    for k in range(TOP_K):
        pltpu.make_async_copy(ys_hbm.at[pl.ds(0, t)], buf.at[slot * TOP_K + k], sem_rows.at[slot]).wait()

    for nxt in range(2):
        @pl.when(jnp.logical_and(i + 1 < n_steps, slot == 1 - nxt))
        def _(nxt=nxt):
            idx_copy(i + 1, nxt).wait()
            start_rows(nxt)

    gate = gate_ref[...]
    y = gate[:, 0:1] * buf[slot * TOP_K].reshape(x_ref.shape)
    for k in range(1, TOP_K):
        y = y + gate[:, k:k + 1] * buf[slot * TOP_K + k].reshape(x_ref.shape)
    out = x_ref[...] + mod_ref[0][5:6, :] * y
    o_ref[...] = _rms(out, fng_ref[...]) if final else out


def _combine(x2, gates, mod, final_g, dest_tiles, ys, t, tiles_per_batch, final):
    n, d = x2.shape
    return pl.pallas_call(
        functools.partial(_combine_kernel, final=final),
        out_shape=jax.ShapeDtypeStruct((n, d), F32),
        grid=(n // t,),
        in_specs=[
            pl.BlockSpec((t, d), lambda i: (i, 0)),
            pl.BlockSpec((t, LANES), lambda i: (i, 0)),
            pl.BlockSpec((1, 6, d), lambda i: (i // tiles_per_batch, 0, 0)),
            pl.BlockSpec((1, d), lambda i: (0, 0)),
            pl.BlockSpec(memory_space=pl.ANY),
            pl.BlockSpec(memory_space=pl.ANY),
        ],
        out_specs=pl.BlockSpec((t, d), lambda i: (i, 0)),
        scratch_shapes=[pltpu.VMEM((2 * TOP_K, t, d // LANES, LANES), F32), pltpu.SMEM((2 * TOP_K * t,), jnp.int32),
                        pltpu.SemaphoreType.DMA((2,)), pltpu.SemaphoreType.DMA((2,))],
        compiler_params=_cparams(("arbitrary",)),
        name="moe_combine",
    )(x2, gates, mod, final_g.reshape(1, d), dest_tiles, ys)


def _moe_layer(x, routed, mod, layer, w_gu, b_gu, w_dn, b_dn, final_g, final, t):
    b, s, d = x.shape
    n = b * s
    x2 = x.reshape(n, d)
    tiles_per_batch = s // t
    h, gates, meta, cnt = routed
    counts = cnt[0, :N_EXPERTS].astype(jnp.int32)
    padded = (counts + EXPERT_ROWS - 1) // EXPERT_ROWS * EXPERT_ROWS
    pend = jnp.cumsum(padded)
    pstart = pend - padded
    n_blocks = -(-(n * TOP_K) // EXPERT_ROWS) + N_EXPERTS
    p_rows = n_blocks * EXPERT_ROWS
    rank = meta[:, :TOP_K, :]
    eidx = meta[:, TOP_K:, :]
    experts = jnp.arange(N_EXPERTS, dtype=jnp.int32)
    dest = rank + jnp.sum(jnp.where(eidx[..., None] == experts, pstart, 0), axis=-1)
    dest_tiles = dest.reshape(n // t, TOP_K * t)
    block_row0 = jnp.arange(n_blocks, dtype=jnp.int32) * EXPERT_ROWS
    block_e = jnp.minimum(jnp.sum((pend[None, :] <= block_row0[:, None]).astype(jnp.int32), axis=1),
                          N_EXPERTS - 1)
    n_used = (pend[-1:] // EXPERT_ROWS).astype(jnp.int32)
    row_end = jnp.sum(jnp.where(block_e[:, None] == experts, pstart + counts, 0), axis=1)
    n_valid = jnp.clip(row_end - block_row0, 0, EXPERT_ROWS).astype(jnp.int32)
    half = EXPERT_ROWS // 2
    pad_n = (counts + half - 1) // half * half - counts
    xs = _dispatch(h, dest_tiles, pstart + counts, pad_n, p_rows, t)
    block_id = jnp.arange(n_blocks, dtype=jnp.int32)
    later_other = jnp.logical_and(
        jnp.logical_and(block_id[None, :] > block_id[:, None], block_id[None, :] < n_used[0]),
        block_e[None, :] != block_e[:, None])
    first_other = jnp.min(jnp.where(later_other, block_id[None, :], n_blocks), axis=1)
    next_e = jnp.where(first_other < n_blocks, block_e[jnp.minimum(first_other, n_blocks - 1)], -1)
    ys = _experts(xs, block_e, n_valid, n_used, next_e.astype(jnp.int32), layer, w_gu, b_gu, w_dn, b_dn)
    out = _combine(x2, gates, mod, final_g, dest_tiles, ys, t, tiles_per_batch, final)
    return out.reshape(b, s, d)


def _kv_kernel(x_ref, mod_ref, ng_ref, wkv_ref, wf_ref, bf_ref, kg_ref, hsum_ref,
               kt_ref, v_ref, qf_ref, fb_ref, carry):
    t, d = x_ref.shape[1], x_ref.shape[2]
    n_pairs = N_HEADS // 2

    @pl.when(pl.program_id(1) == 0)
    def _():
        carry[...] = jnp.zeros(carry.shape, F32)

    mod = mod_ref[0]
    h = _rms(x_ref[0], ng_ref[...]) * (1.0 + mod[1:2, :]) + mod[0:1, :]
    h_hi = h.astype(BF16)
    kv = jnp.dot(h_hi, wkv_ref[...], preferred_element_type=F32)
    k = kv[:, :d]
    v_ref[0] = kv[:, d:].astype(BF16)
    ms = jnp.dot((k * k).astype(BF16), hsum_ref[...], preferred_element_type=F32)
    k = k * lax.rsqrt(ms + EPS) * kg_ref[...]
    kt = k.T
    for hp in range(n_pairs):
        kt_ref[0, hp, 0, pl.ds(0, LANES), :] = kt[hp * LANES:(hp + 1) * LANES, :].astype(BF16)

    h_lo = (h - h_hi.astype(F32)).astype(BF16)
    fz2 = jnp.dot(h_hi, wf_ref[...], preferred_element_type=F32)
    fz = (fz2[:, :LANES] + fz2[:, LANES:] + bf_ref[...]
          + jnp.dot(h_lo, wf_ref[:, pl.ds(0, LANES)], preferred_element_type=F32))
    ls = jax.nn.log_sigmoid(fz)
    r_i = lax.broadcasted_iota(jnp.int32, (t, t), 0)
    c_i = lax.broadcasted_iota(jnp.int32, (t, t), 1)
    tri = (c_i <= r_i).astype(BF16)
    ls_hi, ls_mid, ls_lo = _split3(ls)
    cum2 = jnp.dot(tri, jnp.concatenate([ls_hi, ls_mid], axis=1).astype(BF16), preferred_element_type=F32)
    cum = (cum2[:, :LANES] + cum2[:, LANES:] + carry[0:1, :]
           + jnp.dot(tri, ls_lo.astype(BF16), preferred_element_type=F32))
    carry[...] = jnp.broadcast_to(cum[t - 1:t, :], carry.shape)
    f2 = cum * LOG2E
    row8 = lax.broadcasted_iota(jnp.int32, (8, LANES), 0)
    fb_ref[0, 0] = jnp.where(row8 == 0, f2[0:1, :], jnp.where(row8 == 1, f2[t - 1:t, :], 0.0))
    f2t = f2.T
    lane = lax.broadcasted_iota(jnp.int32, (t, LANES), 1)
    sub = lax.broadcasted_iota(jnp.int32, (LANES, t), 0)
    q_pieces = _split3(f2)
    k_pieces = _split3(-f2t)
    for hp in range(n_pairs):
        q_aug = jnp.zeros((t, LANES), F32)
        k_aug = jnp.zeros((LANES, t), F32)
        for hh in range(2):
            head = 2 * hp + hh
            o = hh * 2 * N_FPIECES
            for p in range(N_FPIECES):
                q_aug = jnp.where(lane == o + p, q_pieces[p][:, head:head + 1], q_aug)
                q_aug = jnp.where(lane == o + N_FPIECES + p, 1.0, q_aug)
                k_aug = jnp.where(sub == o + p, 1.0, k_aug)
                k_aug = jnp.where(sub == o + N_FPIECES + p, k_pieces[p][head:head + 1, :], k_aug)
        qf_ref[0, hp] = q_aug.astype(BF16)
        kt_ref[0, hp, 0, pl.ds(LANES, LANES), :] = k_aug.astype(BF16)


def _shared_kv(x, kvmod, ng, w_kvf, b_f, k_norm_g, t):
    b, s, d = x.shape
    n_pairs = N_HEADS // 2
    wkv = w_kvf[:, :2 * d].astype(BF16)
    wf = jnp.zeros((d, LANES), F32).at[:, :N_HEADS].set(w_kvf[:, 2 * d:])
    wf_hi = wf.astype(BF16)
    wf2 = jnp.concatenate([wf_hi, (wf - wf_hi.astype(F32)).astype(BF16)], axis=1)
    bf = jnp.zeros((1, LANES), F32).at[0, :N_HEADS].set(b_f)
    kg = jnp.tile(k_norm_g, N_HEADS).reshape(1, d)
    head_of = jnp.arange(d) // HEAD_DIM
    hsum = ((head_of[:, None] == head_of[None, :]).astype(F32) / HEAD_DIM).astype(BF16)
    const = lambda shape: pl.BlockSpec(shape, lambda bi, si: (0,) * len(shape))
    return pl.pallas_call(
        _kv_kernel,
        out_shape=(
            jax.ShapeDtypeStruct((b, n_pairs, s // t, 2 * LANES, t), BF16),
            jax.ShapeDtypeStruct((b, s, d), BF16),
            jax.ShapeDtypeStruct((b, n_pairs, s, LANES), BF16),
            jax.ShapeDtypeStruct((b, s // t, 8, LANES), F32),
        ),
        grid=(b, s // t),
        in_specs=[
            pl.BlockSpec((1, t, d), lambda bi, si: (bi, si, 0)),
            pl.BlockSpec((1, 2, d), lambda bi, si: (bi, 0, 0)),
            const((1, d)), const((d, 2 * d)), const((d, 2 * LANES)), const((1, LANES)), const((1, d)),
            const((d, d)),
        ],
        out_specs=(
            pl.BlockSpec((1, n_pairs, 1, 2 * LANES, t), lambda bi, si: (bi, 0, si, 0, 0)),
            pl.BlockSpec((1, t, d), lambda bi, si: (bi, si, 0)),
            pl.BlockSpec((1, n_pairs, t, LANES), lambda bi, si: (bi, 0, si, 0)),
            pl.BlockSpec((1, 1, 8, LANES), lambda bi, si: (bi, si, 0, 0)),
        ),
        scratch_shapes=[pltpu.VMEM((8, LANES), F32)],
        compiler_params=_cparams(("arbitrary", "arbitrary")),
        name="shared_kv",
    )(x, kvmod, ng.reshape(1, d), wkv, wf2, bf, kg, hsum), hsum


def _qg_kernel(x_ref, mod_ref, ng_ref, w_ref, qg_ref, hsum_ref, q_ref, g_ref):
    d = x_ref.shape[2]
    mod = mod_ref[0]
    h = _rms(x_ref[0], ng_ref[...]) * (1.0 + mod[1:2, :]) + mod[0:1, :]
    qg = jnp.dot(h.astype(BF16), w_ref[...], preferred_element_type=F32)
    q = qg[:, :d]
    ms = jnp.dot((q * q).astype(BF16), hsum_ref[...], preferred_element_type=F32)
    q = q * lax.rsqrt(ms + EPS) * qg_ref[...] * (LOG2E / math.sqrt(HEAD_DIM))
    q_ref[0] = q.astype(BF16)
    g_ref[0] = jax.nn.sigmoid(qg[:, d:]).astype(BF16)


def _qg(x, mod, ng, w_qg, q_norm_g, hsum, t):
    b, s, d = x.shape
    const = lambda shape: pl.BlockSpec(shape, lambda bi, si: (0,) * len(shape))
    tile = pl.BlockSpec((1, t, d), lambda bi, si: (bi, si, 0))
    return pl.pallas_call(
        _qg_kernel,
        out_shape=(jax.ShapeDtypeStruct((b, s, d), BF16), jax.ShapeDtypeStruct((b, s, d), BF16)),
        grid=(b, s // t),
        in_specs=[tile, pl.BlockSpec((1, 6, d), lambda bi, si: (bi, 0, 0)),
                  const((1, d)), const((d, 2 * d)), const((1, d)), const((d, d))],
        out_specs=(tile, tile),
        compiler_params=_cparams(("arbitrary", "arbitrary")),
        name="attn_qg",
    )(x, mod, ng.reshape(1, d), w_qg.astype(BF16), jnp.tile(q_norm_g, N_HEADS).reshape(1, d), hsum)


def _attn_kernel(j0_ref, q_ref, qf_ref, kt_ref, v_ref, o_ref, qa_scr, m_scr, l_scr, acc_scr, *, online):
    tq = q_ref.shape[1]
    tk = kt_ref.shape[4]
    i = pl.program_id(2)
    n_q = pl.num_programs(2)
    head0 = pl.program_id(0) * N_HEADS + 2 * pl.program_id(1)
    first = [j0_ref[(head0 + hh) * n_q + i] for hh in range(2)]
    first_both = jnp.maximum(first[0], first[1])
    lane = lax.broadcasted_iota(jnp.int32, (tq, LANES), 1)
    q2 = q_ref[0]
    qf = qf_ref[0, 0]
    zero = jnp.zeros((), BF16)
    n_aug = 2 * N_FPIECES
    qa_scr[0, :, pl.ds(0, LANES)] = jnp.where(lane < HEAD_DIM, q2, zero)
    qa_scr[0, :, pl.ds(LANES, LANES)] = jnp.where(lane < n_aug, qf, zero)
    qa_scr[1, :, pl.ds(0, LANES)] = jnp.where(lane >= HEAD_DIM, q2, zero)
    qa_scr[1, :, pl.ds(LANES, LANES)] = jnp.where(jnp.logical_and(lane >= n_aug, lane < 2 * n_aug), qf, zero)
    if online:
        m_scr[...] = jnp.full(m_scr.shape, NEG_BIG, F32)
    l_scr[...] = jnp.zeros(l_scr.shape, F32)
    acc_scr[...] = jnp.zeros(acc_scr.shape, F32)

    def scores(hh, j, masked):
        s = jnp.dot(qa_scr[hh], kt_ref[0, 0, j], preferred_element_type=F32)
        if masked:
            r_i = lax.broadcasted_iota(jnp.int32, (tq, tk), 0)
            c_i = lax.broadcasted_iota(jnp.int32, (tq, tk), 1)
            s = jnp.where(c_i <= r_i, s, NEG_BIG)
        return s

    def values(j):
        return v_ref[0, pl.ds(pl.multiple_of(j * tk, tk), tk), :]

    def tiles(js, masked, heads):
        for hh in heads:
            if online:
                for j in js:
                    s = scores(hh, j, masked)
                    m_prev = m_scr[hh]
                    m_new = jnp.maximum(m_prev, jnp.max(s, axis=-1, keepdims=True))
                    alpha = jnp.exp2(m_prev - m_new)
                    p = jnp.exp2(s - m_new[:, 0:1])
                    l_scr[hh] = alpha * l_scr[hh] + jnp.sum(p, axis=-1, keepdims=True)
                    acc_scr[hh] = alpha * acc_scr[hh] + jnp.dot(p.astype(BF16), values(j),
                                                                preferred_element_type=F32)
                    m_scr[hh] = m_new
            else:
                part, pv = l_scr[hh], acc_scr[hh]
                for j in js:
                    p = jnp.exp2(scores(hh, j, masked))
                    for c in range(tk // LANES):
                        part = part + p[:, c * LANES:(c + 1) * LANES]
                    pv = pv + jnp.dot(p.astype(BF16), values(j), preferred_element_type=F32)
                l_scr[hh], acc_scr[hh] = part, pv

    def one_tile(heads):
        def body(j, carry):
            tiles((j,), False, heads)
            return carry
        return body

    def tile_group(group, carry):
        j = first_both + KV_UNROLL * group
        tiles(tuple(j + u for u in range(KV_UNROLL)), False, (0, 1))
        return carry

    for hh in range(2):
        n_pairs_hh = (first_both - first[hh]) // 2

        def tile_pair(pair, carry, hh=hh):
            j = first[hh] + 2 * pair
            tiles((j, j + 1), False, (hh,))
            return carry

        lax.fori_loop(0, n_pairs_hh, tile_pair, 0)
        lax.fori_loop(first[hh] + 2 * n_pairs_hh, first_both, one_tile((hh,)), 0)
    n_groups = (i - first_both) // KV_UNROLL
    lax.fori_loop(0, n_groups, tile_group, 0)
    lax.fori_loop(first_both + KV_UNROLL * n_groups, i, one_tile((0, 1)), 0)
    tiles((i,), True, (0, 1))
    if online:
        l0, l1 = l_scr[0], l_scr[1]
    else:
        l0 = jnp.sum(l_scr[0], axis=-1, keepdims=True)
        l1 = jnp.sum(l_scr[1], axis=-1, keepdims=True)
    o_ref[0] = jnp.where(lane < HEAD_DIM, acc_scr[0] / l0, acc_scr[1] / l1).astype(BF16)


def _attention(j0, q, qf, kt, v, t, online):
    b, s, d = q.shape
    n_pairs = N_HEADS // 2
    nkv = s // t
    return pl.pallas_call(
        functools.partial(_attn_kernel, online=online),
        out_shape=jax.ShapeDtypeStruct((b, s, d), BF16),
        grid_spec=pltpu.PrefetchScalarGridSpec(
            num_scalar_prefetch=1,
            grid=(b, n_pairs, s // t),
            in_specs=[
                pl.BlockSpec((1, t, LANES), lambda bi, hp, i, j0r: (bi, i, hp)),
                pl.BlockSpec((1, 1, t, LANES), lambda bi, hp, i, j0r: (bi, hp, i, 0)),
                pl.BlockSpec((1, 1, nkv, 2 * LANES, t), lambda bi, hp, i, j0r: (bi, hp, 0, 0, 0)),
                pl.BlockSpec((1, s, LANES), lambda bi, hp, i, j0r: (bi, 0, hp)),
            ],
            out_specs=pl.BlockSpec((1, t, LANES), lambda bi, hp, i, j0r: (bi, i, hp)),
            scratch_shapes=[
                pltpu.VMEM((2, t, 2 * LANES), BF16),
                pltpu.VMEM((2, t, LANES), F32),
                pltpu.VMEM((2, t, LANES), F32),
                pltpu.VMEM((2, t, LANES), F32),
            ],
        ),
        compiler_params=_cparams(("arbitrary", "arbitrary", "arbitrary")),
        name="fox_attention_online" if online else "fox_attention",
    )(j0, q, qf, kt, v)


def _attn_out_kernel(x_ref, o_ref, g_ref, mod_ref, w_ref, ng2_ref, rw_ref, rb_ref,
                     out_ref, h_ref, gate_ref, meta_ref, cnt_ref):
    mod = mod_ref[0]
    og = o_ref[0] * g_ref[0]
    y = jnp.dot(og, w_ref[...], preferred_element_type=F32)
    x_new = x_ref[0] + mod[2:3, :] * y
    out_ref[0] = x_new
    _route_tail(x_new, mod, ng2_ref, rw_ref, rb_ref, h_ref, gate_ref, meta_ref, cnt_ref)


def _attn_out(x, o, g, mod, w_o, route, t):
    b, s, d = x.shape
    tile = pl.BlockSpec((1, t, d), lambda bi, si: (bi, si, 0))
    r_ops, r_in, r_shapes, r_out = _route_plumbing(*route, b, s, d, t)
    outs = pl.pallas_call(
        _attn_out_kernel,
        out_shape=(jax.ShapeDtypeStruct((b, s, d), F32),) + r_shapes,
        grid=(b, s // t),
        in_specs=[tile, tile, tile, pl.BlockSpec((1, 6, d), lambda bi, si: (bi, 0, 0)),
                  pl.BlockSpec((d, d), lambda bi, si: (0, 0))] + r_in,
        out_specs=(tile,) + r_out,
        compiler_params=_cparams(("arbitrary", "arbitrary")),
        name="attn_out",
    )(x, o, g, mod, w_o.astype(BF16), *r_ops)
    return outs[0], outs[1:]


def _fox_layer(x, mod, ng, w_qg, q_norm_g, k_norm_g, w_o, kv, route, t):
    (kt, v, qf, fb), hsum = kv
    n_t = x.shape[1] // t
    q, g = _qg(x, mod, ng, w_qg, q_norm_g, hsum, t)
    bound = (HEAD_DIM * jnp.max(jnp.abs(q_norm_g)) * jnp.max(jnp.abs(k_norm_g))
             * (LOG2E / math.sqrt(HEAD_DIM)))
    f_first, f_last = fb[:, :, 0, :N_HEADS], fb[:, :, 1, :N_HEADS]
    best = (bound * BOUND_SLACK + f_first[:, :, None, :]) - f_last[:, None, :, :]
    before = jnp.arange(n_t)[None, :] < jnp.arange(n_t)[:, None]
    dead = jnp.logical_and(best < ZERO_WEIGHT_EXPONENT, before[None, :, :, None])
    j0 = jnp.sum(dead.astype(jnp.int32), axis=2).transpose(0, 2, 1).reshape(-1)
    o = lax.cond(bound <= DIRECT_EXP_LIMIT,
                 functools.partial(_attention, t=t, online=False),
                 lambda j0_, *rest: _attention(jnp.zeros_like(j0_), *rest, t=t, online=True),
                 j0, q, qf, kt, v)
    return _attn_out(x, o, g, mod, w_o, route, t)


def kernel(x, c, mod_w, mod_b, norm1_g, norm2_g, conv_w_pw1, conv_b_pw1, conv_w_dw, conv_b_dw, conv_ln_g, conv_ln_b, conv_w_pw2, conv_b_pw2, kv_mod_w, kv_mod_b, kv_norm_g, w_kvf, b_f, k_norm_g, attn_w_qg, q_norm_g, attn_w_o, moe_router_w, moe_router_b, moe_w_gu, moe_b_gu, moe_w_down, moe_b_down, final_norm_g):
    b, s, d = x.shape
    depth = mod_w.shape[0]
    n_a = conv_w_pw1.shape[0]
    t = min(512, s)
    c8 = jnp.zeros((8, d), F32).at[:b].set(c)
    mods = _mods(c8, mod_w, mod_b)[:, :b].reshape(depth, b, 6, d)
    kvmod = _mods(c8, kv_mod_w[None], kv_mod_b[None])[0, :b].reshape(b, 2, d)
    kv = None
    for l in range(depth):
        route = (norm2_g[l], moe_router_w[l], moe_router_b[l])
        if l < n_a:
            x, routed = _conv_layer(x, mods[l], norm1_g[l], conv_w_pw1[l], conv_b_pw1[l], conv_w_dw[l],
                                    conv_b_dw[l], conv_ln_g[l], conv_ln_b[l], conv_w_pw2[l],
                                    conv_b_pw2[l], route, t)
        else:
            lb = l - n_a
            x, routed = _fox_layer(x, mods[l], norm1_g[l], attn_w_qg[lb], q_norm_g[lb], k_norm_g,
                                   attn_w_o[lb], kv, route, t)
        x = _moe_layer(x, routed, mods[l], l, moe_w_gu, moe_b_gu, moe_w_down, moe_b_down,
                       final_norm_g, l == depth - 1, t)
        if l == n_a - 1:
            kv = _shared_kv(x, kvmod, kv_norm_g, w_kvf, b_f, k_norm_g, t)
    return x
```

```python
import functools
import math

import jax
import jax.numpy as jnp
from jax import lax
from jax.experimental import pallas as pl
from jax.experimental.pallas import tpu as pltpu

N_HEADS = 16
HEAD_DIM = 64
CONV_WIDTH = 31
N_EXPERTS = 32
TOP_K = 4
SWIGLU_ALPHA = 1.702
SWIGLU_LIMIT = 7.0
EPS = 1e-6

LANES = 128
HALO = 32
CONV_ROWS = 16
ISSUE_UNROLL = 4
KV_UNROLL = 4
Q_TILES_PER_STEP = 2
EXPERT_ROWS = 512
VMEM_LIMIT = 56 * 1024 * 1024
LOG2E = 1.4426950408889634
NEG_BIG = -1e30
N_FPIECES = 3
DIRECT_EXP_LIMIT = 60.0
ZERO_WEIGHT_EXPONENT = -160.0
BOUND_SLACK = 1.05

F32 = jnp.float32
BF16 = jnp.bfloat16
HIGHEST = lax.Precision.HIGHEST


def _cparams(sem):
    return pltpu.CompilerParams(dimension_semantics=sem, vmem_limit_bytes=VMEM_LIMIT)


def _rms(x, g):
    return x * lax.rsqrt(jnp.mean(x * x, axis=-1, keepdims=True) + EPS) * g


def _split3(f):
    hi = f.astype(BF16).astype(F32)
    r1 = f - hi
    mid = r1.astype(BF16).astype(F32)
    lo = (r1 - mid).astype(BF16).astype(F32)
    return hi, mid, lo


def _mods_kernel(c_ref, w_ref, b_ref, o_ref):
    c = c_ref[...]
    ca = c * jax.nn.sigmoid(c)
    o_ref[0] = jnp.dot(ca, w_ref[0], precision=HIGHEST, preferred_element_type=F32) + b_ref[0]


def _mods(c8, w, b):
    n_l, d, m = w.shape
    tn = min(m, 1024)
    return pl.pallas_call(
        _mods_kernel,
        out_shape=jax.ShapeDtypeStruct((n_l, 8, m), F32),
        grid=(n_l, m // tn),
        in_specs=[
            pl.BlockSpec((8, d), lambda l, j: (0, 0)),
            pl.BlockSpec((1, d, tn), lambda l, j: (l, 0, j)),
            pl.BlockSpec((1, 1, tn), lambda l, j: (l, 0, j)),
        ],
        out_specs=pl.BlockSpec((1, 8, tn), lambda l, j: (l, 0, j)),
        compiler_params=_cparams(("arbitrary", "arbitrary")),
        name="mods",
    )(c8, w, b.reshape(n_l, 1, m))


def _conv_kernel(x_ref, mod_ref, ng_ref, w1_ref, b1_ref, wdw_ref, bdw_ref, lng_ref, lnb_ref,
                 w2_ref, b2_ref, ng2_ref, rw_ref, rb_ref,
                 o_ref, h_ref, gate_ref, meta_ref, cnt_ref, ubuf, cbuf):
    t, d = x_ref.shape[1], x_ref.shape[2]
    slab = ubuf.shape[1:]

    @pl.when(pl.program_id(1) == 0)
    def _():
        ubuf[pl.ds(0, HALO)] = jnp.zeros((HALO,) + slab, F32)

    x = x_ref[0]
    mod = mod_ref[0]
    h = _rms(x, ng_ref[...]) * (1.0 + mod[1:2, :]) + mod[0:1, :]
    u = jnp.dot(h.astype(BF16), w1_ref[...], preferred_element_type=F32) + b1_ref[...]
    u = u[:, :d] * jax.nn.sigmoid(u[:, d:])
    ubuf[pl.ds(HALO, t)] = u.reshape((t,) + slab)

    def conv_rows(c, carry):
        base = pl.multiple_of(c * CONV_ROWS, CONV_ROWS)
        acc = jnp.zeros((CONV_ROWS,) + slab, F32)
        for j in range(CONV_WIDTH):
            acc = acc + ubuf[pl.ds(base + (HALO - (CONV_WIDTH - 1) + j), CONV_ROWS)] * wdw_ref[j]
        cbuf[pl.ds(base, CONV_ROWS)] = acc
        return carry

    lax.fori_loop(0, t // CONV_ROWS, conv_rows, 0)
    ubuf[pl.ds(0, HALO)] = ubuf[pl.ds(t, HALO)]
    acc = cbuf[...].reshape(t, d) + bdw_ref[...]
    mu = jnp.mean(acc, axis=-1, keepdims=True)
    cen = acc - mu
    var = jnp.mean(cen * cen, axis=-1, keepdims=True)
    y = cen * lax.rsqrt(var + EPS) * lng_ref[...] + lnb_ref[...]
    y = y * jax.nn.sigmoid(y)
    y = jnp.dot(y.astype(BF16), w2_ref[...], preferred_element_type=F32) + b2_ref[...]
    x_new = x + mod[2:3, :] * y
    o_ref[0] = x_new
    _route_tail(x_new, mod, ng2_ref, rw_ref, rb_ref, h_ref, gate_ref, meta_ref, cnt_ref)


def _conv_layer(x, mod, ng, w1, b1, wdw, bdw, lng, lnb, w2, b2, route, t):
    b, s, d = x.shape
    row = lambda a: a.reshape(1, -1)
    slab = (d // LANES, LANES)
    wdw_p = jnp.zeros((HALO, d), F32).at[:CONV_WIDTH].set(wdw).reshape((HALO,) + slab)
    const = lambda shape: pl.BlockSpec(shape, lambda bi, si: (0,) * len(shape))
    r_ops, r_in, r_shapes, r_out = _route_plumbing(*route, b, s, d, t)
    outs = pl.pallas_call(
        _conv_kernel,
        out_shape=(jax.ShapeDtypeStruct((b, s, d), F32),) + r_shapes,
        grid=(b, s // t),
        in_specs=[
            pl.BlockSpec((1, t, d), lambda bi, si: (bi, si, 0)),
            pl.BlockSpec((1, 6, d), lambda bi, si: (bi, 0, 0)),
            const((1, d)), const((d, 2 * d)), const((1, 2 * d)), const((HALO,) + slab), const((1, d)),
            const((1, d)), const((1, d)), const((d, d)), const((1, d)),
        ] + r_in,
        out_specs=(pl.BlockSpec((1, t, d), lambda bi, si: (bi, si, 0)),) + r_out,
        scratch_shapes=[pltpu.VMEM((t + HALO,) + slab, F32), pltpu.VMEM((t,) + slab, F32)],
        compiler_params=_cparams(("arbitrary", "arbitrary")),
        name="conv_layer",
    )(x, mod, row(ng), w1.astype(BF16), row(b1), wdw_p, row(bdw), row(lng), row(lnb),
      w2.astype(BF16), row(b2), *r_ops)
    return outs[0], outs[1:]


def _route_tail(x, mod, ng_ref, rw_ref, rb_ref, h_ref, gate_ref, meta_ref, cnt_ref):
    t = x.shape[0]

    @pl.when(jnp.logical_and(pl.program_id(0) == 0, pl.program_id(1) == 0))
    def _():
        cnt_ref[...] = jnp.zeros(cnt_ref.shape, F32)

    h = _rms(x, ng_ref[...]) * (1.0 + mod[4:5, :]) + mod[3:4, :]
    h_ref[...] = h.reshape(h_ref.shape)
    h_hi = h.astype(BF16)
    h_lo = (h - h_hi.astype(F32)).astype(BF16)
    lg2 = jnp.dot(h_hi, rw_ref[...], preferred_element_type=F32)
    logits = (lg2[:, :LANES] + lg2[:, LANES:] + rb_ref[...]
              + jnp.dot(h_lo, rw_ref[:, pl.ds(0, LANES)], preferred_element_type=F32))
    lane = lax.broadcasted_iota(jnp.int32, (t, LANES), 1).astype(F32)
    work = logits
    vals, idxs = [], []
    for _ in range(TOP_K):
        m = jnp.max(work, axis=-1, keepdims=True)
        idx = jnp.min(jnp.where(work == m, lane, float(LANES)), axis=-1, keepdims=True)
        vals.append(m)
        idxs.append(idx)
        work = jnp.where(lane == idx, -jnp.inf, work)
    exps = [jnp.exp(v - vals[0]) for v in vals]
    denom = exps[0] + exps[1] + exps[2] + exps[3]
    onehot = jnp.zeros((t, LANES), F32)
    for idx in idxs:
        onehot = onehot + (lane == idx).astype(F32)
    r_i = lax.broadcasted_iota(jnp.int32, (t, t), 0)
    c_i = lax.broadcasted_iota(jnp.int32, (t, t), 1)
    tri = (c_i < r_i).astype(BF16)
    base = jnp.dot(tri, onehot.astype(BF16), preferred_element_type=F32) + cnt_ref[0:1, :]
    gate_out = jnp.zeros((t, LANES), F32)
    meta = jnp.zeros((t, LANES), F32)
    for k in range(TOP_K):
        rank = jnp.sum(jnp.where(lane == idxs[k], base, 0.0), axis=-1, keepdims=True)
        gate_out = jnp.where(lane == k, exps[k] / denom, gate_out)
        meta = jnp.where(lane == k, rank, meta)
        meta = jnp.where(lane == TOP_K + k, idxs[k], meta)
    gate_ref[...] = gate_out
    meta_ref[0] = meta.T[0:2 * TOP_K, :].astype(jnp.int32)
    cnt_ref[...] = cnt_ref[...] + jnp.sum(onehot, axis=0, keepdims=True)


def _route_plumbing(ng, rw, rb, b, s, d, t):
    n = b * s
    tiles_per_batch = s // t
    rw_p = jnp.zeros((d, LANES), F32).at[:, :N_EXPERTS].set(rw)
    rw_hi = rw_p.astype(BF16)
    rw2 = jnp.concatenate([rw_hi, (rw_p - rw_hi.astype(F32)).astype(BF16)], axis=1)
    rb_p = jnp.full((1, LANES), NEG_BIG, F32).at[0, :N_EXPERTS].set(rb)
    const = lambda shape: pl.BlockSpec(shape, lambda bi, si: (0,) * len(shape))
    flat = lambda bi, si: bi * tiles_per_batch + si
    operands = (ng.reshape(1, d), rw2, rb_p)
    in_specs = [const((1, d)), const((d, 2 * LANES)), const((1, LANES))]
    out_shapes = (
        jax.ShapeDtypeStruct((n, d // LANES, LANES), F32),
        jax.ShapeDtypeStruct((n, LANES), F32),
        jax.ShapeDtypeStruct((n // t, 2 * TOP_K, t), jnp.int32),
        jax.ShapeDtypeStruct((8, LANES), F32),
    )
    out_specs = (
        pl.BlockSpec((t, d // LANES, LANES), lambda bi, si: (flat(bi, si), 0, 0)),
        pl.BlockSpec((t, LANES), lambda bi, si: (flat(bi, si), 0)),
        pl.BlockSpec((1, 2 * TOP_K, t), lambda bi, si: (flat(bi, si), 0, 0)),
        const((8, LANES)),
    )
    return operands, in_specs, out_shapes, out_specs


def _dispatch_kernel(pad0_ref, padn_ref, h_ref, dest_hbm, xs_hbm, zrow, idx_smem, sem_idx, sem_rows, sem_pad):
    t = h_ref.shape[0]
    i = pl.program_id(0)

    @pl.when(i == 0)
    def _():
        zrow[...] = jnp.zeros(zrow.shape, F32)
        for start in (True, False):
            def per_expert(e, carry, start=start):
                def per_row(r, c):
                    pad_cp = pltpu.make_async_copy(zrow, xs_hbm.at[pad0_ref[e] + r], sem_pad)
                    if start:
                        pad_cp.start()
                    else:
                        pad_cp.wait()
                    return c
                return lax.fori_loop(0, padn_ref[e], per_row, carry)
            lax.fori_loop(0, N_EXPERTS, per_expert, 0)

    n_steps = pl.num_programs(0)
    n_idx = TOP_K * t

    def idx_copy(step, s):
        return pltpu.make_async_copy(dest_hbm.at[step], idx_smem.at[pl.ds(s * n_idx, n_idx)], sem_idx.at[s])

    @pl.when(i == 0)
    def _():
        idx_copy(0, 0).start()

    def scatter_rows(s):
        idx_copy(i, s).wait()

        @pl.when(i + 1 < n_steps)
        def _():
            idx_copy(i + 1, 1 - s).start()

        def issue(group, carry):
            tok0 = pl.multiple_of(group * ISSUE_UNROLL, ISSUE_UNROLL)
            for u in range(ISSUE_UNROLL):
                for k in range(TOP_K):
                    pltpu.make_async_copy(h_ref.at[tok0 + u], xs_hbm.at[idx_smem[s * n_idx + k * t + tok0 + u]],
                                          sem_rows).start(priority=(u * TOP_K + k) % 2)
            return carry

        lax.fori_loop(0, t // ISSUE_UNROLL, issue, 0)

    for s in range(2):
        @pl.when(i % 2 == s)
        def _(s=s):
            scatter_rows(s)

    for k in range(TOP_K):
        pltpu.make_async_copy(h_ref, xs_hbm.at[pl.ds(0, t)], sem_rows).wait()


def _dispatch(h, dest_tiles, pad_start, pad_n, p_rows, t):
    n, sub, _ = h.shape
    return pl.pallas_call(
        _dispatch_kernel,
        out_shape=jax.ShapeDtypeStruct((p_rows, sub, LANES), F32),
        grid_spec=pltpu.PrefetchScalarGridSpec(
            num_scalar_prefetch=2,
            grid=(n // t,),
            in_specs=[
                pl.BlockSpec((t, sub, LANES), lambda i, p0, pn: (i, 0, 0)),
                pl.BlockSpec(memory_space=pl.ANY),
            ],
            out_specs=pl.BlockSpec(memory_space=pl.ANY),
            scratch_shapes=[pltpu.VMEM((sub, LANES), F32), pltpu.SMEM((2 * TOP_K * t,), jnp.int32),
                            pltpu.SemaphoreType.DMA((2,)), pltpu.SemaphoreType.DMA, pltpu.SemaphoreType.DMA],
        ),
        compiler_params=_cparams(("arbitrary",)),
        name="moe_dispatch",
    )(pad_start, pad_n, h, dest_tiles)


def _experts_kernel(be_ref, nv_ref, nu_ref, nx_ref, xs_ref, wgu_hbm, bgu_ref, wdn_hbm, bdn_ref, ys_ref,
                    wgu_f32, wdn_f32, wgu_bf, wdn_bf, sem_w, *, layer):
    b = pl.program_id(0)
    f = wdn_bf.shape[0]
    sub = xs_ref.shape[1]
    half = EXPERT_ROWS // 2
    e = be_ref[b]
    e_prev = be_ref[jnp.maximum(b - 1, 0)]
    used = b < nu_ref[0]

    def weight_copies(ex):
        return (pltpu.make_async_copy(wgu_hbm.at[layer, ex], wgu_f32, sem_w.at[0]),
                pltpu.make_async_copy(wdn_hbm.at[layer, ex], wdn_f32, sem_w.at[1]))

    @pl.when(b == 0)
    def _():
        for cp in weight_copies(e):
            cp.start()

    @pl.when(jnp.logical_and(used, jnp.logical_or(b == 0, e != e_prev)))
    def _():
        for cp in weight_copies(e):
            cp.wait()
        wgu_bf[...] = wgu_f32[...].astype(BF16)
        wdn_bf[...] = wdn_f32[...].astype(BF16)

        @pl.when(nx_ref[b] >= 0)
        def _():
            for cp in weight_copies(nx_ref[b]):
                cp.start()

    def mlp(rows):
        x = xs_ref[pl.ds(0, rows)].reshape(rows, sub * LANES).astype(BF16)
        gu = jnp.dot(x, wgu_bf[...], preferred_element_type=F32) + bgu_ref[0, 0]
        x_glu = jnp.minimum(gu[:, :f], SWIGLU_LIMIT)
        x_lin = jnp.clip(gu[:, f:], -SWIGLU_LIMIT, SWIGLU_LIMIT)
        act = x_glu * jax.nn.sigmoid(SWIGLU_ALPHA * x_glu) * (x_lin + 1.0)
        y = jnp.dot(act.astype(BF16), wdn_bf[...], preferred_element_type=F32) + bdn_ref[0, 0]
        ys_ref[pl.ds(0, rows)] = y.reshape(rows, sub, LANES)

    @pl.when(jnp.logical_and(used, nv_ref[b] > half))
    def _():
        mlp(EXPERT_ROWS)

    @pl.when(jnp.logical_and(used, nv_ref[b] <= half))
    def _():
        mlp(half)


def _experts(xs, block_e, n_valid, n_used, next_e, layer, w_gu, b_gu, w_dn, b_dn):
    p_rows, sub, _ = xs.shape
    n_l, n_e, d, f2 = w_gu.shape
    f = f2 // 2
    n_blocks = p_rows // EXPERT_ROWS
    row_map = lambda b, be, nv, nu, nx: (jnp.minimum(b, nu[0] - 1), 0, 0)
    exp_map = lambda b, be, nv, nu, nx: (layer, be[b], 0, 0)
    return pl.pallas_call(
        functools.partial(_experts_kernel, layer=layer),
        out_shape=jax.ShapeDtypeStruct((p_rows, sub, LANES), F32),
        grid_spec=pltpu.PrefetchScalarGridSpec(
            num_scalar_prefetch=4,
            grid=(n_blocks,),
            in_specs=[
                pl.BlockSpec((EXPERT_ROWS, sub, LANES), row_map),
                pl.BlockSpec(memory_space=pl.ANY),
                pl.BlockSpec((1, 1, 1, f2), exp_map),
                pl.BlockSpec(memory_space=pl.ANY),
                pl.BlockSpec((1, 1, 1, d), exp_map),
            ],
            out_specs=pl.BlockSpec((EXPERT_ROWS, sub, LANES), row_map),
            scratch_shapes=[pltpu.VMEM((d, f2), F32), pltpu.VMEM((f, d), F32),
                            pltpu.VMEM((d, f2), BF16), pltpu.VMEM((f, d), BF16),
                            pltpu.SemaphoreType.DMA((2,))],
        ),
        compiler_params=_cparams(("arbitrary",)),
        name="moe_experts",
    )(block_e, n_valid, n_used, next_e, xs, w_gu, b_gu.reshape(n_l, n_e, 1, f2), w_dn,
      b_dn.reshape(n_l, n_e, 1, d))


def _combine_kernel(x_ref, gate_ref, mod_ref, fng_ref, dest_hbm, ys_hbm, o_ref, buf, idx_smem, sem_idx, sem_rows,
                    *, final):
    t = x_ref.shape[0]
    i = pl.program_id(0)
    n_steps = pl.num_programs(0)
    slot = i % 2

    def idx_copy(step, s):
        return pltpu.make_async_copy(dest_hbm.at[step], idx_smem.at[pl.ds(s * TOP_K * t, TOP_K * t)],
                                     sem_idx.at[s])

    def start_rows(s):
        def issue(group, carry):
            tok0 = pl.multiple_of(group * ISSUE_UNROLL, ISSUE_UNROLL)
            for u in range(ISSUE_UNROLL):
                for k in range(TOP_K):
                    pltpu.make_async_copy(ys_hbm.at[idx_smem[(s * TOP_K + k) * t + tok0 + u]],
                                          buf.at[s * TOP_K + k, tok0 + u],
                                          sem_rows.at[s]).start(priority=(u * TOP_K + k) % 2)
            return carry

        lax.fori_loop(0, t // ISSUE_UNROLL, issue, 0)

    @pl.when(i == 0)
    def _():
        first = idx_copy(0, 0)
        first.start()
        first.wait()
        start_rows(0)

    for nxt in range(2):
        @pl.when(jnp.logical_and(i + 1 < n_steps, slot == 1 - nxt))
        def _(nxt=nxt):
            idx_copy(i + 1, nxt).start()

    for k in range(TOP_K):
        pltpu.make_async_copy(ys_hbm.at[pl.ds(0, t)], buf.at[slot * TOP_K + k], sem_rows.at[slot]).wait()

    for nxt in range(2):
        @pl.when(jnp.logical_and(i + 1 < n_steps, slot == 1 - nxt))
        def _(nxt=nxt):
            idx_copy(i + 1, nxt).wait()
            start_rows(nxt)

    gate = gate_ref[...]
    y = gate[:, 0:1] * buf[slot * TOP_K].reshape(x_ref.shape)
    for k in range(1, TOP_K):
        y = y + gate[:, k:k + 1] * buf[slot * TOP_K + k].reshape(x_ref.shape)
    out = x_ref[...] + mod_ref[0][5:6, :] * y
    o_ref[...] = _rms(out, fng_ref[...]) if final else out


def _combine(x2, gates, mod, final_g, dest_tiles, ys, t, tiles_per_batch, final):
    n, d = x2.shape
    return pl.pallas_call(
        functools.partial(_combine_kernel, final=final),
        out_shape=jax.ShapeDtypeStruct((n, d), F32),
        grid=(n // t,),
        in_specs=[
            pl.BlockSpec((t, d), lambda i: (i, 0)),
            pl.BlockSpec((t, LANES), lambda i: (i, 0)),
            pl.BlockSpec((1, 6, d), lambda i: (i // tiles_per_batch, 0, 0)),
            pl.BlockSpec((1, d), lambda i: (0, 0)),
            pl.BlockSpec(memory_space=pl.ANY),
            pl.BlockSpec(memory_space=pl.ANY),
        ],
        out_specs=pl.BlockSpec((t, d), lambda i: (i, 0)),
        scratch_shapes=[pltpu.VMEM((2 * TOP_K, t, d // LANES, LANES), F32), pltpu.SMEM((2 * TOP_K * t,), jnp.int32),
                        pltpu.SemaphoreType.DMA((2,)), pltpu.SemaphoreType.DMA((2,))],
        compiler_params=_cparams(("arbitrary",)),
        name="moe_combine",
    )(x2, gates, mod, final_g.reshape(1, d), dest_tiles, ys)


def _moe_layer(x, routed, mod, layer, w_gu, b_gu, w_dn, b_dn, final_g, final, t):
    b, s, d = x.shape
    n = b * s
    x2 = x.reshape(n, d)
    tiles_per_batch = s // t
    h, gates, meta, cnt = routed
    counts = cnt[0, :N_EXPERTS].astype(jnp.int32)
    padded = (counts + EXPERT_ROWS - 1) // EXPERT_ROWS * EXPERT_ROWS
    pend = jnp.cumsum(padded)
    pstart = pend - padded
    n_blocks = -(-(n * TOP_K) // EXPERT_ROWS) + N_EXPERTS
    p_rows = n_blocks * EXPERT_ROWS
    rank = meta[:, :TOP_K, :]
    eidx = meta[:, TOP_K:, :]
    experts = jnp.arange(N_EXPERTS, dtype=jnp.int32)
    dest = rank + jnp.sum(jnp.where(eidx[..., None] == experts, pstart, 0), axis=-1)
    dest_tiles = dest.reshape(n // t, TOP_K * t)
    block_row0 = jnp.arange(n_blocks, dtype=jnp.int32) * EXPERT_ROWS
    block_e = jnp.minimum(jnp.sum((pend[None, :] <= block_row0[:, None]).astype(jnp.int32), axis=1),
                          N_EXPERTS - 1)
    n_used = (pend[-1:] // EXPERT_ROWS).astype(jnp.int32)
    row_end = jnp.sum(jnp.where(block_e[:, None] == experts, pstart + counts, 0), axis=1)
    n_valid = jnp.clip(row_end - block_row0, 0, EXPERT_ROWS).astype(jnp.int32)
    half = EXPERT_ROWS // 2
    pad_n = (counts + half - 1) // half * half - counts
    xs = _dispatch(h, dest_tiles, pstart + counts, pad_n, p_rows, t)
    block_id = jnp.arange(n_blocks, dtype=jnp.int32)
    later_other = jnp.logical_and(
        jnp.logical_and(block_id[None, :] > block_id[:, None], block_id[None, :] < n_used[0]),
        block_e[None, :] != block_e[:, None])
    first_other = jnp.min(jnp.where(later_other, block_id[None, :], n_blocks), axis=1)
    next_e = jnp.where(first_other < n_blocks, block_e[jnp.minimum(first_other, n_blocks - 1)], -1)
    ys = _experts(xs, block_e, n_valid, n_used, next_e.astype(jnp.int32), layer, w_gu, b_gu, w_dn, b_dn)
    out = _combine(x2, gates, mod, final_g, dest_tiles, ys, t, tiles_per_batch, final)
    return out.reshape(b, s, d)


def _kv_kernel(x_ref, mod_ref, ng_ref, wkv_ref, wf_ref, bf_ref, kg_ref, hsum_ref,
               kt_ref, v_ref, qf_ref, fb_ref, carry):
    t, d = x_ref.shape[1], x_ref.shape[2]
    n_pairs = N_HEADS // 2

    @pl.when(pl.program_id(1) == 0)
    def _():
        carry[...] = jnp.zeros(carry.shape, F32)

    mod = mod_ref[0]
    h = _rms(x_ref[0], ng_ref[...]) * (1.0 + mod[1:2, :]) + mod[0:1, :]
    h_hi = h.astype(BF16)
    kv = jnp.dot(h_hi, wkv_ref[...], preferred_element_type=F32)
    k = kv[:, :d]
    v_ref[0] = kv[:, d:].astype(BF16)
    ms = jnp.dot((k * k).astype(BF16), hsum_ref[...], preferred_element_type=F32)
    k = k * lax.rsqrt(ms + EPS) * kg_ref[...]
    kt = k.T
    for hp in range(n_pairs):
        kt_ref[0, hp, 0, pl.ds(0, LANES), :] = kt[hp * LANES:(hp + 1) * LANES, :].astype(BF16)

    h_lo = (h - h_hi.astype(F32)).astype(BF16)
    fz2 = jnp.dot(h_hi, wf_ref[...], preferred_element_type=F32)
    fz = (fz2[:, :LANES] + fz2[:, LANES:] + bf_ref[...]
          + jnp.dot(h_lo, wf_ref[:, pl.ds(0, LANES)], preferred_element_type=F32))
    ls = jax.nn.log_sigmoid(fz)
    r_i = lax.broadcasted_iota(jnp.int32, (t, t), 0)
    c_i = lax.broadcasted_iota(jnp.int32, (t, t), 1)
    tri = (c_i <= r_i).astype(BF16)
    ls_hi, ls_mid, ls_lo = _split3(ls)
    cum2 = jnp.dot(tri, jnp.concatenate([ls_hi, ls_mid], axis=1).astype(BF16), preferred_element_type=F32)
    cum = (cum2[:, :LANES] + cum2[:, LANES:] + carry[0:1, :]
           + jnp.dot(tri, ls_lo.astype(BF16), preferred_element_type=F32))
    carry[...] = jnp.broadcast_to(cum[t - 1:t, :], carry.shape)
    f2 = cum * LOG2E
    row8 = lax.broadcasted_iota(jnp.int32, (8, LANES), 0)
    fb_ref[0, 0] = jnp.where(row8 == 0, f2[0:1, :], jnp.where(row8 == 1, f2[t - 1:t, :], 0.0))
    f2t = f2.T
    lane = lax.broadcasted_iota(jnp.int32, (t, LANES), 1)
    sub = lax.broadcasted_iota(jnp.int32, (LANES, t), 0)
    q_pieces = _split3(f2)
    k_pieces = _split3(-f2t)
    for hp in range(n_pairs):
        q_aug = jnp.zeros((t, LANES), F32)
        k_aug = jnp.zeros((LANES, t), F32)
        for hh in range(2):
            head = 2 * hp + hh
            o = hh * 2 * N_FPIECES
            for p in range(N_FPIECES):
                q_aug = jnp.where(lane == o + p, q_pieces[p][:, head:head + 1], q_aug)
                q_aug = jnp.where(lane == o + N_FPIECES + p, 1.0, q_aug)
                k_aug = jnp.where(sub == o + p, 1.0, k_aug)
                k_aug = jnp.where(sub == o + N_FPIECES + p, k_pieces[p][head:head + 1, :], k_aug)
        qf_ref[0, hp] = q_aug.astype(BF16)
        kt_ref[0, hp, 0, pl.ds(LANES, LANES), :] = k_aug.astype(BF16)


def _shared_kv(x, kvmod, ng, w_kvf, b_f, k_norm_g, t):
    b, s, d = x.shape
    n_pairs = N_HEADS // 2
    wkv = w_kvf[:, :2 * d].astype(BF16)
    wf = jnp.zeros((d, LANES), F32).at[:, :N_HEADS].set(w_kvf[:, 2 * d:])
    wf_hi = wf.astype(BF16)
    wf2 = jnp.concatenate([wf_hi, (wf - wf_hi.astype(F32)).astype(BF16)], axis=1)
    bf = jnp.zeros((1, LANES), F32).at[0, :N_HEADS].set(b_f)
    kg = jnp.tile(k_norm_g, N_HEADS).reshape(1, d)
    head_of = jnp.arange(d) // HEAD_DIM
    hsum = ((head_of[:, None] == head_of[None, :]).astype(F32) / HEAD_DIM).astype(BF16)
    const = lambda shape: pl.BlockSpec(shape, lambda bi, si: (0,) * len(shape))
    return pl.pallas_call(
        _kv_kernel,
        out_shape=(
            jax.ShapeDtypeStruct((b, n_pairs, s // t, 2 * LANES, t), BF16),
            jax.ShapeDtypeStruct((b, s, d), BF16),
            jax.ShapeDtypeStruct((b, n_pairs, s, LANES), BF16),
            jax.ShapeDtypeStruct((b, s // t, 8, LANES), F32),
        ),
        grid=(b, s // t),
        in_specs=[
            pl.BlockSpec((1, t, d), lambda bi, si: (bi, si, 0)),
            pl.BlockSpec((1, 2, d), lambda bi, si: (bi, 0, 0)),
            const((1, d)), const((d, 2 * d)), const((d, 2 * LANES)), const((1, LANES)), const((1, d)),
            const((d, d)),
        ],
        out_specs=(
            pl.BlockSpec((1, n_pairs, 1, 2 * LANES, t), lambda bi, si: (bi, 0, si, 0, 0)),
            pl.BlockSpec((1, t, d), lambda bi, si: (bi, si, 0)),
            pl.BlockSpec((1, n_pairs, t, LANES), lambda bi, si: (bi, 0, si, 0)),
            pl.BlockSpec((1, 1, 8, LANES), lambda bi, si: (bi, si, 0, 0)),
        ),
        scratch_shapes=[pltpu.VMEM((8, LANES), F32)],
        compiler_params=_cparams(("arbitrary", "arbitrary")),
        name="shared_kv",
    )(x, kvmod, ng.reshape(1, d), wkv, wf2, bf, kg, hsum), hsum


def _qg_kernel(x_ref, mod_ref, ng_ref, w_ref, qg_ref, hsum_ref, q_ref, g_ref):
    d = x_ref.shape[2]
    mod = mod_ref[0]
    h = _rms(x_ref[0], ng_ref[...]) * (1.0 + mod[1:2, :]) + mod[0:1, :]
    qg = jnp.dot(h.astype(BF16), w_ref[...], preferred_element_type=F32)
    q = qg[:, :d]
    ms = jnp.dot((q * q).astype(BF16), hsum_ref[...], preferred_element_type=F32)
    q = q * lax.rsqrt(ms + EPS) * qg_ref[...] * (LOG2E / math.sqrt(HEAD_DIM))
    q_ref[0] = q.astype(BF16)
    g_ref[0] = jax.nn.sigmoid(qg[:, d:]).astype(BF16)


def _qg(x, mod, ng, w_qg, q_norm_g, hsum, t):
    b, s, d = x.shape
    const = lambda shape: pl.BlockSpec(shape, lambda bi, si: (0,) * len(shape))
    tile = pl.BlockSpec((1, t, d), lambda bi, si: (bi, si, 0))
    return pl.pallas_call(
        _qg_kernel,
        out_shape=(jax.ShapeDtypeStruct((b, s, d), BF16), jax.ShapeDtypeStruct((b, s, d), BF16)),
        grid=(b, s // t),
        in_specs=[tile, pl.BlockSpec((1, 6, d), lambda bi, si: (bi, 0, 0)),
                  const((1, d)), const((d, 2 * d)), const((1, d)), const((d, d))],
        out_specs=(tile, tile),
        compiler_params=_cparams(("arbitrary", "arbitrary")),
        name="attn_qg",
    )(x, mod, ng.reshape(1, d), w_qg.astype(BF16), jnp.tile(q_norm_g, N_HEADS).reshape(1, d), hsum)


def _attn_kernel(*refs, online):
    def body(sub, carry):
        _attn_tile(*refs, online=online, sub=sub)
        return carry

    lax.fori_loop(0, Q_TILES_PER_STEP, body, 0)


def _attn_tile(j0_ref, q_ref, qf_ref, kt_ref, v_ref, o_ref, qa_scr, m_scr, l_scr, acc_scr, *, online, sub):
    tq = q_ref.shape[1] // Q_TILES_PER_STEP
    tk = kt_ref.shape[4]
    i = pl.program_id(2) * Q_TILES_PER_STEP + sub
    n_q = pl.num_programs(2) * Q_TILES_PER_STEP
    rows = pl.ds(pl.multiple_of(sub * tq, tq), tq)
    head0 = pl.program_id(0) * N_HEADS + 2 * pl.program_id(1)
    first = [j0_ref[(head0 + hh) * n_q + i] for hh in range(2)]
    first_both = jnp.maximum(first[0], first[1])
    lane = lax.broadcasted_iota(jnp.int32, (tq, LANES), 1)
    q2 = q_ref[0, rows, :]
    qf = qf_ref[0, 0, rows, :]
    zero = jnp.zeros((), BF16)
    n_aug = 2 * N_FPIECES
    qa_scr[0, :, pl.ds(0, LANES)] = jnp.where(lane < HEAD_DIM, q2, zero)
    qa_scr[0, :, pl.ds(LANES, LANES)] = jnp.where(lane < n_aug, qf, zero)
    qa_scr[1, :, pl.ds(0, LANES)] = jnp.where(lane >= HEAD_DIM, q2, zero)
    qa_scr[1, :, pl.ds(LANES, LANES)] = jnp.where(jnp.logical_and(lane >= n_aug, lane < 2 * n_aug), qf, zero)
    if online:
        m_scr[...] = jnp.full(m_scr.shape, NEG_BIG, F32)
    l_scr[...] = jnp.zeros(l_scr.shape, F32)
    acc_scr[...] = jnp.zeros(acc_scr.shape, F32)

    def scores(hh, j, masked):
        s = jnp.dot(qa_scr[hh], kt_ref[0, 0, j], preferred_element_type=F32)
        if masked:
            r_i = lax.broadcasted_iota(jnp.int32, (tq, tk), 0)
            c_i = lax.broadcasted_iota(jnp.int32, (tq, tk), 1)
            s = jnp.where(c_i <= r_i, s, NEG_BIG)
        return s

    def values(j):
        return v_ref[0, pl.ds(pl.multiple_of(j * tk, tk), tk), :]

    def tiles(js, masked, heads):
        for hh in heads:
            if online:
                for j in js:
                    s = scores(hh, j, masked)
                    m_prev = m_scr[hh]
                    m_new = jnp.maximum(m_prev, jnp.max(s, axis=-1, keepdims=True))
                    alpha = jnp.exp2(m_prev - m_new)
                    p = jnp.exp2(s - m_new[:, 0:1])
                    l_scr[hh] = alpha * l_scr[hh] + jnp.sum(p, axis=-1, keepdims=True)
                    acc_scr[hh] = alpha * acc_scr[hh] + jnp.dot(p.astype(BF16), values(j),
                                                                preferred_element_type=F32)
                    m_scr[hh] = m_new
            else:
                part, pv = l_scr[hh], acc_scr[hh]
                for j in js:
                    p = jnp.exp2(scores(hh, j, masked))
                    for c in range(tk // LANES):
                        part = part + p[:, c * LANES:(c + 1) * LANES]
                    pv = pv + jnp.dot(p.astype(BF16), values(j), preferred_element_type=F32)
                l_scr[hh], acc_scr[hh] = part, pv

    def one_tile(heads):
        def body(j, carry):
            tiles((j,), False, heads)
            return carry
        return body

    def tile_group(group, carry):
        j = first_both + KV_UNROLL * group
        tiles(tuple(j + u for u in range(KV_UNROLL)), False, (0, 1))
        return carry

    for hh in range(2):
        n_pairs_hh = (first_both - first[hh]) // 2

        def tile_pair(pair, carry, hh=hh):
            j = first[hh] + 2 * pair
            tiles((j, j + 1), False, (hh,))
            return carry

        lax.fori_loop(0, n_pairs_hh, tile_pair, 0)
        lax.fori_loop(first[hh] + 2 * n_pairs_hh, first_both, one_tile((hh,)), 0)
    n_groups = (i - first_both) // KV_UNROLL
    lax.fori_loop(0, n_groups, tile_group, 0)
    lax.fori_loop(first_both + KV_UNROLL * n_groups, i, one_tile((0, 1)), 0)
    tiles((i,), True, (0, 1))
    if online:
        l0, l1 = l_scr[0], l_scr[1]
    else:
        l0 = jnp.sum(l_scr[0], axis=-1, keepdims=True)
        l1 = jnp.sum(l_scr[1], axis=-1, keepdims=True)
    o_ref[0, rows, :] = jnp.where(lane < HEAD_DIM, acc_scr[0] / l0, acc_scr[1] / l1).astype(BF16)


def _attention(j0, q, qf, kt, v, t, online):
    b, s, d = q.shape
    n_pairs = N_HEADS // 2
    nkv = s // t
    tq_step = Q_TILES_PER_STEP * t
    return pl.pallas_call(
        functools.partial(_attn_kernel, online=online),
        out_shape=jax.ShapeDtypeStruct((b, s, d), BF16),
        grid_spec=pltpu.PrefetchScalarGridSpec(
            num_scalar_prefetch=1,
            grid=(b, n_pairs, s // tq_step),
            in_specs=[
                pl.BlockSpec((1, tq_step, LANES), lambda bi, hp, i, j0r: (bi, i, hp)),
                pl.BlockSpec((1, 1, tq_step, LANES), lambda bi, hp, i, j0r: (bi, hp, i, 0)),
                pl.BlockSpec((1, 1, nkv, 2 * LANES, t), lambda bi, hp, i, j0r: (bi, hp, 0, 0, 0)),
                pl.BlockSpec((1, s, LANES), lambda bi, hp, i, j0r: (bi, 0, hp)),
            ],
            out_specs=pl.BlockSpec((1, tq_step, LANES), lambda bi, hp, i, j0r: (bi, i, hp)),
            scratch_shapes=[
                pltpu.VMEM((2, t, 2 * LANES), BF16),
                pltpu.VMEM((2, t, LANES), F32),
                pltpu.VMEM((2, t, LANES), F32),
                pltpu.VMEM((2, t, LANES), F32),
            ],
        ),
        compiler_params=_cparams(("arbitrary", "arbitrary", "arbitrary")),
        name="fox_attention_online" if online else "fox_attention",
    )(j0, q, qf, kt, v)


def _attn_out_kernel(x_ref, o_ref, g_ref, mod_ref, w_ref, ng2_ref, rw_ref, rb_ref,
                     out_ref, h_ref, gate_ref, meta_ref, cnt_ref):
    mod = mod_ref[0]
    og = o_ref[0] * g_ref[0]
    y = jnp.dot(og, w_ref[...], preferred_element_type=F32)
    x_new = x_ref[0] + mod[2:3, :] * y
    out_ref[0] = x_new
    _route_tail(x_new, mod, ng2_ref, rw_ref, rb_ref, h_ref, gate_ref, meta_ref, cnt_ref)


def _attn_out(x, o, g, mod, w_o, route, t):
    b, s, d = x.shape
    tile = pl.BlockSpec((1, t, d), lambda bi, si: (bi, si, 0))
    r_ops, r_in, r_shapes, r_out = _route_plumbing(*route, b, s, d, t)
    outs = pl.pallas_call(
        _attn_out_kernel,
        out_shape=(jax.ShapeDtypeStruct((b, s, d), F32),) + r_shapes,
        grid=(b, s // t),
        in_specs=[tile, tile, tile, pl.BlockSpec((1, 6, d), lambda bi, si: (bi, 0, 0)),
                  pl.BlockSpec((d, d), lambda bi, si: (0, 0))] + r_in,
        out_specs=(tile,) + r_out,
        compiler_params=_cparams(("arbitrary", "arbitrary")),
        name="attn_out",
    )(x, o, g, mod, w_o.astype(BF16), *r_ops)
    return outs[0], outs[1:]


def _fox_layer(x, mod, ng, w_qg, q_norm_g, k_norm_g, w_o, kv, route, t):
    (kt, v, qf, fb), hsum = kv
    n_t = x.shape[1] // t
    q, g = _qg(x, mod, ng, w_qg, q_norm_g, hsum, t)
    bound = (HEAD_DIM * jnp.max(jnp.abs(q_norm_g)) * jnp.max(jnp.abs(k_norm_g))
             * (LOG2E / math.sqrt(HEAD_DIM)))
    f_first, f_last = fb[:, :, 0, :N_HEADS], fb[:, :, 1, :N_HEADS]
    best = (bound * BOUND_SLACK + f_first[:, :, None, :]) - f_last[:, None, :, :]
    before = jnp.arange(n_t)[None, :] < jnp.arange(n_t)[:, None]
    dead = jnp.logical_and(best < ZERO_WEIGHT_EXPONENT, before[None, :, :, None])
    j0 = jnp.sum(dead.astype(jnp.int32), axis=2).transpose(0, 2, 1).reshape(-1)
    o = lax.cond(bound <= DIRECT_EXP_LIMIT,
                 functools.partial(_attention, t=t, online=False),
                 lambda j0_, *rest: _attention(jnp.zeros_like(j0_), *rest, t=t, online=True),
                 j0, q, qf, kt, v)
    return _attn_out(x, o, g, mod, w_o, route, t)


def kernel(x, c, mod_w, mod_b, norm1_g, norm2_g, conv_w_pw1, conv_b_pw1, conv_w_dw, conv_b_dw, conv_ln_g, conv_ln_b, conv_w_pw2, conv_b_pw2, kv_mod_w, kv_mod_b, kv_norm_g, w_kvf, b_f, k_norm_g, attn_w_qg, q_norm_g, attn_w_o, moe_router_w, moe_router_b, moe_w_gu, moe_b_gu, moe_w_down, moe_b_down, final_norm_g):
    b, s, d = x.shape
    depth = mod_w.shape[0]
    n_a = conv_w_pw1.shape[0]
    t = min(512, s)
    c8 = jnp.zeros((8, d), F32).at[:b].set(c)
    mods = _mods(c8, mod_w, mod_b)[:, :b].reshape(depth, b, 6, d)
    kvmod = _mods(c8, kv_mod_w[None], kv_mod_b[None])[0, :b].reshape(b, 2, d)
    kv = None
    for l in range(depth):
        route = (norm2_g[l], moe_router_w[l], moe_router_b[l])
        if l < n_a:
            x, routed = _conv_layer(x, mods[l], norm1_g[l], conv_w_pw1[l], conv_b_pw1[l], conv_w_dw[l],
                                    conv_b_dw[l], conv_ln_g[l], conv_ln_b[l], conv_w_pw2[l],
                                    conv_b_pw2[l], route, t)
        else:
            lb = l - n_a
            x, routed = _fox_layer(x, mods[l], norm1_g[l], attn_w_qg[lb], q_norm_g[lb], k_norm_g,
                                   attn_w_o[lb], kv, route, t)
        x = _moe_layer(x, routed, mods[l], l, moe_w_gu, moe_b_gu, moe_w_down, moe_b_down,
                       final_norm_g, l == depth - 1, t)
        if l == n_a - 1:
            kv = _shared_kv(x, kvmod, kv_norm_g, w_kvf, b_f, k_norm_g, t)
    return x
```

```python
import functools
import math

import jax
import jax.numpy as jnp
from jax import lax
from jax.experimental import pallas as pl
from jax.experimental.pallas import tpu as pltpu

N_HEADS = 16
HEAD_DIM = 64
CONV_WIDTH = 31
N_EXPERTS = 32
TOP_K = 4
SWIGLU_ALPHA = 1.702
SWIGLU_LIMIT = 7.0
EPS = 1e-6

LANES = 128
HALO = 32
CONV_ROWS = 16
ISSUE_UNROLL = 4
KV_UNROLL = 8
Q_TILES_PER_STEP = 4
EXPERT_ROWS = 512
VMEM_LIMIT = 56 * 1024 * 1024
LOG2E = 1.4426950408889634
NEG_BIG = -1e30
N_FPIECES = 3
DIRECT_EXP_LIMIT = 60.0
ZERO_WEIGHT_EXPONENT = -160.0
BOUND_SLACK = 1.05

F32 = jnp.float32
BF16 = jnp.bfloat16
HIGHEST = lax.Precision.HIGHEST


def _cparams(sem):
    return pltpu.CompilerParams(dimension_semantics=sem, vmem_limit_bytes=VMEM_LIMIT)


def _rms(x, g):
    return x * lax.rsqrt(jnp.mean(x * x, axis=-1, keepdims=True) + EPS) * g


def _split3(f):
    hi = f.astype(BF16).astype(F32)
    r1 = f - hi
    mid = r1.astype(BF16).astype(F32)
    lo = (r1 - mid).astype(BF16).astype(F32)
    return hi, mid, lo


def _mods_kernel(c_ref, w_ref, b_ref, o_ref):
    c = c_ref[...]
    ca = c * jax.nn.sigmoid(c)
    o_ref[0] = jnp.dot(ca, w_ref[0], precision=HIGHEST, preferred_element_type=F32) + b_ref[0]


def _mods(c8, w, b):
    n_l, d, m = w.shape
    tn = min(m, 1024)
    return pl.pallas_call(
        _mods_kernel,
        out_shape=jax.ShapeDtypeStruct((n_l, 8, m), F32),
        grid=(n_l, m // tn),
        in_specs=[
            pl.BlockSpec((8, d), lambda l, j: (0, 0)),
            pl.BlockSpec((1, d, tn), lambda l, j: (l, 0, j)),
            pl.BlockSpec((1, 1, tn), lambda l, j: (l, 0, j)),
        ],
        out_specs=pl.BlockSpec((1, 8, tn), lambda l, j: (l, 0, j)),
        compiler_params=_cparams(("arbitrary", "arbitrary")),
        name="mods",
    )(c8, w, b.reshape(n_l, 1, m))


def _conv_kernel(x_ref, mod_ref, ng_ref, w1_ref, b1_ref, wdw_ref, bdw_ref, lng_ref, lnb_ref,
                 w2_ref, b2_ref, ng2_ref, rw_ref, rb_ref,
                 o_ref, h_ref, gate_ref, meta_ref, cnt_ref, ubuf, cbuf):
    t, d = x_ref.shape[1], x_ref.shape[2]
    slab = ubuf.shape[1:]

    @pl.when(pl.program_id(1) == 0)
    def _():
        ubuf[pl.ds(0, HALO)] = jnp.zeros((HALO,) + slab, F32)

    x = x_ref[0]
    mod = mod_ref[0]
    h = _rms(x, ng_ref[...]) * (1.0 + mod[1:2, :]) + mod[0:1, :]
    u = jnp.dot(h.astype(BF16), w1_ref[...], preferred_element_type=F32) + b1_ref[...]
    u = u[:, :d] * jax.nn.sigmoid(u[:, d:])
    ubuf[pl.ds(HALO, t)] = u.reshape((t,) + slab)

    def conv_rows(c, carry):
        base = pl.multiple_of(c * CONV_ROWS, CONV_ROWS)
        acc = jnp.zeros((CONV_ROWS,) + slab, F32)
        for j in range(CONV_WIDTH):
            acc = acc + ubuf[pl.ds(base + (HALO - (CONV_WIDTH - 1) + j), CONV_ROWS)] * wdw_ref[j]
        cbuf[pl.ds(base, CONV_ROWS)] = acc
        return carry

    lax.fori_loop(0, t // CONV_ROWS, conv_rows, 0)
    ubuf[pl.ds(0, HALO)] = ubuf[pl.ds(t, HALO)]
    acc = cbuf[...].reshape(t, d) + bdw_ref[...]
    mu = jnp.mean(acc, axis=-1, keepdims=True)
    cen = acc - mu
    var = jnp.mean(cen * cen, axis=-1, keepdims=True)
    y = cen * lax.rsqrt(var + EPS) * lng_ref[...] + lnb_ref[...]
    y = y * jax.nn.sigmoid(y)
    y = jnp.dot(y.astype(BF16), w2_ref[...], preferred_element_type=F32) + b2_ref[...]
    x_new = x + mod[2:3, :] * y
    o_ref[0] = x_new
    _route_tail(x_new, mod, ng2_ref, rw_ref, rb_ref, h_ref, gate_ref, meta_ref, cnt_ref)


def _conv_layer(x, mod, ng, w1, b1, wdw, bdw, lng, lnb, w2, b2, route, t):
    b, s, d = x.shape
    row = lambda a: a.reshape(1, -1)
    slab = (d // LANES, LANES)
    wdw_p = jnp.zeros((HALO, d), F32).at[:CONV_WIDTH].set(wdw).reshape((HALO,) + slab)
    const = lambda shape: pl.BlockSpec(shape, lambda bi, si: (0,) * len(shape))
    r_ops, r_in, r_shapes, r_out = _route_plumbing(*route, b, s, d, t)
    outs = pl.pallas_call(
        _conv_kernel,
        out_shape=(jax.ShapeDtypeStruct((b, s, d), F32),) + r_shapes,
        grid=(b, s // t),
        in_specs=[
            pl.BlockSpec((1, t, d), lambda bi, si: (bi, si, 0)),
            pl.BlockSpec((1, 6, d), lambda bi, si: (bi, 0, 0)),
            const((1, d)), const((d, 2 * d)), const((1, 2 * d)), const((HALO,) + slab), const((1, d)),
            const((1, d)), const((1, d)), const((d, d)), const((1, d)),
        ] + r_in,
        out_specs=(pl.BlockSpec((1, t, d), lambda bi, si: (bi, si, 0)),) + r_out,
        scratch_shapes=[pltpu.VMEM((t + HALO,) + slab, F32), pltpu.VMEM((t,) + slab, F32)],
        compiler_params=_cparams(("arbitrary", "arbitrary")),
        name="conv_layer",
    )(x, mod, row(ng), w1.astype(BF16), row(b1), wdw_p, row(bdw), row(lng), row(lnb),
      w2.astype(BF16), row(b2), *r_ops)
    return outs[0], outs[1:]


def _route_tail(x, mod, ng_ref, rw_ref, rb_ref, h_ref, gate_ref, meta_ref, cnt_ref):
    t = x.shape[0]

    @pl.when(jnp.logical_and(pl.program_id(0) == 0, pl.program_id(1) == 0))
    def _():
        cnt_ref[...] = jnp.zeros(cnt_ref.shape, F32)

    h = _rms(x, ng_ref[...]) * (1.0 + mod[4:5, :]) + mod[3:4, :]
    h_ref[...] = h.reshape(h_ref.shape)
    h_hi = h.astype(BF16)
    h_lo = (h - h_hi.astype(F32)).astype(BF16)
    lg2 = jnp.dot(h_hi, rw_ref[...], preferred_element_type=F32)
    logits = (lg2[:, :LANES] + lg2[:, LANES:] + rb_ref[...]
              + jnp.dot(h_lo, rw_ref[:, pl.ds(0, LANES)], preferred_element_type=F32))
    lane = lax.broadcasted_iota(jnp.int32, (t, LANES), 1).astype(F32)
    work = logits
    vals, idxs = [], []
    for _ in range(TOP_K):
        m = jnp.max(work, axis=-1, keepdims=True)
        idx = jnp.min(jnp.where(work == m, lane, float(LANES)), axis=-1, keepdims=True)
        vals.append(m)
        idxs.append(idx)
        work = jnp.where(lane == idx, -jnp.inf, work)
    exps = [jnp.exp(v - vals[0]) for v in vals]
    denom = exps[0] + exps[1] + exps[2] + exps[3]
    onehot = jnp.zeros((t, LANES), F32)
    for idx in idxs:
        onehot = onehot + (lane == idx).astype(F32)
    r_i = lax.broadcasted_iota(jnp.int32, (t, t), 0)
    c_i = lax.broadcasted_iota(jnp.int32, (t, t), 1)
    tri = (c_i < r_i).astype(BF16)
    base = jnp.dot(tri, onehot.astype(BF16), preferred_element_type=F32) + cnt_ref[0:1, :]
    gate_out = jnp.zeros((t, LANES), F32)
    meta = jnp.zeros((t, LANES), F32)
    for k in range(TOP_K):
        rank = jnp.sum(jnp.where(lane == idxs[k], base, 0.0), axis=-1, keepdims=True)
        gate_out = jnp.where(lane == k, exps[k] / denom, gate_out)
        meta = jnp.where(lane == k, rank, meta)
        meta = jnp.where(lane == TOP_K + k, idxs[k], meta)
    gate_ref[...] = gate_out
    meta_ref[0] = meta.T[0:2 * TOP_K, :].astype(jnp.int32)
    cnt_ref[...] = cnt_ref[...] + jnp.sum(onehot, axis=0, keepdims=True)


def _route_plumbing(ng, rw, rb, b, s, d, t):
    n = b * s
    tiles_per_batch = s // t
    rw_p = jnp.zeros((d, LANES), F32).at[:, :N_EXPERTS].set(rw)
    rw_hi = rw_p.astype(BF16)
    rw2 = jnp.concatenate([rw_hi, (rw_p - rw_hi.astype(F32)).astype(BF16)], axis=1)
    rb_p = jnp.full((1, LANES), NEG_BIG, F32).at[0, :N_EXPERTS].set(rb)
    const = lambda shape: pl.BlockSpec(shape, lambda bi, si: (0,) * len(shape))
    flat = lambda bi, si: bi * tiles_per_batch + si
    operands = (ng.reshape(1, d), rw2, rb_p)
    in_specs = [const((1, d)), const((d, 2 * LANES)), const((1, LANES))]
    out_shapes = (
        jax.ShapeDtypeStruct((n, d // LANES, LANES), F32),
        jax.ShapeDtypeStruct((n, LANES), F32),
        jax.ShapeDtypeStruct((n // t, 2 * TOP_K, t), jnp.int32),
        jax.ShapeDtypeStruct((8, LANES), F32),
    )
    out_specs = (
        pl.BlockSpec((t, d // LANES, LANES), lambda bi, si: (flat(bi, si), 0, 0)),
        pl.BlockSpec((t, LANES), lambda bi, si: (flat(bi, si), 0)),
        pl.BlockSpec((1, 2 * TOP_K, t), lambda bi, si: (flat(bi, si), 0, 0)),
        const((8, LANES)),
    )
    return operands, in_specs, out_shapes, out_specs


def _dispatch_kernel(pad0_ref, padn_ref, h_ref, dest_hbm, xs_hbm, zrow, idx_smem, sem_idx, sem_rows, sem_pad):
    t = h_ref.shape[0]
    i = pl.program_id(0)

    @pl.when(i == 0)
    def _():
        zrow[...] = jnp.zeros(zrow.shape, F32)
        for start in (True, False):
            def per_expert(e, carry, start=start):
                def per_row(r, c):
                    pad_cp = pltpu.make_async_copy(zrow, xs_hbm.at[pad0_ref[e] + r], sem_pad)
                    if start:
                        pad_cp.start()
                    else:
                        pad_cp.wait()
                    return c
                return lax.fori_loop(0, padn_ref[e], per_row, carry)
            lax.fori_loop(0, N_EXPERTS, per_expert, 0)

    n_steps = pl.num_programs(0)
    n_idx = TOP_K * t

    def idx_copy(step, s):
        return pltpu.make_async_copy(dest_hbm.at[step], idx_smem.at[pl.ds(s * n_idx, n_idx)], sem_idx.at[s])

    @pl.when(i == 0)
    def _():
        idx_copy(0, 0).start()

    def scatter_rows(s):
        idx_copy(i, s).wait()

        @pl.when(i + 1 < n_steps)
        def _():
            idx_copy(i + 1, 1 - s).start()

        def issue(group, carry):
            tok0 = pl.multiple_of(group * ISSUE_UNROLL, ISSUE_UNROLL)
            for u in range(ISSUE_UNROLL):
                for k in range(TOP_K):
                    pltpu.make_async_copy(h_ref.at[tok0 + u], xs_hbm.at[idx_smem[s * n_idx + k * t + tok0 + u]],
                                          sem_rows).start(priority=(u * TOP_K + k) % 2)
            return carry

        lax.fori_loop(0, t // ISSUE_UNROLL, issue, 0)

    for s in range(2):
        @pl.when(i % 2 == s)
        def _(s=s):
            scatter_rows(s)

    for k in range(TOP_K):
        pltpu.make_async_copy(h_ref, xs_hbm.at[pl.ds(0, t)], sem_rows).wait()


def _dispatch(h, dest_tiles, pad_start, pad_n, p_rows, t):
    n, sub, _ = h.shape
    return pl.pallas_call(
        _dispatch_kernel,
        out_shape=jax.ShapeDtypeStruct((p_rows, sub, LANES), F32),
        grid_spec=pltpu.PrefetchScalarGridSpec(
            num_scalar_prefetch=2,
            grid=(n // t,),
            in_specs=[
                pl.BlockSpec((t, sub, LANES), lambda i, p0, pn: (i, 0, 0)),
                pl.BlockSpec(memory_space=pl.ANY),
            ],
            out_specs=pl.BlockSpec(memory_space=pl.ANY),
            scratch_shapes=[pltpu.VMEM((sub, LANES), F32), pltpu.SMEM((2 * TOP_K * t,), jnp.int32),
                            pltpu.SemaphoreType.DMA((2,)), pltpu.SemaphoreType.DMA, pltpu.SemaphoreType.DMA],
        ),
        compiler_params=_cparams(("arbitrary",)),
        name="moe_dispatch",
    )(pad_start, pad_n, h, dest_tiles)


def _experts_kernel(be_ref, nv_ref, nu_ref, nx_ref, xs_ref, wgu_hbm, bgu_ref, wdn_hbm, bdn_ref, ys_ref,
                    wgu_f32, wdn_f32, wgu_bf, wdn_bf, sem_w, *, layer):
    b = pl.program_id(0)
    f = wdn_bf.shape[0]
    sub = xs_ref.shape[1]
    half = EXPERT_ROWS // 2
    e = be_ref[b]
    e_prev = be_ref[jnp.maximum(b - 1, 0)]
    used = b < nu_ref[0]

    def weight_copies(ex):
        return (pltpu.make_async_copy(wgu_hbm.at[layer, ex], wgu_f32, sem_w.at[0]),
                pltpu.make_async_copy(wdn_hbm.at[layer, ex], wdn_f32, sem_w.at[1]))

    @pl.when(b == 0)
    def _():
        for cp in weight_copies(e):
            cp.start()

    @pl.when(jnp.logical_and(used, jnp.logical_or(b == 0, e != e_prev)))
    def _():
        for cp in weight_copies(e):
            cp.wait()
        wgu_bf[...] = wgu_f32[...].astype(BF16)
        wdn_bf[...] = wdn_f32[...].astype(BF16)

        @pl.when(nx_ref[b] >= 0)
        def _():
            for cp in weight_copies(nx_ref[b]):
                cp.start()

    def mlp(rows):
        x = xs_ref[pl.ds(0, rows)].reshape(rows, sub * LANES).astype(BF16)
        gu = jnp.dot(x, wgu_bf[...], preferred_element_type=F32) + bgu_ref[0, 0]
        x_glu = jnp.minimum(gu[:, :f], SWIGLU_LIMIT)
        x_lin = jnp.clip(gu[:, f:], -SWIGLU_LIMIT, SWIGLU_LIMIT)
        act = x_glu * jax.nn.sigmoid(SWIGLU_ALPHA * x_glu) * (x_lin + 1.0)
        y = jnp.dot(act.astype(BF16), wdn_bf[...], preferred_element_type=F32) + bdn_ref[0, 0]
        ys_ref[pl.ds(0, rows)] = y.reshape(rows, sub, LANES)

    @pl.when(jnp.logical_and(used, nv_ref[b] > half))
    def _():
        mlp(EXPERT_ROWS)

    @pl.when(jnp.logical_and(used, nv_ref[b] <= half))
    def _():
        mlp(half)


def _experts(xs, block_e, n_valid, n_used, next_e, layer, w_gu, b_gu, w_dn, b_dn):
    p_rows, sub, _ = xs.shape
    n_l, n_e, d, f2 = w_gu.shape
    f = f2 // 2
    n_blocks = p_rows // EXPERT_ROWS
    row_map = lambda b, be, nv, nu, nx: (jnp.minimum(b, nu[0] - 1), 0, 0)
    exp_map = lambda b, be, nv, nu, nx: (layer, be[b], 0, 0)
    return pl.pallas_call(
        functools.partial(_experts_kernel, layer=layer),
        out_shape=jax.ShapeDtypeStruct((p_rows, sub, LANES), F32),
        grid_spec=pltpu.PrefetchScalarGridSpec(
            num_scalar_prefetch=4,
            grid=(n_blocks,),
            in_specs=[
                pl.BlockSpec((EXPERT_ROWS, sub, LANES), row_map),
                pl.BlockSpec(memory_space=pl.ANY),
                pl.BlockSpec((1, 1, 1, f2), exp_map),
                pl.BlockSpec(memory_space=pl.ANY),
                pl.BlockSpec((1, 1, 1, d), exp_map),
            ],
            out_specs=pl.BlockSpec((EXPERT_ROWS, sub, LANES), row_map),
            scratch_shapes=[pltpu.VMEM((d, f2), F32), pltpu.VMEM((f, d), F32),
                            pltpu.VMEM((d, f2), BF16), pltpu.VMEM((f, d), BF16),
                            pltpu.SemaphoreType.DMA((2,))],
        ),
        compiler_params=_cparams(("arbitrary",)),
        name="moe_experts",
    )(block_e, n_valid, n_used, next_e, xs, w_gu, b_gu.reshape(n_l, n_e, 1, f2), w_dn,
      b_dn.reshape(n_l, n_e, 1, d))


def _combine_kernel(x_ref, gate_ref, mod_ref, fng_ref, dest_hbm, ys_hbm, o_ref, buf, idx_smem, sem_idx, sem_rows,
                    *, final):
    t = x_ref.shape[0]
    i = pl.program_id(0)
    n_steps = pl.num_programs(0)
    slot = i % 2

    def idx_copy(step, s):
        return pltpu.make_async_copy(dest_hbm.at[step], idx_smem.at[pl.ds(s * TOP_K * t, TOP_K * t)],
                                     sem_idx.at[s])

    def start_rows(s):
        def issue(group, carry):
            tok0 = pl.multiple_of(group * ISSUE_UNROLL, ISSUE_UNROLL)
            for u in range(ISSUE_UNROLL):
                for k in range(TOP_K):
                    pltpu.make_async_copy(ys_hbm.at[idx_smem[(s * TOP_K + k) * t + tok0 + u]],
                                          buf.at[s * TOP_K + k, tok0 + u],
                                          sem_rows.at[s]).start(priority=(u * TOP_K + k) % 2)
            return carry

        lax.fori_loop(0, t // ISSUE_UNROLL, issue, 0)

    @pl.when(i == 0)
    def _():
        first = idx_copy(0, 0)
        first.start()
        first.wait()
        start_rows(0)

    for nxt in range(2):
        @pl.when(jnp.logical_and(i + 1 < n_steps, slot == 1 - nxt))
        def _(nxt=nxt):
            idx_copy(i + 1, nxt).start()

    for k in range(TOP_K):
        pltpu.make_async_copy(ys_hbm.at[pl.ds(0, t)], buf.at[slot * TOP_K + k], sem_rows.at[slot]).wait()

    for nxt in range(2):
        @pl.when(jnp.logical_and(i + 1 < n_steps, slot == 1 - nxt))
        def _(nxt=nxt):
            idx_copy(i + 1, nxt).wait()
            start_rows(nxt)

    gate = gate_ref[...]
    y = gate[:, 0:1] * buf[slot * TOP_K].reshape(x_ref.shape)
    for k in range(1, TOP_K):
        y = y + gate[:, k:k + 1] * buf[slot * TOP_K + k].reshape(x_ref.shape)
    out = x_ref[...] + mod_ref[0][5:6, :] * y
    o_ref[...] = _rms(out, fng_ref[...]) if final else out


def _combine(x2, gates, mod, final_g, dest_tiles, ys, t, tiles_per_batch, final):
    n, d = x2.shape
    return pl.pallas_call(
        functools.partial(_combine_kernel, final=final),
        out_shape=jax.ShapeDtypeStruct((n, d), F32),
        grid=(n // t,),
        in_specs=[
            pl.BlockSpec((t, d), lambda i: (i, 0)),
            pl.BlockSpec((t, LANES), lambda i: (i, 0)),
            pl.BlockSpec((1, 6, d), lambda i: (i // tiles_per_batch, 0, 0)),
            pl.BlockSpec((1, d), lambda i: (0, 0)),
            pl.BlockSpec(memory_space=pl.ANY),
            pl.BlockSpec(memory_space=pl.ANY),
        ],
        out_specs=pl.BlockSpec((t, d), lambda i: (i, 0)),
        scratch_shapes=[pltpu.VMEM((2 * TOP_K, t, d // LANES, LANES), F32), pltpu.SMEM((2 * TOP_K * t,), jnp.int32),
                        pltpu.SemaphoreType.DMA((2,)), pltpu.SemaphoreType.DMA((2,))],
        compiler_params=_cparams(("arbitrary",)),
        name="moe_combine",
    )(x2, gates, mod, final_g.reshape(1, d), dest_tiles, ys)


def _moe_layer(x, routed, mod, layer, w_gu, b_gu, w_dn, b_dn, final_g, final, t):
    b, s, d = x.shape
    n = b * s
    x2 = x.reshape(n, d)
    tiles_per_batch = s // t
    h, gates, meta, cnt = routed
    counts = cnt[0, :N_EXPERTS].astype(jnp.int32)
    padded = (counts + EXPERT_ROWS - 1) // EXPERT_ROWS * EXPERT_ROWS
    pend = jnp.cumsum(padded)
    pstart = pend - padded
    n_blocks = -(-(n * TOP_K) // EXPERT_ROWS) + N_EXPERTS
    p_rows = n_blocks * EXPERT_ROWS
    rank = meta[:, :TOP_K, :]
    eidx = meta[:, TOP_K:, :]
    experts = jnp.arange(N_EXPERTS, dtype=jnp.int32)
    dest = rank + jnp.sum(jnp.where(eidx[..., None] == experts, pstart, 0), axis=-1)
    dest_tiles = dest.reshape(n // t, TOP_K * t)
    block_row0 = jnp.arange(n_blocks, dtype=jnp.int32) * EXPERT_ROWS
    block_e = jnp.minimum(jnp.sum((pend[None, :] <= block_row0[:, None]).astype(jnp.int32), axis=1),
                          N_EXPERTS - 1)
    n_used = (pend[-1:] // EXPERT_ROWS).astype(jnp.int32)
    row_end = jnp.sum(jnp.where(block_e[:, None] == experts, pstart + counts, 0), axis=1)
    n_valid = jnp.clip(row_end - block_row0, 0, EXPERT_ROWS).astype(jnp.int32)
    half = EXPERT_ROWS // 2
    pad_n = (counts + half - 1) // half * half - counts
    xs = _dispatch(h, dest_tiles, pstart + counts, pad_n, p_rows, t)
    block_id = jnp.arange(n_blocks, dtype=jnp.int32)
    later_other = jnp.logical_and(
        jnp.logical_and(block_id[None, :] > block_id[:, None], block_id[None, :] < n_used[0]),
        block_e[None, :] != block_e[:, None])
    first_other = jnp.min(jnp.where(later_other, block_id[None, :], n_blocks), axis=1)
    next_e = jnp.where(first_other < n_blocks, block_e[jnp.minimum(first_other, n_blocks - 1)], -1)
    ys = _experts(xs, block_e, n_valid, n_used, next_e.astype(jnp.int32), layer, w_gu, b_gu, w_dn, b_dn)
    out = _combine(x2, gates, mod, final_g, dest_tiles, ys, t, tiles_per_batch, final)
    return out.reshape(b, s, d)


def _kv_kernel(x_ref, mod_ref, ng_ref, wkv_ref, wf_ref, bf_ref, kg_ref, hsum_ref,
               kt_ref, v_ref, qf_ref, fb_ref, carry):
    t, d = x_ref.shape[1], x_ref.shape[2]
    n_pairs = N_HEADS // 2

    @pl.when(pl.program_id(1) == 0)
    def _():
        carry[...] = jnp.zeros(carry.shape, F32)

    mod = mod_ref[0]
    h = _rms(x_ref[0], ng_ref[...]) * (1.0 + mod[1:2, :]) + mod[0:1, :]
    h_hi = h.astype(BF16)
    kv = jnp.dot(h_hi, wkv_ref[...], preferred_element_type=F32)
    k = kv[:, :d]
    v_ref[0] = kv[:, d:].astype(BF16)
    ms = jnp.dot((k * k).astype(BF16), hsum_ref[...], preferred_element_type=F32)
    k = k * lax.rsqrt(ms + EPS) * kg_ref[...]
    kt = k.T
    for hp in range(n_pairs):
        kt_ref[0, hp, 0, pl.ds(0, LANES), :] = kt[hp * LANES:(hp + 1) * LANES, :].astype(BF16)

    h_lo = (h - h_hi.astype(F32)).astype(BF16)
    fz2 = jnp.dot(h_hi, wf_ref[...], preferred_element_type=F32)
    fz = (fz2[:, :LANES] + fz2[:, LANES:] + bf_ref[...]
          + jnp.dot(h_lo, wf_ref[:, pl.ds(0, LANES)], preferred_element_type=F32))
    ls = jax.nn.log_sigmoid(fz)
    r_i = lax.broadcasted_iota(jnp.int32, (t, t), 0)
    c_i = lax.broadcasted_iota(jnp.int32, (t, t), 1)
    tri = (c_i <= r_i).astype(BF16)
    ls_hi, ls_mid, ls_lo = _split3(ls)
    cum2 = jnp.dot(tri, jnp.concatenate([ls_hi, ls_mid], axis=1).astype(BF16), preferred_element_type=F32)
    cum = (cum2[:, :LANES] + cum2[:, LANES:] + carry[0:1, :]
           + jnp.dot(tri, ls_lo.astype(BF16), preferred_element_type=F32))
    carry[...] = jnp.broadcast_to(cum[t - 1:t, :], carry.shape)
    f2 = cum * LOG2E
    row8 = lax.broadcasted_iota(jnp.int32, (8, LANES), 0)
    fb_ref[0, 0] = jnp.where(row8 == 0, f2[0:1, :], jnp.where(row8 == 1, f2[t - 1:t, :], 0.0))
    f2t = f2.T
    lane = lax.broadcasted_iota(jnp.int32, (t, LANES), 1)
    sub = lax.broadcasted_iota(jnp.int32, (LANES, t), 0)
    q_pieces = _split3(f2)
    k_pieces = _split3(-f2t)
    for hp in range(n_pairs):
        q_aug = jnp.zeros((t, LANES), F32)
        k_aug = jnp.zeros((LANES, t), F32)
        for hh in range(2):
            head = 2 * hp + hh
            o = hh * 2 * N_FPIECES
            for p in range(N_FPIECES):
                q_aug = jnp.where(lane == o + p, q_pieces[p][:, head:head + 1], q_aug)
                q_aug = jnp.where(lane == o + N_FPIECES + p, 1.0, q_aug)
                k_aug = jnp.where(sub == o + p, 1.0, k_aug)
                k_aug = jnp.where(sub == o + N_FPIECES + p, k_pieces[p][head:head + 1, :], k_aug)
        qf_ref[0, hp] = q_aug.astype(BF16)
        kt_ref[0, hp, 0, pl.ds(LANES, LANES), :] = k_aug.astype(BF16)


def _shared_kv(x, kvmod, ng, w_kvf, b_f, k_norm_g, t):
    b, s, d = x.shape
    n_pairs = N_HEADS // 2
    wkv = w_kvf[:, :2 * d].astype(BF16)
    wf = jnp.zeros((d, LANES), F32).at[:, :N_HEADS].set(w_kvf[:, 2 * d:])
    wf_hi = wf.astype(BF16)
    wf2 = jnp.concatenate([wf_hi, (wf - wf_hi.astype(F32)).astype(BF16)], axis=1)
    bf = jnp.zeros((1, LANES), F32).at[0, :N_HEADS].set(b_f)
    kg = jnp.tile(k_norm_g, N_HEADS).reshape(1, d)
    head_of = jnp.arange(d) // HEAD_DIM
    hsum = ((head_of[:, None] == head_of[None, :]).astype(F32) / HEAD_DIM).astype(BF16)
    const = lambda shape: pl.BlockSpec(shape, lambda bi, si: (0,) * len(shape))
    return pl.pallas_call(
        _kv_kernel,
        out_shape=(
            jax.ShapeDtypeStruct((b, n_pairs, s // t, 2 * LANES, t), BF16),
            jax.ShapeDtypeStruct((b, s, d), BF16),
            jax.ShapeDtypeStruct((b, n_pairs, s, LANES), BF16),
            jax.ShapeDtypeStruct((b, s // t, 8, LANES), F32),
        ),
        grid=(b, s // t),
        in_specs=[
            pl.BlockSpec((1, t, d), lambda bi, si: (bi, si, 0)),
            pl.BlockSpec((1, 2, d), lambda bi, si: (bi, 0, 0)),
            const((1, d)), const((d, 2 * d)), const((d, 2 * LANES)), const((1, LANES)), const((1, d)),
            const((d, d)),
        ],
        out_specs=(
            pl.BlockSpec((1, n_pairs, 1, 2 * LANES, t), lambda bi, si: (bi, 0, si, 0, 0)),
            pl.BlockSpec((1, t, d), lambda bi, si: (bi, si, 0)),
            pl.BlockSpec((1, n_pairs, t, LANES), lambda bi, si: (bi, 0, si, 0)),
            pl.BlockSpec((1, 1, 8, LANES), lambda bi, si: (bi, si, 0, 0)),
        ),
        scratch_shapes=[pltpu.VMEM((8, LANES), F32)],
        compiler_params=_cparams(("arbitrary", "arbitrary")),
        name="shared_kv",
    )(x, kvmod, ng.reshape(1, d), wkv, wf2, bf, kg, hsum), hsum


def _qg_kernel(x_ref, mod_ref, ng_ref, w_ref, qg_ref, hsum_ref, q_ref, g_ref):
    d = x_ref.shape[2]
    mod = mod_ref[0]
    h = _rms(x_ref[0], ng_ref[...]) * (1.0 + mod[1:2, :]) + mod[0:1, :]
    qg = jnp.dot(h.astype(BF16), w_ref[...], preferred_element_type=F32)
    q = qg[:, :d]
    ms = jnp.dot((q * q).astype(BF16), hsum_ref[...], preferred_element_type=F32)
    q = q * lax.rsqrt(ms + EPS) * qg_ref[...] * (LOG2E / math.sqrt(HEAD_DIM))
    q_ref[0] = q.astype(BF16)
    g_ref[0] = jax.nn.sigmoid(qg[:, d:]).astype(BF16)


def _qg(x, mod, ng, w_qg, q_norm_g, hsum, t):
    b, s, d = x.shape
    const = lambda shape: pl.BlockSpec(shape, lambda bi, si: (0,) * len(shape))
    tile = pl.BlockSpec((1, t, d), lambda bi, si: (bi, si, 0))
    return pl.pallas_call(
        _qg_kernel,
        out_shape=(jax.ShapeDtypeStruct((b, s, d), BF16), jax.ShapeDtypeStruct((b, s, d), BF16)),
        grid=(b, s // t),
        in_specs=[tile, pl.BlockSpec((1, 6, d), lambda bi, si: (bi, 0, 0)),
                  const((1, d)), const((d, 2 * d)), const((1, d)), const((d, d))],
        out_specs=(tile, tile),
        compiler_params=_cparams(("arbitrary", "arbitrary")),
        name="attn_qg",
    )(x, mod, ng.reshape(1, d), w_qg.astype(BF16), jnp.tile(q_norm_g, N_HEADS).reshape(1, d), hsum)


def _attn_kernel(*refs, online):
    def body(sub, carry):
        _attn_tile(*refs, online=online, sub=sub)
        return carry

    lax.fori_loop(0, Q_TILES_PER_STEP, body, 0)


def _attn_tile(j0_ref, q_ref, qf_ref, kt_ref, v_ref, o_ref, qa_scr, m_scr, l_scr, acc_scr, *, online, sub):
    tq = q_ref.shape[1] // Q_TILES_PER_STEP
    tk = kt_ref.shape[4]
    i = pl.program_id(2) * Q_TILES_PER_STEP + sub
    n_q = pl.num_programs(2) * Q_TILES_PER_STEP
    rows = pl.ds(pl.multiple_of(sub * tq, tq), tq)
    head0 = pl.program_id(0) * N_HEADS + 2 * pl.program_id(1)
    first = [j0_ref[(head0 + hh) * n_q + i] for hh in range(2)]
    first_both = jnp.maximum(first[0], first[1])
    lane = lax.broadcasted_iota(jnp.int32, (tq, LANES), 1)
    q2 = q_ref[0, rows, :]
    qf = qf_ref[0, 0, rows, :]
    zero = jnp.zeros((), BF16)
    n_aug = 2 * N_FPIECES
    qa_scr[0, :, pl.ds(0, LANES)] = jnp.where(lane < HEAD_DIM, q2, zero)
    qa_scr[0, :, pl.ds(LANES, LANES)] = jnp.where(lane < n_aug, qf, zero)
    qa_scr[1, :, pl.ds(0, LANES)] = jnp.where(lane >= HEAD_DIM, q2, zero)
    qa_scr[1, :, pl.ds(LANES, LANES)] = jnp.where(jnp.logical_and(lane >= n_aug, lane < 2 * n_aug), qf, zero)
    if online:
        m_scr[...] = jnp.full(m_scr.shape, NEG_BIG, F32)
    l_scr[...] = jnp.zeros(l_scr.shape, F32)
    acc_scr[...] = jnp.zeros(acc_scr.shape, F32)

    def scores(hh, j, masked):
        s = jnp.dot(qa_scr[hh], kt_ref[0, 0, j], preferred_element_type=F32)
        if masked:
            r_i = lax.broadcasted_iota(jnp.int32, (tq, tk), 0)
            c_i = lax.broadcasted_iota(jnp.int32, (tq, tk), 1)
            s = jnp.where(c_i <= r_i, s, NEG_BIG)
        return s

    def values(j):
        return v_ref[0, pl.ds(pl.multiple_of(j * tk, tk), tk), :]

    def tiles(js, masked, heads):
        for hh in heads:
            if online:
                for j in js:
                    s = scores(hh, j, masked)
                    m_prev = m_scr[hh]
                    m_new = jnp.maximum(m_prev, jnp.max(s, axis=-1, keepdims=True))
                    alpha = jnp.exp2(m_prev - m_new)
                    p = jnp.exp2(s - m_new[:, 0:1])
                    l_scr[hh] = alpha * l_scr[hh] + jnp.sum(p, axis=-1, keepdims=True)
                    acc_scr[hh] = alpha * acc_scr[hh] + jnp.dot(p.astype(BF16), values(j),
                                                                preferred_element_type=F32)
                    m_scr[hh] = m_new
            else:
                part, pv = l_scr[hh], acc_scr[hh]
                for j in js:
                    p = jnp.exp2(scores(hh, j, masked))
                    for c in range(tk // LANES):
                        part = part + p[:, c * LANES:(c + 1) * LANES]
                    pv = pv + jnp.dot(p.astype(BF16), values(j), preferred_element_type=F32)
                l_scr[hh], acc_scr[hh] = part, pv

    def one_tile(heads):
        def body(j, carry):
            tiles((j,), False, heads)
            return carry
        return body

    def tile_group(group, carry):
        j = first_both + KV_UNROLL * group
        tiles(tuple(j + u for u in range(KV_UNROLL)), False, (0, 1))
        return carry

    for hh in range(2):
        n_pairs_hh = (first_both - first[hh]) // 2

        def tile_pair(pair, carry, hh=hh):
            j = first[hh] + 2 * pair
            tiles((j, j + 1), False, (hh,))
            return carry

        lax.fori_loop(0, n_pairs_hh, tile_pair, 0)
        lax.fori_loop(first[hh] + 2 * n_pairs_hh, first_both, one_tile((hh,)), 0)
    n_groups = (i - first_both) // KV_UNROLL
    lax.fori_loop(0, n_groups, tile_group, 0)
    rest = first_both + KV_UNROLL * n_groups
    n_rest_pairs = (i - rest) // 2

    def rest_pair(pair, carry):
        j = rest + 2 * pair
        tiles((j, j + 1), False, (0, 1))
        return carry

    lax.fori_loop(0, n_rest_pairs, rest_pair, 0)
    lax.fori_loop(rest + 2 * n_rest_pairs, i, one_tile((0, 1)), 0)
    tiles((i,), True, (0, 1))
    if online:
        l0, l1 = l_scr[0], l_scr[1]
    else:
        l0 = jnp.sum(l_scr[0], axis=-1, keepdims=True)
        l1 = jnp.sum(l_scr[1], axis=-1, keepdims=True)
    o_ref[0, rows, :] = jnp.where(lane < HEAD_DIM, acc_scr[0] / l0, acc_scr[1] / l1).astype(BF16)


def _attention(j0, q, qf, kt, v, t, online):
    b, s, d = q.shape
    n_pairs = N_HEADS // 2
    nkv = s // t
    tq_step = Q_TILES_PER_STEP * t
    return pl.pallas_call(
        functools.partial(_attn_kernel, online=online),
        out_shape=jax.ShapeDtypeStruct((b, s, d), BF16),
        grid_spec=pltpu.PrefetchScalarGridSpec(
            num_scalar_prefetch=1,
            grid=(b, n_pairs, s // tq_step),
            in_specs=[
                pl.BlockSpec((1, tq_step, LANES), lambda bi, hp, i, j0r: (bi, i, hp)),
                pl.BlockSpec((1, 1, tq_step, LANES), lambda bi, hp, i, j0r: (bi, hp, i, 0)),
                pl.BlockSpec((1, 1, nkv, 2 * LANES, t), lambda bi, hp, i, j0r: (bi, hp, 0, 0, 0)),
                pl.BlockSpec((1, s, LANES), lambda bi, hp, i, j0r: (bi, 0, hp)),
            ],
            out_specs=pl.BlockSpec((1, tq_step, LANES), lambda bi, hp, i, j0r: (bi, i, hp)),
            scratch_shapes=[
                pltpu.VMEM((2, t, 2 * LANES), BF16),
                pltpu.VMEM((2, t, LANES), F32),
                pltpu.VMEM((2, t, LANES), F32),
                pltpu.VMEM((2, t, LANES), F32),
            ],
        ),
        compiler_params=_cparams(("arbitrary", "arbitrary", "arbitrary")),
        name="fox_attention_online" if online else "fox_attention",
    )(j0, q, qf, kt, v)


def _attn_out_kernel(x_ref, o_ref, g_ref, mod_ref, w_ref, ng2_ref, rw_ref, rb_ref,
                     out_ref, h_ref, gate_ref, meta_ref, cnt_ref):
    mod = mod_ref[0]
    og = o_ref[0] * g_ref[0]
    y = jnp.dot(og, w_ref[...], preferred_element_type=F32)
    x_new = x_ref[0] + mod[2:3, :] * y
    out_ref[0] = x_new
    _route_tail(x_new, mod, ng2_ref, rw_ref, rb_ref, h_ref, gate_ref, meta_ref, cnt_ref)


def _attn_out(x, o, g, mod, w_o, route, t):
    b, s, d = x.shape
    tile = pl.BlockSpec((1, t, d), lambda bi, si: (bi, si, 0))
    r_ops, r_in, r_shapes, r_out = _route_plumbing(*route, b, s, d, t)
    outs = pl.pallas_call(
        _attn_out_kernel,
        out_shape=(jax.ShapeDtypeStruct((b, s, d), F32),) + r_shapes,
        grid=(b, s // t),
        in_specs=[tile, tile, tile, pl.BlockSpec((1, 6, d), lambda bi, si: (bi, 0, 0)),
                  pl.BlockSpec((d, d), lambda bi, si: (0, 0))] + r_in,
        out_specs=(tile,) + r_out,
        compiler_params=_cparams(("arbitrary", "arbitrary")),
        name="attn_out",
    )(x, o, g, mod, w_o.astype(BF16), *r_ops)
    return outs[0], outs[1:]


def _fox_layer(x, mod, ng, w_qg, q_norm_g, k_norm_g, w_o, kv, route, t):
    (kt, v, qf, fb), hsum = kv
    n_t = x.shape[1] // t
    q, g = _qg(x, mod, ng, w_qg, q_norm_g, hsum, t)
    bound = (HEAD_DIM * jnp.max(jnp.abs(q_norm_g)) * jnp.max(jnp.abs(k_norm_g))
             * (LOG2E / math.sqrt(HEAD_DIM)))
    f_first, f_last = fb[:, :, 0, :N_HEADS], fb[:, :, 1, :N_HEADS]
    best = (bound * BOUND_SLACK + f_first[:, :, None, :]) - f_last[:, None, :, :]
    before = jnp.arange(n_t)[None, :] < jnp.arange(n_t)[:, None]
    dead = jnp.logical_and(best < ZERO_WEIGHT_EXPONENT, before[None, :, :, None])
    j0 = jnp.sum(dead.astype(jnp.int32), axis=2).transpose(0, 2, 1).reshape(-1)
    o = lax.cond(bound <= DIRECT_EXP_LIMIT,
                 functools.partial(_attention, t=t, online=False),
                 lambda j0_, *rest: _attention(jnp.zeros_like(j0_), *rest, t=t, online=True),
                 j0, q, qf, kt, v)
    return _attn_out(x, o, g, mod, w_o, route, t)


def kernel(x, c, mod_w, mod_b, norm1_g, norm2_g, conv_w_pw1, conv_b_pw1, conv_w_dw, conv_b_dw, conv_ln_g, conv_ln_b, conv_w_pw2, conv_b_pw2, kv_mod_w, kv_mod_b, kv_norm_g, w_kvf, b_f, k_norm_g, attn_w_qg, q_norm_g, attn_w_o, moe_router_w, moe_router_b, moe_w_gu, moe_b_gu, moe_w_down, moe_b_down, final_norm_g):
    b, s, d = x.shape
    depth = mod_w.shape[0]
    n_a = conv_w_pw1.shape[0]
    t = min(512, s)
    c8 = jnp.zeros((8, d), F32).at[:b].set(c)
    mods = _mods(c8, mod_w, mod_b)[:, :b].reshape(depth, b, 6, d)
    kvmod = _mods(c8, kv_mod_w[None], kv_mod_b[None])[0, :b].reshape(b, 2, d)
    kv = None
    for l in range(depth):
        route = (norm2_g[l], moe_router_w[l], moe_router_b[l])
        if l < n_a:
            x, routed = _conv_layer(x, mods[l], norm1_g[l], conv_w_pw1[l], conv_b_pw1[l], conv_w_dw[l],
                                    conv_b_dw[l], conv_ln_g[l], conv_ln_b[l], conv_w_pw2[l],
                                    conv_b_pw2[l], route, t)
        else:
            lb = l - n_a
            x, routed = _fox_layer(x, mods[l], norm1_g[l], attn_w_qg[lb], q_norm_g[lb], k_norm_g,
                                   attn_w_o[lb], kv, route, t)
        x = _moe_layer(x, routed, mods[l], l, moe_w_gu, moe_b_gu, moe_w_down, moe_b_down,
                       final_norm_g, l == depth - 1, t)
        if l == n_a - 1:
            kv = _shared_kv(x, kvmod, kv_norm_g, w_kvf, b_f, k_norm_g, t)
    return x
```

```python
import functools
import math

import jax
import jax.numpy as jnp
from jax import lax
from jax.experimental import pallas as pl
from jax.experimental.pallas import tpu as pltpu

N_HEADS = 16
HEAD_DIM = 64
CONV_WIDTH = 31
N_EXPERTS = 32
TOP_K = 4
SWIGLU_ALPHA = 1.702
SWIGLU_LIMIT = 7.0
EPS = 1e-6

LANES = 128
HALO = 32
CONV_ROWS = 16
ISSUE_UNROLL = 4
KV_UNROLL = 8
Q_TILES_PER_STEP = 4
EXPERT_ROWS = 512
VMEM_LIMIT = 56 * 1024 * 1024
LOG2E = 1.4426950408889634
NEG_BIG = -1e30
N_FPIECES = 3
DIRECT_EXP_LIMIT = 60.0
ZERO_WEIGHT_EXPONENT = -160.0
BOUND_SLACK = 1.05

F32 = jnp.float32
BF16 = jnp.bfloat16
HIGHEST = lax.Precision.HIGHEST


def _cparams(sem):
    return pltpu.CompilerParams(dimension_semantics=sem, vmem_limit_bytes=VMEM_LIMIT)


def _rms(x, g):
    return x * lax.rsqrt(jnp.mean(x * x, axis=-1, keepdims=True) + EPS) * g


def _split3(f):
    hi = f.astype(BF16).astype(F32)
    r1 = f - hi
    mid = r1.astype(BF16).astype(F32)
    lo = (r1 - mid).astype(BF16).astype(F32)
    return hi, mid, lo


def _mods_kernel(c_ref, w_ref, b_ref, o_ref):
    c = c_ref[...]
    ca = c * jax.nn.sigmoid(c)
    o_ref[0] = jnp.dot(ca, w_ref[0], precision=HIGHEST, preferred_element_type=F32) + b_ref[0]


def _mods(c8, w, b):
    n_l, d, m = w.shape
    tn = min(m, 1024)
    return pl.pallas_call(
        _mods_kernel,
        out_shape=jax.ShapeDtypeStruct((n_l, 8, m), F32),
        grid=(n_l, m // tn),
        in_specs=[
            pl.BlockSpec((8, d), lambda l, j: (0, 0)),
            pl.BlockSpec((1, d, tn), lambda l, j: (l, 0, j)),
            pl.BlockSpec((1, 1, tn), lambda l, j: (l, 0, j)),
        ],
        out_specs=pl.BlockSpec((1, 8, tn), lambda l, j: (l, 0, j)),
        compiler_params=_cparams(("arbitrary", "arbitrary")),
        name="mods",
    )(c8, w, b.reshape(n_l, 1, m))


def _conv_kernel(x_ref, mod_ref, ng_ref, w1_ref, b1_ref, wdw_ref, bdw_ref, lng_ref, lnb_ref,
                 w2_ref, b2_ref, ng2_ref, rw_ref, rb_ref,
                 o_ref, h_ref, gate_ref, meta_ref, cnt_ref, ubuf, cbuf):
    t, d = x_ref.shape[1], x_ref.shape[2]
    slab = ubuf.shape[1:]

    @pl.when(pl.program_id(1) == 0)
    def _():
        ubuf[pl.ds(0, HALO)] = jnp.zeros((HALO,) + slab, F32)

    x = x_ref[0]
    mod = mod_ref[0]
    h = _rms(x, ng_ref[...]) * (1.0 + mod[1:2, :]) + mod[0:1, :]
    u = jnp.dot(h.astype(BF16), w1_ref[...], preferred_element_type=F32) + b1_ref[...]
    u = u[:, :d] * jax.nn.sigmoid(u[:, d:])
    ubuf[pl.ds(HALO, t)] = u.reshape((t,) + slab)

    def conv_rows(c, carry):
        base = pl.multiple_of(c * CONV_ROWS, CONV_ROWS)
        acc = jnp.zeros((CONV_ROWS,) + slab, F32)
        for j in range(CONV_WIDTH):
            acc = acc + ubuf[pl.ds(base + (HALO - (CONV_WIDTH - 1) + j), CONV_ROWS)] * wdw_ref[j]
        cbuf[pl.ds(base, CONV_ROWS)] = acc
        return carry

    lax.fori_loop(0, t // CONV_ROWS, conv_rows, 0)
    ubuf[pl.ds(0, HALO)] = ubuf[pl.ds(t, HALO)]
    acc = cbuf[...].reshape(t, d) + bdw_ref[...]
    mu = jnp.mean(acc, axis=-1, keepdims=True)
    cen = acc - mu
    var = jnp.mean(cen * cen, axis=-1, keepdims=True)
    y = cen * lax.rsqrt(var + EPS) * lng_ref[...] + lnb_ref[...]
    y = y * jax.nn.sigmoid(y)
    y = jnp.dot(y.astype(BF16), w2_ref[...], preferred_element_type=F32) + b2_ref[...]
    x_new = x + mod[2:3, :] * y
    o_ref[0] = x_new
    _route_tail(x_new, mod, ng2_ref, rw_ref, rb_ref, h_ref, gate_ref, meta_ref, cnt_ref)


def _conv_layer(x, mod, ng, w1, b1, wdw, bdw, lng, lnb, w2, b2, route, t):
    b, s, d = x.shape
    row = lambda a: a.reshape(1, -1)
    slab = (d // LANES, LANES)
    wdw_p = jnp.zeros((HALO, d), F32).at[:CONV_WIDTH].set(wdw).reshape((HALO,) + slab)
    const = lambda shape: pl.BlockSpec(shape, lambda bi, si: (0,) * len(shape))
    r_ops, r_in, r_shapes, r_out = _route_plumbing(*route, b, s, d, t)
    outs = pl.pallas_call(
        _conv_kernel,
        out_shape=(jax.ShapeDtypeStruct((b, s, d), F32),) + r_shapes,
        grid=(b, s // t),
        in_specs=[
            pl.BlockSpec((1, t, d), lambda bi, si: (bi, si, 0)),
            pl.BlockSpec((1, 6, d), lambda bi, si: (bi, 0, 0)),
            const((1, d)), const((d, 2 * d)), const((1, 2 * d)), const((HALO,) + slab), const((1, d)),
            const((1, d)), const((1, d)), const((d, d)), const((1, d)),
        ] + r_in,
        out_specs=(pl.BlockSpec((1, t, d), lambda bi, si: (bi, si, 0)),) + r_out,
        scratch_shapes=[pltpu.VMEM((t + HALO,) + slab, F32), pltpu.VMEM((t,) + slab, F32)],
        compiler_params=_cparams(("arbitrary", "arbitrary")),
        name="conv_layer",
    )(x, mod, row(ng), w1.astype(BF16), row(b1), wdw_p, row(bdw), row(lng), row(lnb),
      w2.astype(BF16), row(b2), *r_ops)
    return outs[0], outs[1:]


def _route_tail(x, mod, ng_ref, rw_ref, rb_ref, h_ref, gate_ref, meta_ref, cnt_ref):
    t = x.shape[0]

    @pl.when(jnp.logical_and(pl.program_id(0) == 0, pl.program_id(1) == 0))
    def _():
        cnt_ref[...] = jnp.zeros(cnt_ref.shape, F32)

    h = _rms(x, ng_ref[...]) * (1.0 + mod[4:5, :]) + mod[3:4, :]
    h_ref[...] = h.reshape(h_ref.shape)
    h_hi = h.astype(BF16)
    h_lo = (h - h_hi.astype(F32)).astype(BF16)
    lg2 = jnp.dot(h_hi, rw_ref[...], preferred_element_type=F32)
    logits = (lg2[:, :LANES] + lg2[:, LANES:] + rb_ref[...]
              + jnp.dot(h_lo, rw_ref[:, pl.ds(0, LANES)], preferred_element_type=F32))
    lane = lax.broadcasted_iota(jnp.int32, (t, LANES), 1).astype(F32)
    work = logits
    vals, idxs = [], []
    for _ in range(TOP_K):
        m = jnp.max(work, axis=-1, keepdims=True)
        idx = jnp.min(jnp.where(work == m, lane, float(LANES)), axis=-1, keepdims=True)
        vals.append(m)
        idxs.append(idx)
        work = jnp.where(lane == idx, -jnp.inf, work)
    exps = [jnp.exp(v - vals[0]) for v in vals]
    denom = exps[0] + exps[1] + exps[2] + exps[3]
    onehot = jnp.zeros((t, LANES), F32)
    for idx in idxs:
        onehot = onehot + (lane == idx).astype(F32)
    r_i = lax.broadcasted_iota(jnp.int32, (t, t), 0)
    c_i = lax.broadcasted_iota(jnp.int32, (t, t), 1)
    tri = (c_i < r_i).astype(BF16)
    base = jnp.dot(tri, onehot.astype(BF16), preferred_element_type=F32) + cnt_ref[0:1, :]
    gate_out = jnp.zeros((t, LANES), F32)
    meta = jnp.zeros((t, LANES), F32)
    for k in range(TOP_K):
        rank = jnp.sum(jnp.where(lane == idxs[k], base, 0.0), axis=-1, keepdims=True)
        gate_out = jnp.where(lane == k, exps[k] / denom, gate_out)
        meta = jnp.where(lane == k, rank, meta)
        meta = jnp.where(lane == TOP_K + k, idxs[k], meta)
    gate_ref[...] = gate_out
    meta_ref[0] = meta.T[0:2 * TOP_K, :].astype(jnp.int32)
    cnt_ref[...] = cnt_ref[...] + jnp.sum(onehot, axis=0, keepdims=True)


def _route_plumbing(ng, rw, rb, b, s, d, t):
    n = b * s
    tiles_per_batch = s // t
    rw_p = jnp.zeros((d, LANES), F32).at[:, :N_EXPERTS].set(rw)
    rw_hi = rw_p.astype(BF16)
    rw2 = jnp.concatenate([rw_hi, (rw_p - rw_hi.astype(F32)).astype(BF16)], axis=1)
    rb_p = jnp.full((1, LANES), NEG_BIG, F32).at[0, :N_EXPERTS].set(rb)
    const = lambda shape: pl.BlockSpec(shape, lambda bi, si: (0,) * len(shape))
    flat = lambda bi, si: bi * tiles_per_batch + si
    operands = (ng.reshape(1, d), rw2, rb_p)
    in_specs = [const((1, d)), const((d, 2 * LANES)), const((1, LANES))]
    out_shapes = (
        jax.ShapeDtypeStruct((n, d // LANES, LANES), F32),
        jax.ShapeDtypeStruct((n, LANES), F32),
        jax.ShapeDtypeStruct((n // t, 2 * TOP_K, t), jnp.int32),
        jax.ShapeDtypeStruct((8, LANES), F32),
    )
    out_specs = (
        pl.BlockSpec((t, d // LANES, LANES), lambda bi, si: (flat(bi, si), 0, 0)),
        pl.BlockSpec((t, LANES), lambda bi, si: (flat(bi, si), 0)),
        pl.BlockSpec((1, 2 * TOP_K, t), lambda bi, si: (flat(bi, si), 0, 0)),
        const((8, LANES)),
    )
    return operands, in_specs, out_shapes, out_specs


def _dispatch_kernel(pad0_ref, padn_ref, h_ref, dest_hbm, xs_hbm, zrow, idx_smem, sem_idx, sem_rows, sem_pad):
    t = h_ref.shape[0]
    i = pl.program_id(0)

    @pl.when(i == 0)
    def _():
        zrow[...] = jnp.zeros(zrow.shape, F32)
        for start in (True, False):
            def per_expert(e, carry, start=start):
                def per_row(r, c):
                    pad_cp = pltpu.make_async_copy(zrow, xs_hbm.at[pad0_ref[e] + r], sem_pad)
                    if start:
                        pad_cp.start()
                    else:
                        pad_cp.wait()
                    return c
                return lax.fori_loop(0, padn_ref[e], per_row, carry)
            lax.fori_loop(0, N_EXPERTS, per_expert, 0)

    n_steps = pl.num_programs(0)
    n_idx = TOP_K * t

    def idx_copy(step, s):
        return pltpu.make_async_copy(dest_hbm.at[step], idx_smem.at[pl.ds(s * n_idx, n_idx)], sem_idx.at[s])

    @pl.when(i == 0)
    def _():
        idx_copy(0, 0).start()

    def scatter_rows(s):
        idx_copy(i, s).wait()

        @pl.when(i + 1 < n_steps)
        def _():
            idx_copy(i + 1, 1 - s).start()

        def issue(group, carry):
            tok0 = pl.multiple_of(group * ISSUE_UNROLL, ISSUE_UNROLL)
            for u in range(ISSUE_UNROLL):
                for k in range(TOP_K):
                    pltpu.make_async_copy(h_ref.at[tok0 + u], xs_hbm.at[idx_smem[s * n_idx + k * t + tok0 + u]],
                                          sem_rows).start(priority=(u * TOP_K + k) % 2)
            return carry

        lax.fori_loop(0, t // ISSUE_UNROLL, issue, 0)

    for s in range(2):
        @pl.when(i % 2 == s)
        def _(s=s):
            scatter_rows(s)

    for k in range(TOP_K):
        pltpu.make_async_copy(h_ref, xs_hbm.at[pl.ds(0, t)], sem_rows).wait()


def _dispatch(h, dest_tiles, pad_start, pad_n, p_rows, t):
    n, sub, _ = h.shape
    return pl.pallas_call(
        _dispatch_kernel,
        out_shape=jax.ShapeDtypeStruct((p_rows, sub, LANES), F32),
        grid_spec=pltpu.PrefetchScalarGridSpec(
            num_scalar_prefetch=2,
            grid=(n // t,),
            in_specs=[
                pl.BlockSpec((t, sub, LANES), lambda i, p0, pn: (i, 0, 0)),
                pl.BlockSpec(memory_space=pl.ANY),
            ],
            out_specs=pl.BlockSpec(memory_space=pl.ANY),
            scratch_shapes=[pltpu.VMEM((sub, LANES), F32), pltpu.SMEM((2 * TOP_K * t,), jnp.int32),
                            pltpu.SemaphoreType.DMA((2,)), pltpu.SemaphoreType.DMA, pltpu.SemaphoreType.DMA],
        ),
        compiler_params=_cparams(("arbitrary",)),
        name="moe_dispatch",
    )(pad_start, pad_n, h, dest_tiles)


def _experts_kernel(be_ref, nv_ref, nu_ref, nx_ref, xs_ref, wgu_hbm, bgu_ref, wdn_hbm, bdn_ref, ys_ref,
                    wgu_f32, wdn_f32, wgu_bf, wdn_bf, sem_w, *, layer):
    b = pl.program_id(0)
    f = wdn_bf.shape[0]
    sub = xs_ref.shape[1]
    half = EXPERT_ROWS // 2
    e = be_ref[b]
    e_prev = be_ref[jnp.maximum(b - 1, 0)]
    used = b < nu_ref[0]

    def weight_copies(ex):
        return (pltpu.make_async_copy(wgu_hbm.at[layer, ex], wgu_f32, sem_w.at[0]),
                pltpu.make_async_copy(wdn_hbm.at[layer, ex], wdn_f32, sem_w.at[1]))

    @pl.when(b == 0)
    def _():
        for cp in weight_copies(e):
            cp.start()

    @pl.when(jnp.logical_and(used, jnp.logical_or(b == 0, e != e_prev)))
    def _():
        for cp in weight_copies(e):
            cp.wait()
        wgu_bf[...] = wgu_f32[...].astype(BF16)
        wdn_bf[...] = wdn_f32[...].astype(BF16)

        @pl.when(nx_ref[b] >= 0)
        def _():
            for cp in weight_copies(nx_ref[b]):
                cp.start()

    def mlp(rows):
        x = xs_ref[pl.ds(0, rows)].reshape(rows, sub * LANES).astype(BF16)
        gu = jnp.dot(x, wgu_bf[...], preferred_element_type=F32) + bgu_ref[0, 0]
        x_glu = jnp.minimum(gu[:, :f], SWIGLU_LIMIT)
        x_lin = jnp.clip(gu[:, f:], -SWIGLU_LIMIT, SWIGLU_LIMIT)
        act = x_glu * jax.nn.sigmoid(SWIGLU_ALPHA * x_glu) * (x_lin + 1.0)
        y = jnp.dot(act.astype(BF16), wdn_bf[...], preferred_element_type=F32) + bdn_ref[0, 0]
        ys_ref[pl.ds(0, rows)] = y.reshape(rows, sub, LANES)

    @pl.when(jnp.logical_and(used, nv_ref[b] > half))
    def _():
        mlp(EXPERT_ROWS)

    @pl.when(jnp.logical_and(used, nv_ref[b] <= half))
    def _():
        mlp(half)


def _experts(xs, block_e, n_valid, n_used, next_e, layer, w_gu, b_gu, w_dn, b_dn):
    p_rows, sub, _ = xs.shape
    n_l, n_e, d, f2 = w_gu.shape
    f = f2 // 2
    n_blocks = p_rows // EXPERT_ROWS
    row_map = lambda b, be, nv, nu, nx: (jnp.minimum(b, nu[0] - 1), 0, 0)
    exp_map = lambda b, be, nv, nu, nx: (layer, be[b], 0, 0)
    return pl.pallas_call(
        functools.partial(_experts_kernel, layer=layer),
        out_shape=jax.ShapeDtypeStruct((p_rows, sub, LANES), F32),
        grid_spec=pltpu.PrefetchScalarGridSpec(
            num_scalar_prefetch=4,
            grid=(n_blocks,),
            in_specs=[
                pl.BlockSpec((EXPERT_ROWS, sub, LANES), row_map),
                pl.BlockSpec(memory_space=pl.ANY),
                pl.BlockSpec((1, 1, 1, f2), exp_map),
                pl.BlockSpec(memory_space=pl.ANY),
                pl.BlockSpec((1, 1, 1, d), exp_map),
            ],
            out_specs=pl.BlockSpec((EXPERT_ROWS, sub, LANES), row_map),
            scratch_shapes=[pltpu.VMEM((d, f2), F32), pltpu.VMEM((f, d), F32),
                            pltpu.VMEM((d, f2), BF16), pltpu.VMEM((f, d), BF16),
                            pltpu.SemaphoreType.DMA((2,))],
        ),
        compiler_params=_cparams(("arbitrary",)),
        name="moe_experts",
    )(block_e, n_valid, n_used, next_e, xs, w_gu, b_gu.reshape(n_l, n_e, 1, f2), w_dn,
      b_dn.reshape(n_l, n_e, 1, d))


def _combine_kernel(x_ref, gate_ref, mod_ref, fng_ref, dest_hbm, ys_hbm, o_ref, buf, idx_smem, sem_idx, sem_rows,
                    *, final):
    t = x_ref.shape[0]
    i = pl.program_id(0)
    n_steps = pl.num_programs(0)
    slot = i % 2

    def idx_copy(step, s):
        return pltpu.make_async_copy(dest_hbm.at[step], idx_smem.at[pl.ds(s * TOP_K * t, TOP_K * t)],
                                     sem_idx.at[s])

    def start_rows(s):
        def issue(group, carry):
            tok0 = pl.multiple_of(group * ISSUE_UNROLL, ISSUE_UNROLL)
            for u in range(ISSUE_UNROLL):
                for k in range(TOP_K):
                    pltpu.make_async_copy(ys_hbm.at[idx_smem[(s * TOP_K + k) * t + tok0 + u]],
                                          buf.at[s * TOP_K + k, tok0 + u],
                                          sem_rows.at[s]).start(priority=(u * TOP_K + k) % 2)
            return carry

        lax.fori_loop(0, t // ISSUE_UNROLL, issue, 0)

    @pl.when(i == 0)
    def _():
        first = idx_copy(0, 0)
        first.start()
        first.wait()
        start_rows(0)

    for nxt in range(2):
        @pl.when(jnp.logical_and(i + 1 < n_steps, slot == 1 - nxt))
        def _(nxt=nxt):
            idx_copy(i + 1, nxt).start()

    for k in range(TOP_K):
        pltpu.make_async_copy(ys_hbm.at[pl.ds(0, t)], buf.at[slot * TOP_K + k], sem_rows.at[slot]).wait()

    for nxt in range(2):
        @pl.when(jnp.logical_and(i + 1 < n_steps, slot == 1 - nxt))
        def _(nxt=nxt):
            idx_copy(i + 1, nxt).wait()
            start_rows(nxt)

    gate = gate_ref[...]
    y = gate[:, 0:1] * buf[slot * TOP_K].reshape(x_ref.shape)
    for k in range(1, TOP_K):
        y = y + gate[:, k:k + 1] * buf[slot * TOP_K + k].reshape(x_ref.shape)
    out = x_ref[...] + mod_ref[0][5:6, :] * y
    o_ref[...] = _rms(out, fng_ref[...]) if final else out


def _combine(x2, gates, mod, final_g, dest_tiles, ys, t, tiles_per_batch, final):
    n, d = x2.shape
    return pl.pallas_call(
        functools.partial(_combine_kernel, final=final),
        out_shape=jax.ShapeDtypeStruct((n, d), F32),
        grid=(n // t,),
        in_specs=[
            pl.BlockSpec((t, d), lambda i: (i, 0)),
            pl.BlockSpec((t, LANES), lambda i: (i, 0)),
            pl.BlockSpec((1, 6, d), lambda i: (i // tiles_per_batch, 0, 0)),
            pl.BlockSpec((1, d), lambda i: (0, 0)),
            pl.BlockSpec(memory_space=pl.ANY),
            pl.BlockSpec(memory_space=pl.ANY),
        ],
        out_specs=pl.BlockSpec((t, d), lambda i: (i, 0)),
        scratch_shapes=[pltpu.VMEM((2 * TOP_K, t, d // LANES, LANES), F32), pltpu.SMEM((2 * TOP_K * t,), jnp.int32),
                        pltpu.SemaphoreType.DMA((2,)), pltpu.SemaphoreType.DMA((2,))],
        compiler_params=_cparams(("arbitrary",)),
        name="moe_combine",
    )(x2, gates, mod, final_g.reshape(1, d), dest_tiles, ys)


def _moe_layer(x, routed, mod, layer, w_gu, b_gu, w_dn, b_dn, final_g, final, t):
    b, s, d = x.shape
    n = b * s
    x2 = x.reshape(n, d)
    tiles_per_batch = s // t
    h, gates, meta, cnt = routed
    counts = cnt[0, :N_EXPERTS].astype(jnp.int32)
    padded = (counts + EXPERT_ROWS - 1) // EXPERT_ROWS * EXPERT_ROWS
    pend = jnp.cumsum(padded)
    pstart = pend - padded
    n_blocks = -(-(n * TOP_K) // EXPERT_ROWS) + N_EXPERTS
    p_rows = n_blocks * EXPERT_ROWS
    rank = meta[:, :TOP_K, :]
    eidx = meta[:, TOP_K:, :]
    experts = jnp.arange(N_EXPERTS, dtype=jnp.int32)
    dest = rank + jnp.sum(jnp.where(eidx[..., None] == experts, pstart, 0), axis=-1)
    dest_tiles = dest.reshape(n // t, TOP_K * t)
    block_row0 = jnp.arange(n_blocks, dtype=jnp.int32) * EXPERT_ROWS
    block_e = jnp.minimum(jnp.sum((pend[None, :] <= block_row0[:, None]).astype(jnp.int32), axis=1),
                          N_EXPERTS - 1)
    n_used = (pend[-1:] // EXPERT_ROWS).astype(jnp.int32)
    row_end = jnp.sum(jnp.where(block_e[:, None] == experts, pstart + counts, 0), axis=1)
    n_valid = jnp.clip(row_end - block_row0, 0, EXPERT_ROWS).astype(jnp.int32)
    half = EXPERT_ROWS // 2
    pad_n = (counts + half - 1) // half * half - counts
    xs = _dispatch(h, dest_tiles, pstart + counts, pad_n, p_rows, t)
    block_id = jnp.arange(n_blocks, dtype=jnp.int32)
    later_other = jnp.logical_and(
        jnp.logical_and(block_id[None, :] > block_id[:, None], block_id[None, :] < n_used[0]),
        block_e[None, :] != block_e[:, None])
    first_other = jnp.min(jnp.where(later_other, block_id[None, :], n_blocks), axis=1)
    next_e = jnp.where(first_other < n_blocks, block_e[jnp.minimum(first_other, n_blocks - 1)], -1)
    ys = _experts(xs, block_e, n_valid, n_used, next_e.astype(jnp.int32), layer, w_gu, b_gu, w_dn, b_dn)
    out = _combine(x2, gates, mod, final_g, dest_tiles, ys, t, tiles_per_batch, final)
    return out.reshape(b, s, d)


def _kv_kernel(x_ref, mod_ref, ng_ref, wkv_ref, wf_ref, bf_ref, kg_ref, hsum_ref,
               kt_ref, v_ref, qf_ref, fb_ref, carry):
    t, d = x_ref.shape[1], x_ref.shape[2]
    n_pairs = N_HEADS // 2

    @pl.when(pl.program_id(1) == 0)
    def _():
        carry[...] = jnp.zeros(carry.shape, F32)

    mod = mod_ref[0]
    h = _rms(x_ref[0], ng_ref[...]) * (1.0 + mod[1:2, :]) + mod[0:1, :]
    h_hi = h.astype(BF16)
    kv = jnp.dot(h_hi, wkv_ref[...], preferred_element_type=F32)
    k = kv[:, :d]
    v_ref[0] = kv[:, d:].astype(BF16)
    ms = jnp.dot((k * k).astype(BF16), hsum_ref[...], preferred_element_type=F32)
    k = k * lax.rsqrt(ms + EPS) * kg_ref[...]
    kt = k.T
    for hp in range(n_pairs):
        kt_ref[0, hp, 0, pl.ds(0, LANES), :] = kt[hp * LANES:(hp + 1) * LANES, :].astype(BF16)

    h_lo = (h - h_hi.astype(F32)).astype(BF16)
    fz2 = jnp.dot(h_hi, wf_ref[...], preferred_element_type=F32)
    fz = (fz2[:, :LANES] + fz2[:, LANES:] + bf_ref[...]
          + jnp.dot(h_lo, wf_ref[:, pl.ds(0, LANES)], preferred_element_type=F32))
    ls = jax.nn.log_sigmoid(fz)
    r_i = lax.broadcasted_iota(jnp.int32, (t, t), 0)
    c_i = lax.broadcasted_iota(jnp.int32, (t, t), 1)
    tri = (c_i <= r_i).astype(BF16)
    ls_hi, ls_mid, ls_lo = _split3(ls)
    cum2 = jnp.dot(tri, jnp.concatenate([ls_hi, ls_mid], axis=1).astype(BF16), preferred_element_type=F32)
    cum = (cum2[:, :LANES] + cum2[:, LANES:] + carry[0:1, :]
           + jnp.dot(tri, ls_lo.astype(BF16), preferred_element_type=F32))
    carry[...] = jnp.broadcast_to(cum[t - 1:t, :], carry.shape)
    f2 = cum * LOG2E
    row8 = lax.broadcasted_iota(jnp.int32, (8, LANES), 0)
    fb_ref[0, 0] = jnp.where(row8 == 0, f2[0:1, :], jnp.where(row8 == 1, f2[t - 1:t, :], 0.0))
    f2t = f2.T
    lane = lax.broadcasted_iota(jnp.int32, (t, LANES), 1)
    sub = lax.broadcasted_iota(jnp.int32, (LANES, t), 0)
    q_pieces = _split3(f2)
    k_pieces = _split3(-f2t)
    for hp in range(n_pairs):
        q_aug = jnp.zeros((t, LANES), F32)
        k_aug = jnp.zeros((LANES, t), F32)
        for hh in range(2):
            head = 2 * hp + hh
            o = hh * 2 * N_FPIECES
            for p in range(N_FPIECES):
                q_aug = jnp.where(lane == o + p, q_pieces[p][:, head:head + 1], q_aug)
                q_aug = jnp.where(lane == o + N_FPIECES + p, 1.0, q_aug)
                k_aug = jnp.where(sub == o + p, 1.0, k_aug)
                k_aug = jnp.where(sub == o + N_FPIECES + p, k_pieces[p][head:head + 1, :], k_aug)
        qf_ref[0, hp] = q_aug.astype(BF16)
        kt_ref[0, hp, 0, pl.ds(LANES, LANES), :] = k_aug.astype(BF16)


def _shared_kv(x, kvmod, ng, w_kvf, b_f, k_norm_g, t):
    b, s, d = x.shape
    n_pairs = N_HEADS // 2
    wkv = w_kvf[:, :2 * d].astype(BF16)
    wf = jnp.zeros((d, LANES), F32).at[:, :N_HEADS].set(w_kvf[:, 2 * d:])
    wf_hi = wf.astype(BF16)
    wf2 = jnp.concatenate([wf_hi, (wf - wf_hi.astype(F32)).astype(BF16)], axis=1)
    bf = jnp.zeros((1, LANES), F32).at[0, :N_HEADS].set(b_f)
    kg = jnp.tile(k_norm_g, N_HEADS).reshape(1, d)
    head_of = jnp.arange(d) // HEAD_DIM
    hsum = ((head_of[:, None] == head_of[None, :]).astype(F32) / HEAD_DIM).astype(BF16)
    const = lambda shape: pl.BlockSpec(shape, lambda bi, si: (0,) * len(shape))
    return pl.pallas_call(
        _kv_kernel,
        out_shape=(
            jax.ShapeDtypeStruct((b, n_pairs, s // t, 2 * LANES, t), BF16),
            jax.ShapeDtypeStruct((b, s, d), BF16),
            jax.ShapeDtypeStruct((b, n_pairs, s, LANES), BF16),
            jax.ShapeDtypeStruct((b, s // t, 8, LANES), F32),
        ),
        grid=(b, s // t),
        in_specs=[
            pl.BlockSpec((1, t, d), lambda bi, si: (bi, si, 0)),
            pl.BlockSpec((1, 2, d), lambda bi, si: (bi, 0, 0)),
            const((1, d)), const((d, 2 * d)), const((d, 2 * LANES)), const((1, LANES)), const((1, d)),
            const((d, d)),
        ],
        out_specs=(
            pl.BlockSpec((1, n_pairs, 1, 2 * LANES, t), lambda bi, si: (bi, 0, si, 0, 0)),
            pl.BlockSpec((1, t, d), lambda bi, si: (bi, si, 0)),
            pl.BlockSpec((1, n_pairs, t, LANES), lambda bi, si: (bi, 0, si, 0)),
            pl.BlockSpec((1, 1, 8, LANES), lambda bi, si: (bi, si, 0, 0)),
        ),
        scratch_shapes=[pltpu.VMEM((8, LANES), F32)],
        compiler_params=_cparams(("arbitrary", "arbitrary")),
        name="shared_kv",
    )(x, kvmod, ng.reshape(1, d), wkv, wf2, bf, kg, hsum), hsum


def _qg_kernel(x_ref, mod_ref, ng_ref, w_ref, qg_ref, hsum_ref, q_ref, g_ref):
    d = x_ref.shape[2]
    mod = mod_ref[0]
    h = _rms(x_ref[0], ng_ref[...]) * (1.0 + mod[1:2, :]) + mod[0:1, :]
    qg = jnp.dot(h.astype(BF16), w_ref[...], preferred_element_type=F32)
    q = qg[:, :d]
    ms = jnp.dot((q * q).astype(BF16), hsum_ref[...], preferred_element_type=F32)
    q = q * lax.rsqrt(ms + EPS) * qg_ref[...] * (LOG2E / math.sqrt(HEAD_DIM))
    q_ref[0] = q.astype(BF16)
    g_ref[0] = jax.nn.sigmoid(qg[:, d:]).astype(BF16)


def _qg(x, mod, ng, w_qg, q_norm_g, hsum, t):
    b, s, d = x.shape
    const = lambda shape: pl.BlockSpec(shape, lambda bi, si: (0,) * len(shape))
    tile = pl.BlockSpec((1, t, d), lambda bi, si: (bi, si, 0))
    return pl.pallas_call(
        _qg_kernel,
        out_shape=(jax.ShapeDtypeStruct((b, s, d), BF16), jax.ShapeDtypeStruct((b, s, d), BF16)),
        grid=(b, s // t),
        in_specs=[tile, pl.BlockSpec((1, 6, d), lambda bi, si: (bi, 0, 0)),
                  const((1, d)), const((d, 2 * d)), const((1, d)), const((d, d))],
        out_specs=(tile, tile),
        compiler_params=_cparams(("arbitrary", "arbitrary")),
        name="attn_qg",
    )(x, mod, ng.reshape(1, d), w_qg.astype(BF16), jnp.tile(q_norm_g, N_HEADS).reshape(1, d), hsum)


def _attn_kernel(*refs, online):
    def body(sub, carry):
        _attn_tile(*refs, online=online, sub=sub)
        return carry

    lax.fori_loop(0, Q_TILES_PER_STEP, body, 0)


def _attn_tile(j0_ref, q_ref, qf_ref, kt_ref, v_ref, o_ref, qa_scr, m_scr, l_scr, acc_scr, *, online, sub):
    tq = q_ref.shape[1] // Q_TILES_PER_STEP
    tk = kt_ref.shape[4]
    i = pl.program_id(2) * Q_TILES_PER_STEP + sub
    n_q = pl.num_programs(2) * Q_TILES_PER_STEP
    rows = pl.ds(pl.multiple_of(sub * tq, tq), tq)
    head0 = pl.program_id(0) * N_HEADS + 2 * pl.program_id(1)
    first = [j0_ref[(head0 + hh) * n_q + i] for hh in range(2)]
    first_both = jnp.maximum(first[0], first[1])
    lane = lax.broadcasted_iota(jnp.int32, (tq, LANES), 1)
    q2 = q_ref[0, rows, :]
    qf = qf_ref[0, 0, rows, :]
    zero = jnp.zeros((), BF16)
    n_aug = 2 * N_FPIECES
    qa_scr[0, :, pl.ds(0, LANES)] = jnp.where(lane < HEAD_DIM, q2, zero)
    qa_scr[0, :, pl.ds(LANES, LANES)] = jnp.where(lane < n_aug, qf, zero)
    qa_scr[1, :, pl.ds(0, LANES)] = jnp.where(lane >= HEAD_DIM, q2, zero)
    qa_scr[1, :, pl.ds(LANES, LANES)] = jnp.where(jnp.logical_and(lane >= n_aug, lane < 2 * n_aug), qf, zero)
    if online:
        m_scr[...] = jnp.full(m_scr.shape, NEG_BIG, F32)
    l_scr[...] = jnp.zeros(l_scr.shape, F32)
    acc_scr[...] = jnp.zeros(acc_scr.shape, F32)

    def scores(hh, j, masked):
        s = jnp.dot(qa_scr[hh], kt_ref[0, 0, j], preferred_element_type=F32)
        if masked:
            r_i = lax.broadcasted_iota(jnp.int32, (tq, tk), 0)
            c_i = lax.broadcasted_iota(jnp.int32, (tq, tk), 1)
            s = jnp.where(c_i <= r_i, s, NEG_BIG)
        return s

    def values(j):
        return v_ref[0, pl.ds(pl.multiple_of(j * tk, tk), tk), :]

    def tiles(js, masked, heads):
        for hh in heads:
            if online:
                for j in js:
                    s = scores(hh, j, masked)
                    m_prev = m_scr[hh]
                    m_new = jnp.maximum(m_prev, jnp.max(s, axis=-1, keepdims=True))
                    alpha = jnp.exp2(m_prev - m_new)
                    p = jnp.exp2(s - m_new[:, 0:1])
                    l_scr[hh] = alpha * l_scr[hh] + jnp.sum(p, axis=-1, keepdims=True)
                    acc_scr[hh] = alpha * acc_scr[hh] + jnp.dot(p.astype(BF16), values(j),
                                                                preferred_element_type=F32)
                    m_scr[hh] = m_new
            else:
                part, pv = l_scr[hh], acc_scr[hh]
                for j in js:
                    p = jnp.exp2(scores(hh, j, masked))
                    for c in range(tk // LANES):
                        part = part + p[:, c * LANES:(c + 1) * LANES]
                    pv = pv + jnp.dot(p.astype(BF16), values(j), preferred_element_type=F32)
                l_scr[hh], acc_scr[hh] = part, pv

    def one_tile(heads):
        def body(j, carry):
            tiles((j,), False, heads)
            return carry
        return body

    def tile_group(group, carry):
        j = first_both + KV_UNROLL * group
        tiles(tuple(j + u for u in range(KV_UNROLL)), False, (0, 1))
        return carry

    for hh in range(2):
        n_quads_hh = (first_both - first[hh]) // 4

        def tile_quad(quad, carry, hh=hh):
            j = first[hh] + 4 * quad
            tiles((j, j + 1, j + 2, j + 3), False, (hh,))
            return carry

        lax.fori_loop(0, n_quads_hh, tile_quad, 0)
        lax.fori_loop(first[hh] + 4 * n_quads_hh, first_both, one_tile((hh,)), 0)
    n_groups = (i - first_both) // KV_UNROLL
    lax.fori_loop(0, n_groups, tile_group, 0)
    rest = first_both + KV_UNROLL * n_groups
    n_rest_pairs = (i - rest) // 2

    def rest_pair(pair, carry):
        j = rest + 2 * pair
        tiles((j, j + 1), False, (0, 1))
        return carry

    lax.fori_loop(0, n_rest_pairs, rest_pair, 0)
    lax.fori_loop(rest + 2 * n_rest_pairs, i, one_tile((0, 1)), 0)
    tiles((i,), True, (0, 1))
    if online:
        l0, l1 = l_scr[0], l_scr[1]
    else:
        l0 = jnp.sum(l_scr[0], axis=-1, keepdims=True)
        l1 = jnp.sum(l_scr[1], axis=-1, keepdims=True)
    o_ref[0, rows, :] = jnp.where(lane < HEAD_DIM, acc_scr[0] / l0, acc_scr[1] / l1).astype(BF16)


def _attention(j0, q, qf, kt, v, t, online):
    b, s, d = q.shape
    n_pairs = N_HEADS // 2
    nkv = s // t
    tq_step = Q_TILES_PER_STEP * t
    return pl.pallas_call(
        functools.partial(_attn_kernel, online=online),
        out_shape=jax.ShapeDtypeStruct((b, s, d), BF16),
        grid_spec=pltpu.PrefetchScalarGridSpec(
            num_scalar_prefetch=1,
            grid=(b, n_pairs, s // tq_step),
            in_specs=[
                pl.BlockSpec((1, tq_step, LANES), lambda bi, hp, i, j0r: (bi, i, hp)),
                pl.BlockSpec((1, 1, tq_step, LANES), lambda bi, hp, i, j0r: (bi, hp, i, 0)),
                pl.BlockSpec((1, 1, nkv, 2 * LANES, t), lambda bi, hp, i, j0r: (bi, hp, 0, 0, 0)),
                pl.BlockSpec((1, s, LANES), lambda bi, hp, i, j0r: (bi, 0, hp)),
            ],
            out_specs=pl.BlockSpec((1, tq_step, LANES), lambda bi, hp, i, j0r: (bi, i, hp)),
            scratch_shapes=[
                pltpu.VMEM((2, t, 2 * LANES), BF16),
                pltpu.VMEM((2, t, LANES), F32),
                pltpu.VMEM((2, t, LANES), F32),
                pltpu.VMEM((2, t, LANES), F32),
            ],
        ),
        compiler_params=_cparams(("arbitrary", "arbitrary", "arbitrary")),
        name="fox_attention_online" if online else "fox_attention",
    )(j0, q, qf, kt, v)


def _attn_out_kernel(x_ref, o_ref, g_ref, mod_ref, w_ref, ng2_ref, rw_ref, rb_ref,
                     out_ref, h_ref, gate_ref, meta_ref, cnt_ref):
    mod = mod_ref[0]
    og = o_ref[0] * g_ref[0]
    y = jnp.dot(og, w_ref[...], preferred_element_type=F32)
    x_new = x_ref[0] + mod[2:3, :] * y
    out_ref[0] = x_new
    _route_tail(x_new, mod, ng2_ref, rw_ref, rb_ref, h_ref, gate_ref, meta_ref, cnt_ref)


def _attn_out(x, o, g, mod, w_o, route, t):
    b, s, d = x.shape
    tile = pl.BlockSpec((1, t, d), lambda bi, si: (bi, si, 0))
    r_ops, r_in, r_shapes, r_out = _route_plumbing(*route, b, s, d, t)
    outs = pl.pallas_call(
        _attn_out_kernel,
        out_shape=(jax.ShapeDtypeStruct((b, s, d), F32),) + r_shapes,
        grid=(b, s // t),
        in_specs=[tile, tile, tile, pl.BlockSpec((1, 6, d), lambda bi, si: (bi, 0, 0)),
                  pl.BlockSpec((d, d), lambda bi, si: (0, 0))] + r_in,
        out_specs=(tile,) + r_out,
        compiler_params=_cparams(("arbitrary", "arbitrary")),
        name="attn_out",
    )(x, o, g, mod, w_o.astype(BF16), *r_ops)
    return outs[0], outs[1:]


def _fox_layer(x, mod, ng, w_qg, q_norm_g, k_norm_g, w_o, kv, route, t):
    (kt, v, qf, fb), hsum = kv
    n_t = x.shape[1] // t
    q, g = _qg(x, mod, ng, w_qg, q_norm_g, hsum, t)
    bound = (HEAD_DIM * jnp.max(jnp.abs(q_norm_g)) * jnp.max(jnp.abs(k_norm_g))
             * (LOG2E / math.sqrt(HEAD_DIM)))
    f_first, f_last = fb[:, :, 0, :N_HEADS], fb[:, :, 1, :N_HEADS]
    best = (bound * BOUND_SLACK + f_first[:, :, None, :]) - f_last[:, None, :, :]
    before = jnp.arange(n_t)[None, :] < jnp.arange(n_t)[:, None]
    dead = jnp.logical_and(best < ZERO_WEIGHT_EXPONENT, before[None, :, :, None])
    j0 = jnp.sum(dead.astype(jnp.int32), axis=2).transpose(0, 2, 1).reshape(-1)
    o = lax.cond(bound <= DIRECT_EXP_LIMIT,
                 functools.partial(_attention, t=t, online=False),
                 lambda j0_, *rest: _attention(jnp.zeros_like(j0_), *rest, t=t, online=True),
                 j0, q, qf, kt, v)
    return _attn_out(x, o, g, mod, w_o, route, t)


def kernel(x, c, mod_w, mod_b, norm1_g, norm2_g, conv_w_pw1, conv_b_pw1, conv_w_dw, conv_b_dw, conv_ln_g, conv_ln_b, conv_w_pw2, conv_b_pw2, kv_mod_w, kv_mod_b, kv_norm_g, w_kvf, b_f, k_norm_g, attn_w_qg, q_norm_g, attn_w_o, moe_router_w, moe_router_b, moe_w_gu, moe_b_gu, moe_w_down, moe_b_down, final_norm_g):
    b, s, d = x.shape
    depth = mod_w.shape[0]
    n_a = conv_w_pw1.shape[0]
    t = min(512, s)
    c8 = jnp.zeros((8, d), F32).at[:b].set(c)
    mods = _mods(c8, mod_w, mod_b)[:, :b].reshape(depth, b, 6, d)
    kvmod = _mods(c8, kv_mod_w[None], kv_mod_b[None])[0, :b].reshape(b, 2, d)
    kv = None
    for l in range(depth):
        route = (norm2_g[l], moe_router_w[l], moe_router_b[l])
        if l < n_a:
            x, routed = _conv_layer(x, mods[l], norm1_g[l], conv_w_pw1[l], conv_b_pw1[l], conv_w_dw[l],
                                    conv_b_dw[l], conv_ln_g[l], conv_ln_b[l], conv_w_pw2[l],
                                    conv_b_pw2[l], route, t)
        else:
            lb = l - n_a
            x, routed = _fox_layer(x, mods[l], norm1_g[l], attn_w_qg[lb], q_norm_g[lb], k_norm_g,
                                   attn_w_o[lb], kv, route, t)
        x = _moe_layer(x, routed, mods[l], l, moe_w_gu, moe_b_gu, moe_w_down, moe_b_down,
                       final_norm_g, l == depth - 1, t)
        if l == n_a - 1:
            kv = _shared_kv(x, kvmod, kv_norm_g, w_kvf, b_f, k_norm_g, t)
    return x
```
